```python
import math
import jax
import jax.numpy as jnp
from jax import lax
import numpy as np

D_MODEL = 4096
BATCH = 2
SEQ = 8192
DEPTH = 2

HEAD_DIM = 128
ROT_DIM = HEAD_DIM // 4
ROPE_THETA = 500000.0
ATTN_SCALE = HEAD_DIM ** -0.5
NEG_INF = -1e30
LN_EPS = 1e-5

NSA_HEADS = D_MODEL // 256
NSA_KV_HEADS = NSA_HEADS // 4
NSA_GROUP = NSA_HEADS // NSA_KV_HEADS
CMP_LEN = 32
CMP_STRIDE = 16
SEL_LEN = 64
SEL_TOPN = 16
WINDOW = 512
NSA_CHUNK = 64

DIFF_HEADS = D_MODEL // 1024
DIFF_VDIM = 2 * HEAD_DIM
DENSE_CHUNK = 128

MOBA_HEADS = D_MODEL // 512
MOBA_BLOCK = 256
MOBA_TOPK = 3
MOBA_CHUNK = 32

N_EXPERTS = 32
TOP_K = 4
D_EXPERT = D_MODEL // 8
SWIGLU_LIMIT = 7.0
SWIGLU_ALPHA = 1.702
MOE_BLOCK = 512

DEEPNORM_ALPHA = (2 * DEPTH) ** 0.25
DEEPNORM_BETA = (8 * DEPTH) ** -0.25

COL_SIZES = ((NSA_HEADS * HEAD_DIM,) + (NSA_KV_HEADS * HEAD_DIM,) * 6 + (3 * NSA_HEADS,)
             + (2 * DIFF_HEADS * HEAD_DIM,) * 2 + (DIFF_HEADS * DIFF_VDIM,)
             + (MOBA_HEADS * HEAD_DIM,) * 3)
IN_COLS = sum(COL_SIZES)
MIX_WIDTH = NSA_HEADS * HEAD_DIM + DIFF_HEADS * DIFF_VDIM + MOBA_HEADS * HEAD_DIM

kernel_name = 'hybrid_nsa_diff_moba_moe_deepnorm'


def layer_norm(x, g, b):
    xf = x.astype(jnp.float32)
    mu = jnp.mean(xf, -1, keepdims=True)
    var = jnp.mean(jnp.square(xf - mu), -1, keepdims=True)
    return ((xf - mu) * lax.rsqrt(var + LN_EPS) * g + b).astype(x.dtype)


def masked_softmax(s, mask):
    s = jnp.where(mask, s, NEG_INF)
    p = jnp.exp(s - jnp.max(s, -1, keepdims=True)) * mask
    return p / jnp.maximum(jnp.sum(p, -1, keepdims=True), 1e-30)


def rope_tables(positions):
    inv_freq = 1.0 / (ROPE_THETA ** (jnp.arange(0, ROT_DIM, 2, dtype=jnp.float32) / ROT_DIM))
    ang = positions.astype(jnp.float32)[..., None] * inv_freq
    return jnp.cos(ang), jnp.sin(ang)


def apply_rope(x, cos, sin):
    half = ROT_DIM // 2
    c = cos[:, None].astype(x.dtype)
    s = sin[:, None].astype(x.dtype)
    x1 = x[..., :half]
    x2 = x[..., half:ROT_DIM]
    return jnp.concatenate([x1 * c - x2 * s, x2 * c + x1 * s, x[..., ROT_DIM:]], -1)


def to_heads(t, n_heads, dim):
    b, s, _ = t.shape
    return t.reshape(b, s, n_heads, dim).transpose(0, 2, 1, 3)


def split_columns(h):
    offsets = np.cumsum(np.array(COL_SIZES))[:-1].tolist()
    return jnp.split(h, offsets, axis=-1)


def nsa_compress(k, pos_emb, w1, w2):
    b, hk, s, d = k.shape
    n_cmp = (s - CMP_LEN) // CMP_STRIDE + 1
    idx = np.arange(n_cmp)[:, None] * CMP_STRIDE + np.arange(CMP_LEN)[None, :]
    blocks = k[:, :, idx] + pos_emb.astype(k.dtype)
    flat = blocks.reshape(b, hk, n_cmp, CMP_LEN * d)
    return jax.nn.gelu(flat @ w1) @ w2


def nsa_attention(q, k_cmp, v_cmp, k_slc, v_slc, k_win, v_win, gate_logits,
                  cmp_pos, cmp_w1, cmp_w2, cos, sin):
    b, s, _ = q.shape
    hk, g, dh = NSA_KV_HEADS, NSA_GROUP, HEAD_DIM
    qg = apply_rope(to_heads(q, NSA_HEADS, dh), cos, sin).reshape(b, hk, g, s, dh)
    gates = jax.nn.sigmoid(gate_logits.astype(jnp.float32)).reshape(b, s, hk, g, 3)
    gates = gates.transpose(0, 2, 3, 1, 4).astype(q.dtype)

    kc = nsa_compress(apply_rope(to_heads(k_cmp, hk, dh), cos, sin), cmp_pos[0], cmp_w1[0], cmp_w2[0])
    vc = nsa_compress(to_heads(v_cmp, hk, dh), cmp_pos[1], cmp_w1[1], cmp_w2[1])
    n_cmp = kc.shape[2]
    cmp_end = np.arange(n_cmp) * CMP_STRIDE + CMP_LEN - 1

    n_sel = s // SEL_LEN
    n_sel_pad = max(n_sel, SEL_TOPN)
    ratio = SEL_LEN // CMP_STRIDE
    cover = np.arange(n_sel)[:, None] * ratio + np.arange(1 - CMP_LEN // CMP_STRIDE, ratio)[None, :]
    cover_ok = (cover >= 0) & (cover < n_cmp)
    cover = np.clip(cover, 0, n_cmp - 1)

    ks_blocks = apply_rope(to_heads(k_slc, hk, dh), cos, sin).reshape(b, hk, n_sel, SEL_LEN, dh)
    vs_blocks = to_heads(v_slc, hk, dh).reshape(b, hk, n_sel, SEL_LEN, dh)
    pad_w = ((0, 0), (0, 0), (WINDOW, 0), (0, 0))
    kw_all = jnp.pad(apply_rope(to_heads(k_win, hk, dh), cos, sin), pad_w)
    vw_all = jnp.pad(to_heads(v_win, hk, dh), pad_w)

    bi = jnp.arange(b)[:, None, None, None]
    hi = jnp.arange(hk)[None, :, None, None]
    blk = jnp.arange(n_sel_pad)
    c = NSA_CHUNK

    def chunk(ci):
        s0 = ci * c
        t = s0 + jnp.arange(c)
        qc = lax.dynamic_slice_in_dim(qg, s0, c, axis=3)
        gc = lax.dynamic_slice_in_dim(gates, s0, c, axis=3)

        sc = jnp.einsum('bkgcd,bknd->bkgcn', qc, kc).astype(jnp.float32) * ATTN_SCALE
        pc = masked_softmax(sc, cmp_end[None, :] <= t[:, None])
        o_cmp = jnp.einsum('bkgcn,bknd->bkgcd', pc.astype(vc.dtype), vc)

        p_grp = jnp.sum(pc, axis=2)
        imp = jnp.sum(jnp.where(cover_ok, p_grp[..., cover], 0.0), -1)
        imp = jnp.pad(imp, ((0, 0), (0, 0), (0, 0), (0, n_sel_pad - n_sel)))
        cur = t // SEL_LEN
        forced = (blk[None] == 0) | (blk[None] == cur[:, None]) | (blk[None] == cur[:, None] - 1)
        imp = jnp.where(forced, jnp.inf, imp)
        imp = jnp.where(blk[None] <= cur[:, None], imp, -jnp.inf)
        _, sel = lax.top_k(imp, SEL_TOPN)
        sel_ok = jnp.arange(SEL_TOPN)[None, :] < jnp.minimum(cur + 1, SEL_TOPN)[:, None]
        sel = jnp.minimum(sel, n_sel - 1)
        ks = ks_blocks[bi, hi, sel]
        vs = vs_blocks[bi, hi, sel]
        kpos = sel[..., None] * SEL_LEN + jnp.arange(SEL_LEN)
        m_sel = sel_ok[:, :, None] & (kpos <= t[:, None, None])
        ss = jnp.einsum('bkgcd,bkcnld->bkgcnl', qc, ks).astype(jnp.float32) * ATTN_SCALE
        ps = masked_softmax(ss.reshape(b, hk, g, c, SEL_TOPN * SEL_LEN),
                            m_sel.reshape(b, hk, 1, c, SEL_TOPN * SEL_LEN))
        o_slc = jnp.einsum('bkgcm,bkcmd->bkgcd', ps.astype(vs.dtype),
                           vs.reshape(b, hk, c, SEL_TOPN * SEL_LEN, dh))

        kw = lax.dynamic_slice_in_dim(kw_all, s0, WINDOW + c, axis=2)
        vw = lax.dynamic_slice_in_dim(vw_all, s0, WINDOW + c, axis=2)
        wpos = s0 - WINDOW + jnp.arange(WINDOW + c)
        m_win = (wpos[None] <= t[:, None]) & (wpos[None] > t[:, None] - WINDOW) & (wpos[None] >= 0)
        sw = jnp.einsum('bkgcd,bkjd->bkgcj', qc, kw).astype(jnp.float32) * ATTN_SCALE
        pw = masked_softmax(sw, m_win)
        o_win = jnp.einsum('bkgcj,bkjd->bkgcd', pw.astype(vw.dtype), vw)

        return gc[..., 0:1] * o_cmp + gc[..., 1:2] * o_slc + gc[..., 2:3] * o_win

    out = lax.map(chunk, jnp.arange(s // c))
    return out.transpose(1, 0, 4, 2, 3, 5).reshape(b, s, NSA_HEADS * dh)


def diff_attention(q, k, v, lam_vecs, subln_g, lambda_init, cos, sin):
    b, s, _ = q.shape
    dh = HEAD_DIM
    qh = apply_rope(to_heads(q, 2 * DIFF_HEADS, dh), cos, sin).reshape(b, DIFF_HEADS, 2, s, dh)
    kh = apply_rope(to_heads(k, 2 * DIFF_HEADS, dh), cos, sin).reshape(b, DIFF_HEADS, 2, s, dh)
    vh = to_heads(v, DIFF_HEADS, DIFF_VDIM)
    lv = lam_vecs.astype(jnp.float32)
    lam = jnp.exp(jnp.sum(lv[0] * lv[1])) - jnp.exp(jnp.sum(lv[2] * lv[3])) + lambda_init
    kpos = jnp.arange(s)
    cd = DENSE_CHUNK

    def block(ci):
        s0 = ci * cd
        t = s0 + jnp.arange(cd)
        qb = lax.dynamic_slice_in_dim(qh, s0, cd, axis=3)
        sc = jnp.einsum('bhmcd,bhmjd->bhmcj', qb, kh).astype(jnp.float32) * ATTN_SCALE
        p = masked_softmax(sc, kpos[None, :] <= t[:, None])
        a = p[:, :, 0] - lam * p[:, :, 1]
        return jnp.einsum('bhcj,bhjd->bhcd', a.astype(vh.dtype), vh)

    o = lax.map(block, jnp.arange(s // cd))
    o = o.transpose(1, 0, 3, 2, 4).reshape(b, s, DIFF_HEADS, DIFF_VDIM).astype(jnp.float32)
    o = o * lax.rsqrt(jnp.mean(jnp.square(o), -1, keepdims=True) + LN_EPS) * subln_g
    o = o * (1.0 - lambda_init)
    return o.astype(q.dtype).reshape(b, s, DIFF_HEADS * DIFF_VDIM)


def moba_attention(q, k, v, cos, sin):
    b, s, _ = q.shape
    h, dh, bl = MOBA_HEADS, HEAD_DIM, MOBA_BLOCK
    qh = apply_rope(to_heads(q, h, dh), cos, sin)
    kh = apply_rope(to_heads(k, h, dh), cos, sin)
    vh = to_heads(v, h, dh)
    n_blk = max(-(-s // bl), MOBA_TOPK)
    pad = ((0, 0), (0, 0), (0, n_blk * bl - s), (0, 0))
    kp = jnp.pad(kh, pad)
    vp = jnp.pad(vh, pad)
    kb = kp.reshape(b, h, n_blk, bl, dh)
    vb = vp.reshape(b, h, n_blk, bl, dh)
    k_mean = jnp.mean(kb.astype(jnp.float32), axis=3).astype(kh.dtype)
    bi = jnp.arange(b)[:, None, None, None]
    hi = jnp.arange(h)[None, :, None, None]
    blk = jnp.arange(n_blk)
    c = MOBA_CHUNK

    def chunk(ci):
        s0 = ci * c
        t = s0 + jnp.arange(c)
        j = s0 // bl
        qc = lax.dynamic_slice_in_dim(qh, s0, c, axis=2)
        score = jnp.einsum('bhcd,bhnd->bhcn', qc, k_mean).astype(jnp.float32)
        score = jnp.where(blk < j, score, -jnp.inf)
        _, sel = lax.top_k(score, MOBA_TOPK)
        sel_ok = jnp.arange(MOBA_TOPK) < j
        ks = kb[bi, hi, sel]
        vs = vb[bi, hi, sel]
        ko = lax.dynamic_slice_in_dim(kp, j * bl, bl, axis=2)
        vo = lax.dynamic_slice_in_dim(vp, j * bl, bl, axis=2)
        opos = j * bl + jnp.arange(bl)
        s_sel = jnp.einsum('bhcd,bhcnld->bhcnl', qc, ks).reshape(b, h, c, MOBA_TOPK * bl)
        s_own = jnp.einsum('bhcd,bhld->bhcl', qc, ko)
        sc = jnp.concatenate([s_sel, s_own], -1).astype(jnp.float32) * ATTN_SCALE
        mask = jnp.concatenate([
            jnp.broadcast_to(jnp.repeat(sel_ok, bl)[None, :], (c, MOBA_TOPK * bl)),
            opos[None, :] <= t[:, None]], -1)
        p = masked_softmax(sc, mask).astype(vh.dtype)
        o = jnp.einsum('bhcm,bhcmd->bhcd', p[..., :MOBA_TOPK * bl], vs.reshape(b, h, c, MOBA_TOPK * bl, dh))
        return o + jnp.einsum('bhcl,bhld->bhcd', p[..., MOBA_TOPK * bl:], vo)

    out = lax.map(chunk, jnp.arange(s // c))
    return out.transpose(1, 0, 3, 2, 4).reshape(b, s, h * dh)


def clamped_swiglu(hid):
    h_glu = jnp.minimum(hid[..., ::2], SWIGLU_LIMIT)
    h_lin = jnp.clip(hid[..., 1::2], -SWIGLU_LIMIT, SWIGLU_LIMIT)
    return h_glu * jax.nn.sigmoid(SWIGLU_ALPHA * h_glu) * (h_lin + 1.0)


def moe_ffn(x, w_router, b_router, w_gate_up, b_gate_up, w_down, b_down):
    n_tok, d = x.shape
    logits = (x @ w_router + b_router).astype(jnp.float32)
    top_logit, top_idx = lax.top_k(logits, TOP_K)
    gate = jax.nn.softmax(top_logit, axis=-1)
    n_assign = n_tok * TOP_K
    expert = top_idx.reshape(-1)
    token = jnp.arange(n_assign, dtype=jnp.int32) // TOP_K
    order = jnp.argsort(expert)
    expert_sorted = expert[order]
    counts = jnp.bincount(expert, length=N_EXPERTS)
    start = jnp.cumsum(counts) - counts
    padded = (counts + MOE_BLOCK - 1) // MOE_BLOCK * MOE_BLOCK
    pad_end = jnp.cumsum(padded)
    pad_start = pad_end - padded
    dest = pad_start[expert_sorted] + jnp.arange(n_assign) - start[expert_sorted]
    n_blocks = -(-n_assign // MOE_BLOCK) + N_EXPERTS
    n_rows = n_blocks * MOE_BLOCK
    row_token = jnp.full((n_rows,), n_tok, jnp.int32).at[dest].set(token[order])
    row_gate = jnp.zeros((n_rows,), jnp.float32).at[dest].set(gate.reshape(-1)[order])
    block_expert = jnp.minimum(
        jnp.searchsorted(pad_end, jnp.arange(n_blocks) * MOE_BLOCK, side='right'), N_EXPERTS - 1)
    x_pad = jnp.concatenate([x, jnp.zeros((1, d), x.dtype)], 0)

    def expert_block(args):
        rows, e = args
        hid = x_pad[rows] @ w_gate_up[e] + b_gate_up[e]
        return clamped_swiglu(hid) @ w_down[e] + b_down[e]

    out = lax.map(expert_block, (row_token.reshape(n_blocks, MOE_BLOCK), block_expert))
    out = out.reshape(n_rows, d) * row_gate[:, None].astype(out.dtype)
    return jnp.zeros((n_tok + 1, d), out.dtype).at[row_token].add(out)[:n_tok]


def setup_inputs(seed: int = 0) -> dict:
    key = jax.random.key(seed)
    k = jax.random.split(key, 19)
    f32 = jnp.float32
    L, D, E, F = DEPTH, D_MODEL, N_EXPERTS, D_EXPERT

    def normal(kk, shape, scale):
        return jax.random.normal(kk, shape, f32) * scale

    return {
        'x': normal(k[0], (BATCH, SEQ, D), 1.0),
        'positions': jnp.broadcast_to(jnp.arange(SEQ, dtype=jnp.int32)[None, :], (BATCH, SEQ)),
        'w_in': normal(k[1], (L, D, IN_COLS), D ** -0.5),
        'nsa_cmp_pos': normal(k[2], (L, 2, CMP_LEN, HEAD_DIM), 0.02),
        'nsa_cmp_w1': normal(k[3], (L, 2, CMP_LEN * HEAD_DIM, HEAD_DIM), (CMP_LEN * HEAD_DIM) ** -0.5),
        'nsa_cmp_w2': normal(k[4], (L, 2, HEAD_DIM, HEAD_DIM), HEAD_DIM ** -0.5),
        'diff_lambda': normal(k[5], (L, 4, HEAD_DIM), 0.1),
        'diff_subln_g': 1.0 + normal(k[6], (L, DIFF_VDIM), 0.02),
        'w_out': normal(k[7], (L, MIX_WIDTH, D), MIX_WIDTH ** -0.5 * DEEPNORM_BETA),
        'ln1_g': 1.0 + normal(k[8], (L, D), 0.02),
        'ln1_b': normal(k[9], (L, D), 0.02),
        'w_router': normal(k[10], (L, D, E), D ** -0.5),
        'b_router': normal(k[11], (L, E), 0.01),
        'w_gate_up': normal(k[12], (L, E, D, 2 * F), D ** -0.5),
        'b_gate_up': normal(k[13], (L, E, 2 * F), 0.01),
        'w_down': normal(k[14], (L, E, F, D), F ** -0.5 * DEEPNORM_BETA),
        'b_down': normal(k[15], (L, E, D), 0.01),
        'ln2_g': 1.0 + normal(k[16], (L, D), 0.02),
        'ln2_b': normal(k[17], (L, D), 0.02),
    }


def reference(x, positions, w_in, nsa_cmp_pos, nsa_cmp_w1, nsa_cmp_w2, diff_lambda, diff_subln_g,
              w_out, ln1_g, ln1_b, w_router, b_router, w_gate_up, b_gate_up, w_down, b_down,
              ln2_g, ln2_b):
    b, s, d = x.shape
    cos, sin = rope_tables(positions)
    for layer in range(DEPTH):
        h = x @ w_in[layer]
        (nq, nkc, nvc, nks, nvs, nkw, nvw, ngate,
         dq, dk, dv, mq, mk, mv) = split_columns(h)
        y_nsa = nsa_attention(nq, nkc, nvc, nks, nvs, nkw, nvw, ngate,
                              nsa_cmp_pos[layer], nsa_cmp_w1[layer], nsa_cmp_w2[layer], cos, sin)
        lambda_init = 0.8 - 0.6 * math.exp(-0.3 * layer)
        y_diff = diff_attention(dq, dk, dv, diff_lambda[layer], diff_subln_g[layer], lambda_init, cos, sin)
        y_moba = moba_attention(mq, mk, mv, cos, sin)
        mix = jnp.concatenate([y_nsa, y_diff, y_moba], -1) @ w_out[layer]
        x = layer_norm(DEEPNORM_ALPHA * x + mix, ln1_g[layer], ln1_b[layer])
        ffn = moe_ffn(x.reshape(b * s, d), w_router[layer], b_router[layer], w_gate_up[layer],
                      b_gate_up[layer], w_down[layer], b_down[layer]).reshape(b, s, d)
        x = layer_norm(DEEPNORM_ALPHA * x + ffn, ln2_g[layer], ln2_b[layer])
    return x
```

```python
import functools
import math

import numpy as np
import jax
import jax.numpy as jnp
from jax import lax
from jax.experimental import pallas as pl
from jax.experimental.pallas import tpu as pltpu

HEAD_DIM = 128
ROT_DIM = HEAD_DIM // 4
ROT_HALF = ROT_DIM // 2
ROPE_THETA = 500000.0
ATTN_SCALE = HEAD_DIM ** -0.5
NEG_INF = -1e30
LN_EPS = 1e-5

NSA_GROUP = 4
CMP_LEN = 32
CMP_STRIDE = 16
SEL_LEN = 64
SEL_TOPN = 16
WINDOW = 512
CMP_PER_SEL = SEL_LEN // CMP_STRIDE

DIFF_VDIM = 2 * HEAD_DIM
MOBA_BLOCK = 256
MOBA_TOPK = 3

TOP_K = 4
SWIGLU_LIMIT = 7.0
SWIGLU_ALPHA = 1.702

LANES = 128
MXU_DTYPE = jnp.bfloat16
VMEM_LIMIT = 56 * 1024 * 1024

ROPED = ('nq', 'nkc', 'nks', 'nkw', 'dq', 'dk', 'mq', 'mk')
PLAIN = ('nvc', 'nvs', 'nvw', 'dv', 'mv')
REF_ORDER = ('nq', 'nkc', 'nvc', 'nks', 'nvs', 'nkw', 'nvw', 'ngate', 'dq', 'dk', 'dv', 'mq', 'mk', 'mv')


def _layout(d):
    nsa_h = d // 256
    kv = nsa_h // NSA_GROUP
    diff_h = d // 1024
    moba_h = d // 512
    size = dict(nq=nsa_h * HEAD_DIM, nkc=kv * HEAD_DIM, nvc=kv * HEAD_DIM, nks=kv * HEAD_DIM,
                nvs=kv * HEAD_DIM, nkw=kv * HEAD_DIM, nvw=kv * HEAD_DIM, ngate=3 * nsa_h,
                dq=2 * diff_h * HEAD_DIM, dk=2 * diff_h * HEAD_DIM, dv=diff_h * DIFF_VDIM,
                mq=moba_h * HEAD_DIM, mk=moba_h * HEAD_DIM, mv=moba_h * HEAD_DIM)
    ref_off, o = {}, 0
    for n in REF_ORDER:
        ref_off[n] = o
        o += size[n]
    off, o = {}, 0
    for n in ROPED + PLAIN:
        off[n] = o
        o += size[n]
    n_roped = sum(size[n] for n in ROPED)
    return dict(nsa_h=nsa_h, kv=kv, diff_h=diff_h, moba_h=moba_h, size=size, ref_off=ref_off,
                off=off, n_roped=n_roped, n_cols=o)


def _params(*sem):
    return pltpu.CompilerParams(dimension_semantics=sem, vmem_limit_bytes=VMEM_LIMIT)


def _dot(a, b):
    return jnp.dot(a, b, preferred_element_type=jnp.float32)


def _dot_nt(a, b):
    return lax.dot_general(a, b, (((1,), (1,)), ((), ())), preferred_element_type=jnp.float32)


def _iota(shape, dim):
    return lax.broadcasted_iota(jnp.int32, shape, dim)


def _first_lane(hit, lane):
    first = jnp.min(jnp.where(hit, lane.astype(jnp.float32), float(LANES)), axis=-1, keepdims=True)
    return first.astype(jnp.int32)


def _proj_kernel(x_ref, w_ref, ct_ref, sa_ref, sb_ref, o_ref, *, n_rope_blocks, tn):
    j = pl.program_id(1)
    acc = _dot(x_ref[...], w_ref[...])

    @pl.when(j < n_rope_blocks)
    def _():
        ct, sa, sb = ct_ref[...], sa_ref[...], sb_ref[...]
        for c in range(tn // HEAD_DIM):
            a = acc[:, c * HEAD_DIM:(c + 1) * HEAD_DIM]
            r = (a * ct + pltpu.roll(a, ROT_HALF, 1) * sa
                 + pltpu.roll(a, HEAD_DIM - ROT_HALF, 1) * sb)
            o_ref[:, c * HEAD_DIM:(c + 1) * HEAD_DIM] = r.astype(o_ref.dtype)

    @pl.when(j >= n_rope_blocks)
    def _():
        o_ref[...] = acc.astype(o_ref.dtype)


def _project(xb, w, ct, sa, sb, n_roped):
    m, k = xb.shape
    n = w.shape[1]
    tm = min(1024, m)
    tn = next(t for t in (512, 256, 128) if n % t == 0 and n_roped % t == 0)
    kern = functools.partial(_proj_kernel, n_rope_blocks=n_roped // tn, tn=tn)
    return pl.pallas_call(
        kern,
        out_shape=jax.ShapeDtypeStruct((m, n), MXU_DTYPE),
        grid=(m // tm, n // tn),
        in_specs=[pl.BlockSpec((tm, k), lambda i, j: (i, 0)),
                  pl.BlockSpec((k, tn), lambda i, j: (0, j)),
                  pl.BlockSpec((tm, HEAD_DIM), lambda i, j: (i, 0)),
                  pl.BlockSpec((tm, HEAD_DIM), lambda i, j: (i, 0)),
                  pl.BlockSpec((tm, HEAD_DIM), lambda i, j: (i, 0))],
        out_specs=pl.BlockSpec((tm, tn), lambda i, j: (i, j)),
        compiler_params=_params("parallel", "arbitrary"),
    )(xb, w, ct, sa, sb)


def _gate_kernel(x_ref, w_ref, o_ref):
    o_ref[...] = _dot(x_ref[...], w_ref[...])


def _gate_logits(xb, wg):
    m, k = xb.shape
    n = wg.shape[1]
    tm = min(1024, m)
    return pl.pallas_call(
        _gate_kernel,
        out_shape=jax.ShapeDtypeStruct((m, n), jnp.float32),
        grid=(m // tm,),
        in_specs=[pl.BlockSpec((tm, k), lambda i: (i, 0)),
                  pl.BlockSpec((k, n), lambda i: (0, 0))],
        out_specs=pl.BlockSpec((tm, n), lambda i: (i, 0)),
        compiler_params=_params("parallel"),
    )(xb, wg)


def _flash_step(s, mask, v, m_ref, l_ref, acc_ref, idx):
    m_old = m_ref[idx]
    if mask is not None:
        s = jnp.where(mask, s, NEG_INF)
    m_new = jnp.maximum(m_old, jnp.max(s, axis=-1, keepdims=True))
    p = jnp.exp(s - m_new)
    if mask is not None:
        p = jnp.where(mask, p, 0.0)
    alpha = jnp.exp(m_old - m_new)
    l_ref[idx] = alpha * l_ref[idx] + jnp.sum(p, axis=-1, keepdims=True)
    acc_ref[idx] = alpha * acc_ref[idx] + _dot(p.astype(v.dtype), v)
    m_ref[idx] = m_new


def _init_state(m_ref, l_ref, acc_ref):
    m_ref[...] = jnp.full(m_ref.shape, NEG_INF, jnp.float32)
    l_ref[...] = jnp.zeros(l_ref.shape, jnp.float32)
    acc_ref[...] = jnp.zeros(acc_ref.shape, jnp.float32)


def _causal_pairs(nq, tq, tk):
    qi, ki = [], []
    for q in range(nq):
        last = (q * tq + tq - 1) // tk
        for k in range(last + 1):
            qi.append(q)
            ki.append(k)
    return np.asarray(qi, np.int32), np.asarray(ki, np.int32)


def _diff_kernel(qi_tab, ki_tab, lam_ref, g_ref, q_ref, k_ref, v_ref, o_ref, m_ref, l_ref, acc_ref,
                 *, tq, tk, lambda_init):
    p_id = pl.program_id(2)
    qi = qi_tab[p_id]
    ki = ki_tab[p_id]
    last = (qi * tq + tq - 1) // tk

    @pl.when(ki == 0)
    def _():
        _init_state(m_ref, l_ref, acc_ref)

    def step(masked):
        mask = None
        if masked:
            t = qi * tq + _iota((tq, tk), 0)
            u = ki * tk + _iota((tq, tk), 1)
            mask = u <= t
        v = v_ref[...]
        for mp in range(2):
            q = q_ref[:, mp * HEAD_DIM:(mp + 1) * HEAD_DIM]
            k = k_ref[:, mp * HEAD_DIM:(mp + 1) * HEAD_DIM]
            s = _dot_nt(q, k) * ATTN_SCALE
            _flash_step(s, mask, v, m_ref, l_ref, acc_ref, mp)

    below = (ki + 1) * tk - 1 <= qi * tq

    @pl.when(below)
    def _():
        step(False)

    @pl.when(jnp.logical_not(below))
    def _():
        step(True)

    @pl.when(ki == last)
    def _():
        lv = lam_ref[...]
        lam = (jnp.exp(jnp.sum(lv[0:1] * lv[1:2], axis=-1, keepdims=True))
               - jnp.exp(jnp.sum(lv[2:3] * lv[3:4], axis=-1, keepdims=True)) + lambda_init)
        o0 = acc_ref[0] / jnp.maximum(l_ref[0], 1e-30)
        o1 = acc_ref[1] / jnp.maximum(l_ref[1], 1e-30)
        o = o0 - lam * o1
        o = o * lax.rsqrt(jnp.mean(jnp.square(o), axis=-1, keepdims=True) + LN_EPS) * g_ref[...]
        o_ref[...] = (o * (1.0 - lambda_init)).astype(o_ref.dtype)


def _diff_attention(hp, lam_vecs, subln_g, lay, b, s, lambda_init):
    hd = lay['diff_h']
    tq = tk = min(512, s)
    nq = s // tq
    nk = s // tk
    qi_tab, ki_tab = _causal_pairs(nq, tq, tk)
    qb, kb, vb = (lay['off'][n] // DIFF_VDIM for n in ('dq', 'dk', 'dv'))
    assert all(lay['off'][n] % DIFF_VDIM == 0 for n in ('dq', 'dk', 'dv'))
    kern = functools.partial(_diff_kernel, tq=tq, tk=tk, lambda_init=lambda_init)
    grid_spec = pltpu.PrefetchScalarGridSpec(
        num_scalar_prefetch=2,
        grid=(b, hd, len(qi_tab)),
        in_specs=[pl.BlockSpec((4, HEAD_DIM), lambda bi, h, p, qt, kt: (0, 0)),
                  pl.BlockSpec((1, DIFF_VDIM), lambda bi, h, p, qt, kt: (0, 0)),
                  pl.BlockSpec((tq, DIFF_VDIM), lambda bi, h, p, qt, kt: (bi * nq + qt[p], qb + h)),
                  pl.BlockSpec((tk, DIFF_VDIM), lambda bi, h, p, qt, kt: (bi * nk + kt[p], kb + h)),
                  pl.BlockSpec((tk, DIFF_VDIM), lambda bi, h, p, qt, kt: (bi * nk + kt[p], vb + h))],
        out_specs=pl.BlockSpec((tq, DIFF_VDIM), lambda bi, h, p, qt, kt: (bi * nq + qt[p], h)),
        scratch_shapes=[pltpu.VMEM((2, tq, 1), jnp.float32), pltpu.VMEM((2, tq, 1), jnp.float32),
                        pltpu.VMEM((2, tq, DIFF_VDIM), jnp.float32)])
    return pl.pallas_call(
        kern, out_shape=jax.ShapeDtypeStruct((b * s, hd * DIFF_VDIM), MXU_DTYPE),
        grid_spec=grid_spec, compiler_params=_params("parallel", "parallel", "arbitrary"),
    )(jnp.asarray(qi_tab), jnp.asarray(ki_tab), lam_vecs.astype(jnp.float32),
      subln_g.reshape(1, DIFF_VDIM).astype(jnp.float32), hp, hp, hp)


def _kmean_kernel(k_ref, o_ref, *, n_blk):
    k = k_ref[...].astype(jnp.float32).reshape(n_blk, MOBA_BLOCK, HEAD_DIM)
    o_ref[0, 0] = jnp.mean(k, axis=1)


def _moba_kmean(hp, lay, b, s):
    h = lay['moba_h']
    n_blk = s // MOBA_BLOCK
    kb = lay['off']['mk'] // HEAD_DIM
    return pl.pallas_call(
        functools.partial(_kmean_kernel, n_blk=n_blk),
        out_shape=jax.ShapeDtypeStruct((b, h, n_blk, HEAD_DIM), jnp.float32),
        grid=(b, h),
        in_specs=[pl.BlockSpec((s, HEAD_DIM), lambda bi, hi: (bi, kb + hi))],
        out_specs=pl.BlockSpec((1, 1, n_blk, HEAD_DIM), lambda bi, hi: (bi, hi, 0, 0)),
        compiler_params=_params("parallel", "parallel"),
    )(hp)


def _lane_column(x, n):
    return jnp.sum(jnp.where(_iota(x.shape, 1) == n, x, 0.0), axis=-1, keepdims=True)


def _moba_kernel(qi_tab, ki_tab, q_ref, k_ref, v_ref, km_ref, o_ref, sel_ref, m_ref, l_ref, acc_ref,
                 *, tq, tk):
    p_id = pl.program_id(2)
    qi = qi_tab[p_id]
    ki = ki_tab[p_id]
    last = (qi * tq + tq - 1) // tk
    q = q_ref[...]

    @pl.when(ki == 0)
    def _():
        _init_state(m_ref, l_ref, acc_ref)
        score = _dot_nt(q, km_ref[0, 0].astype(q.dtype))
        blk = _iota((tq, LANES), 1)
        own = (qi * tq + _iota((tq, LANES), 0)) // MOBA_BLOCK
        work = jnp.where(blk < own, score, -jnp.inf)
        sel = jnp.where(blk == own, 1.0, 0.0)
        for r in range(MOBA_TOPK):
            mx = jnp.max(work, axis=-1, keepdims=True)
            first = _first_lane(work == mx, blk)
            pick = blk == first
            sel = jnp.where(pick & (own > r), 1.0, sel)
            work = jnp.where(pick, -jnp.inf, work)
        sel_ref[...] = sel

    t = qi * tq + _iota((tq, tk), 0)
    u = ki * tk + _iota((tq, tk), 1)
    sel = sel_ref[...]
    cols = []
    for c in range(tk // MOBA_BLOCK):
        col = _lane_column(sel, ki * (tk // MOBA_BLOCK) + c)
        cols.append(jnp.broadcast_to(col, (tq, MOBA_BLOCK)))
    allowed = cols[0] if len(cols) == 1 else jnp.concatenate(cols, axis=1)
    mask = (allowed > 0.5) & (u <= t)
    s = _dot_nt(q, k_ref[...]) * ATTN_SCALE
    _flash_step(s, mask, v_ref[...], m_ref, l_ref, acc_ref, 0)

    @pl.when(ki == last)
    def _():
        o_ref[...] = (acc_ref[0] / jnp.maximum(l_ref[0], 1e-30)).astype(o_ref.dtype)


def _moba_attention(hp, lay, b, s):
    h = lay['moba_h']
    n_blk = s // MOBA_BLOCK
    assert s % MOBA_BLOCK == 0 and MOBA_TOPK <= n_blk <= LANES
    km = _moba_kmean(hp, lay, b, s)
    km = jnp.pad(km, ((0, 0), (0, 0), (0, LANES - n_blk), (0, 0)))
    tq = min(1024, s)
    tk = min(512, s)
    nq, nk = s // tq, s // tk
    qi_tab, ki_tab = _causal_pairs(nq, tq, tk)
    qb, kb, vb = (lay['off'][n] // HEAD_DIM for n in ('mq', 'mk', 'mv'))
    grid_spec = pltpu.PrefetchScalarGridSpec(
        num_scalar_prefetch=2,
        grid=(b, h, len(qi_tab)),
        in_specs=[pl.BlockSpec((tq, HEAD_DIM), lambda bi, hi, p, qt, kt: (bi * nq + qt[p], qb + hi)),
                  pl.BlockSpec((tk, HEAD_DIM), lambda bi, hi, p, qt, kt: (bi * nk + kt[p], kb + hi)),
                  pl.BlockSpec((tk, HEAD_DIM), lambda bi, hi, p, qt, kt: (bi * nk + kt[p], vb + hi)),
                  pl.BlockSpec((1, 1, LANES, HEAD_DIM), lambda bi, hi, p, qt, kt: (bi, hi, 0, 0))],
        out_specs=pl.BlockSpec((tq, HEAD_DIM), lambda bi, hi, p, qt, kt: (bi * nq + qt[p], hi)),
        scratch_shapes=[pltpu.VMEM((tq, LANES), jnp.float32),
                        pltpu.VMEM((1, tq, 1), jnp.float32), pltpu.VMEM((1, tq, 1), jnp.float32),
                        pltpu.VMEM((1, tq, HEAD_DIM), jnp.float32)])
    return pl.pallas_call(
        functools.partial(_moba_kernel, tq=tq, tk=tk),
        out_shape=jax.ShapeDtypeStruct((b * s, h * HEAD_DIM), MXU_DTYPE),
        grid_spec=grid_spec, compiler_params=_params("parallel", "parallel", "arbitrary"),
    )(jnp.asarray(qi_tab), jnp.asarray(ki_tab), hp, hp, hp, km)


def _gelu_tanh(x):
    return 0.5 * x * (1.0 + jnp.tanh(math.sqrt(2.0 / math.pi) * (x + 0.044715 * (x * x * x))))


def _compress_kernel(seg_ref, plo_ref, phi_ref, w1lo_ref, w1hi_ref, w2_ref, o_ref, *, n_seg):
    seg = seg_ref[0, 0, 0].astype(jnp.float32)
    lo = _dot((seg + plo_ref[0]).astype(MXU_DTYPE), w1lo_ref[0])
    hi = _dot((seg + phi_ref[0]).astype(MXU_DTYPE), w1hi_ref[0])
    pre = lo + pltpu.roll(hi, n_seg - 1, 0)
    o_ref[0, 0, 0] = _dot(_gelu_tanh(pre).astype(MXU_DTYPE), w2_ref[0]).astype(o_ref.dtype)


def _nsa_compress(hp, cmp_pos, cmp_w1, cmp_w2, lay, b, s):
    kv = lay['kv']
    n_seg = s // CMP_STRIDE
    half = CMP_STRIDE * HEAD_DIM

    def segments(name):
        o = lay['off'][name]
        t = hp[:, o:o + kv * HEAD_DIM].reshape(b, n_seg, CMP_STRIDE, kv, HEAD_DIM)
        return t.transpose(0, 3, 1, 2, 4).reshape(b, kv, n_seg, half)

    seg = jnp.stack([segments('nkc'), segments('nvc')], axis=1)
    pos = cmp_pos.astype(jnp.float32).reshape(2, CMP_LEN * HEAD_DIM)
    plo = pos[:, :half].reshape(2, 1, half)
    phi = pos[:, half:].reshape(2, 1, half)
    w1 = cmp_w1.astype(MXU_DTYPE)
    w1lo, w1hi = w1[:, :half], w1[:, half:]
    w2 = cmp_w2.astype(MXU_DTYPE)
    return pl.pallas_call(
        functools.partial(_compress_kernel, n_seg=n_seg),
        out_shape=jax.ShapeDtypeStruct((b, 2, kv, n_seg, HEAD_DIM), MXU_DTYPE),
        grid=(b, 2, kv),
        in_specs=[pl.BlockSpec((1, 1, 1, n_seg, half), lambda bi, c, k: (bi, c, k, 0, 0)),
                  pl.BlockSpec((1, 1, half), lambda bi, c, k: (c, 0, 0)),
                  pl.BlockSpec((1, 1, half), lambda bi, c, k: (c, 0, 0)),
                  pl.BlockSpec((1, half, HEAD_DIM), lambda bi, c, k: (c, 0, 0)),
                  pl.BlockSpec((1, half, HEAD_DIM), lambda bi, c, k: (c, 0, 0)),
                  pl.BlockSpec((1, HEAD_DIM, HEAD_DIM), lambda bi, c, k: (c, 0, 0))],
        out_specs=pl.BlockSpec((1, 1, 1, n_seg, HEAD_DIM), lambda bi, c, k: (bi, c, k, 0, 0)),
        compiler_params=_params("parallel", "parallel", "parallel"),
    )(seg, plo, phi, w1lo, w1hi, w2)


def _nsa_cmp_kernel(q_ref, kc_ref, vc_ref, o_ref, sel_ref, *, tq, n_sel, n_cmp):
    qi = pl.program_id(2)
    width = CMP_PER_SEL * LANES
    t = qi * tq + _iota((tq, width), 0)
    pos = _iota((tq, width), 1)
    m_idx = pos % LANES
    n_idx = CMP_PER_SEL * m_idx + pos // LANES
    valid = (m_idx < n_sel) & (n_idx < n_cmp) & (n_idx * CMP_STRIDE + CMP_LEN - 1 <= t)
    kc = kc_ref[0, 0, 0]
    vc = vc_ref[0, 0, 0]
    p_grp = jnp.zeros((tq, width), jnp.float32)
    for g in range(NSA_GROUP):
        q = q_ref[:, g * HEAD_DIM:(g + 1) * HEAD_DIM]
        s = jnp.where(valid, _dot_nt(q, kc) * ATTN_SCALE, NEG_INF)
        p = jnp.where(valid, jnp.exp(s - jnp.max(s, axis=-1, keepdims=True)), 0.0)
        p = p / jnp.maximum(jnp.sum(p, axis=-1, keepdims=True), 1e-30)
        o_ref[:, g * HEAD_DIM:(g + 1) * HEAD_DIM] = _dot(p.astype(vc.dtype), vc)
        p_grp = p_grp + p

    slabs = [p_grp[:, r * LANES:(r + 1) * LANES] for r in range(CMP_PER_SEL)]
    blk = _iota((tq, LANES), 1)
    prev = jnp.where(blk == 0, 0.0, pltpu.roll(slabs[CMP_PER_SEL - 1], 1, 1))
    imp = prev
    for r in range(CMP_PER_SEL):
        imp = imp + slabs[r]
    cur = (qi * tq + _iota((tq, LANES), 0)) // SEL_LEN
    forced = (blk == 0) | (blk == cur) | (blk == cur - 1)
    work = jnp.where(forced, jnp.inf, imp)
    work = jnp.where(blk <= cur, work, -jnp.inf)
    sel = jnp.zeros((tq, LANES), jnp.float32)
    for _ in range(SEL_TOPN):
        mx = jnp.max(work, axis=-1, keepdims=True)
        first = _first_lane(work == mx, blk)
        pick = blk == first
        sel = jnp.where(pick, 1.0, sel)
        work = jnp.where(pick, -jnp.inf, work)
    sel_ref[0, 0] = jnp.where(blk <= cur, sel, 0.0).astype(sel_ref.dtype)


def _nsa_cmp_select(hp, kcv, lay, b, s):
    kv = lay['kv']
    n_seg = s // CMP_STRIDE
    n_sel = s // SEL_LEN
    n_cmp = (s - CMP_LEN) // CMP_STRIDE + 1
    assert SEL_TOPN <= n_sel <= LANES
    width = CMP_PER_SEL * LANES
    kcp = kcv.reshape(b, 2, kv, n_sel, CMP_PER_SEL, HEAD_DIM).transpose(0, 1, 2, 4, 3, 5)
    kcp = jnp.pad(kcp, ((0, 0),) * 4 + ((0, LANES - n_sel), (0, 0))).reshape(b, 2, kv, width, HEAD_DIM)
    tq = min(256, s)
    nq = s // tq
    gw = NSA_GROUP * HEAD_DIM
    return pl.pallas_call(
        functools.partial(_nsa_cmp_kernel, tq=tq, n_sel=n_sel, n_cmp=n_cmp),
        out_shape=(jax.ShapeDtypeStruct((b * s, lay['nsa_h'] * HEAD_DIM), jnp.float32),
                   jax.ShapeDtypeStruct((b, kv, s, LANES), MXU_DTYPE)),
        grid=(b, kv, nq),
        in_specs=[pl.BlockSpec((tq, gw), lambda bi, k, qi: (bi * nq + qi, k)),
                  pl.BlockSpec((1, 1, 1, width, HEAD_DIM), lambda bi, k, qi: (bi, 0, k, 0, 0)),
                  pl.BlockSpec((1, 1, 1, width, HEAD_DIM), lambda bi, k, qi: (bi, 1, k, 0, 0))],
        out_specs=(pl.BlockSpec((tq, gw), lambda bi, k, qi: (bi * nq + qi, k)),
                   pl.BlockSpec((1, 1, tq, LANES), lambda bi, k, qi: (bi, k, qi, 0))),
        compiler_params=_params("parallel", "parallel", "parallel"),
    )(hp, kcp, kcp)


def _nsa_main_kernel(qi_tab, ki_tab, q_ref, ks_ref, vs_ref, kw_ref, vw_ref, sel_ref, e_ref, oc_ref,
                     gl_ref, o_ref, ms_ref, ls_ref, as_ref, mw_ref, lw_ref, aw_ref, *, tq, tk):
    p_id = pl.program_id(2)
    qi = qi_tab[p_id]
    ki = ki_tab[p_id]
    last = (qi * tq + tq - 1) // tk
    first_win = jnp.maximum((qi * tq - WINDOW + 1) // tk, 0)

    @pl.when(ki == 0)
    def _():
        _init_state(ms_ref, ls_ref, as_ref)
        _init_state(mw_ref, lw_ref, aw_ref)

    t = qi * tq + _iota((tq, tk), 0)
    u = ki * tk + _iota((tq, tk), 1)
    chosen = _dot(sel_ref[0, 0], e_ref[...])
    m_sel = (chosen > 0.5) & (u <= t)
    ks, vs = ks_ref[...], vs_ref[...]
    for g in range(NSA_GROUP):
        q = q_ref[:, g * HEAD_DIM:(g + 1) * HEAD_DIM]
        _flash_step(_dot_nt(q, ks) * ATTN_SCALE, m_sel, vs, ms_ref, ls_ref, as_ref, g)

    @pl.when(ki >= first_win)
    def _():
        m_win = (u <= t) & (u > t - WINDOW)
        kw, vw = kw_ref[...], vw_ref[...]
        for g in range(NSA_GROUP):
            q = q_ref[:, g * HEAD_DIM:(g + 1) * HEAD_DIM]
            _flash_step(_dot_nt(q, kw) * ATTN_SCALE, m_win, vw, mw_ref, lw_ref, aw_ref, g)

    @pl.when(ki == last)
    def _():
        gate = 1.0 / (1.0 + jnp.exp(-gl_ref[...]))
        for g in range(NSA_GROUP):
            o_cmp = oc_ref[:, g * HEAD_DIM:(g + 1) * HEAD_DIM]
            o_slc = as_ref[g] / jnp.maximum(ls_ref[g], 1e-30)
            o_win = aw_ref[g] / jnp.maximum(lw_ref[g], 1e-30)
            out = (gate[:, 3 * g:3 * g + 1] * o_cmp + gate[:, 3 * g + 1:3 * g + 2] * o_slc
                   + gate[:, 3 * g + 2:3 * g + 3] * o_win)
            o_ref[:, g * HEAD_DIM:(g + 1) * HEAD_DIM] = out.astype(o_ref.dtype)


def _nsa_main(hp, sel, o_cmp, gate_logits, lay, b, s):
    kv = lay['kv']
    tq = min(256, s)
    tk = min(512, s)
    nq, nk = s // tq, s // tk
    qi_tab, ki_tab = _causal_pairs(nq, tq, tk)
    gw = NSA_GROUP * HEAD_DIM
    ksb, vsb, kwb, vwb = (lay['off'][n] // HEAD_DIM for n in ('nks', 'nvs', 'nkw', 'nvw'))
    expand = (np.arange(s)[None, :] // SEL_LEN == np.arange(LANES)[:, None])
    expand = jnp.asarray(expand, MXU_DTYPE)

    def win_blk(qt, kt, p):
        return jnp.maximum(kt[p], jnp.maximum((qt[p] * tq - WINDOW + 1) // tk, 0))

    grid_spec = pltpu.PrefetchScalarGridSpec(
        num_scalar_prefetch=2,
        grid=(b, kv, len(qi_tab)),
        in_specs=[pl.BlockSpec((tq, gw), lambda bi, k, p, qt, kt: (bi * nq + qt[p], k)),
                  pl.BlockSpec((tk, HEAD_DIM), lambda bi, k, p, qt, kt: (bi * nk + kt[p], ksb + k)),
                  pl.BlockSpec((tk, HEAD_DIM), lambda bi, k, p, qt, kt: (bi * nk + kt[p], vsb + k)),
                  pl.BlockSpec((tk, HEAD_DIM),
                               lambda bi, k, p, qt, kt: (bi * nk + win_blk(qt, kt, p), kwb + k)),
                  pl.BlockSpec((tk, HEAD_DIM),
                               lambda bi, k, p, qt, kt: (bi * nk + win_blk(qt, kt, p), vwb + k)),
                  pl.BlockSpec((1, 1, tq, LANES), lambda bi, k, p, qt, kt: (bi, k, qt[p], 0)),
                  pl.BlockSpec((LANES, tk), lambda bi, k, p, qt, kt: (0, kt[p])),
                  pl.BlockSpec((tq, gw), lambda bi, k, p, qt, kt: (bi * nq + qt[p], k)),
                  pl.BlockSpec((tq, LANES), lambda bi, k, p, qt, kt: (bi * nq + qt[p], k))],
        out_specs=pl.BlockSpec((tq, gw), lambda bi, k, p, qt, kt: (bi * nq + qt[p], k)),
        scratch_shapes=[pltpu.VMEM((NSA_GROUP, tq, 1), jnp.float32),
                        pltpu.VMEM((NSA_GROUP, tq, 1), jnp.float32),
                        pltpu.VMEM((NSA_GROUP, tq, HEAD_DIM), jnp.float32),
                        pltpu.VMEM((NSA_GROUP, tq, 1), jnp.float32),
                        pltpu.VMEM((NSA_GROUP, tq, 1), jnp.float32),
                        pltpu.VMEM((NSA_GROUP, tq, HEAD_DIM), jnp.float32)])
    return pl.pallas_call(
        functools.partial(_nsa_main_kernel, tq=tq, tk=tk),
        out_shape=jax.ShapeDtypeStruct((b * s, lay['nsa_h'] * HEAD_DIM), MXU_DTYPE),
        grid_spec=grid_spec, compiler_params=_params("parallel", "parallel", "arbitrary"),
    )(jnp.asarray(qi_tab), jnp.asarray(ki_tab), hp, hp, hp, hp, hp, sel, expand, o_cmp, gate_logits)


def _layer_norm(y, g, b):
    mu = jnp.mean(y, axis=-1, keepdims=True)
    var = jnp.mean(jnp.square(y - mu), axis=-1, keepdims=True)
    return (y - mu) * lax.rsqrt(var + LN_EPS) * g + b


def _out_proj_kernel(mix_ref, w_ref, x_ref, g_ref, b_ref, wr_ref, br_ref, x1_ref, idx_ref, gate_ref,
                     acc_ref, *, alpha, n_experts):
    k = pl.program_id(1)

    @pl.when(k == 0)
    def _():
        acc_ref[...] = jnp.zeros(acc_ref.shape, jnp.float32)

    acc_ref[...] += _dot(mix_ref[...], w_ref[...])

    @pl.when(k == pl.num_programs(1) - 1)
    def _():
        x1 = _layer_norm(alpha * x_ref[...] + acc_ref[...], g_ref[...], b_ref[...])
        x1_ref[...] = x1
        logits = _dot(x1.astype(MXU_DTYPE), wr_ref[...]) + br_ref[...]
        lane = _iota(logits.shape, 1)
        work = jnp.where(lane < n_experts, logits, -jnp.inf)
        idx_out = jnp.zeros(logits.shape, jnp.int32)
        val_out = jnp.zeros(logits.shape, jnp.float32)
        top = None
        for r in range(TOP_K):
            mx = jnp.max(work, axis=-1, keepdims=True)
            first = _first_lane(work == mx, lane)
            top = mx if top is None else top
            idx_out = jnp.where(lane == r, first, idx_out)
            val_out = jnp.where(lane == r, jnp.exp(mx - top), val_out)
            work = jnp.where(lane == first, -jnp.inf, work)
        idx_ref[...] = idx_out
        gate_ref[...] = val_out / jnp.sum(val_out, axis=-1, keepdims=True)


def _out_proj_ln_router(mix, w_out, x, ln_g, ln_b, w_router, b_router, alpha):
    t, kdim = mix.shape
    d = w_out.shape[1]
    n_experts = w_router.shape[1]
    tm = min(256, t)
    tk = min(512, kdim)
    wr = jnp.pad(w_router, ((0, 0), (0, LANES - n_experts))).astype(MXU_DTYPE)
    br = jnp.pad(b_router.astype(jnp.float32), (0, LANES - n_experts)).reshape(1, LANES)
    row = lambda i, k: (i, 0)
    const = lambda i, k: (0, 0)
    return pl.pallas_call(
        functools.partial(_out_proj_kernel, alpha=alpha, n_experts=n_experts),
        out_shape=(jax.ShapeDtypeStruct((t, d), jnp.float32),
                   jax.ShapeDtypeStruct((t, LANES), jnp.int32),
                   jax.ShapeDtypeStruct((t, LANES), jnp.float32)),
        grid=(t // tm, kdim // tk),
        in_specs=[pl.BlockSpec((tm, tk), lambda i, k: (i, k)),
                  pl.BlockSpec((tk, d), lambda i, k: (k, 0)),
                  pl.BlockSpec((tm, d), row),
                  pl.BlockSpec((1, d), const), pl.BlockSpec((1, d), const),
                  pl.BlockSpec((d, LANES), const), pl.BlockSpec((1, LANES), const)],
        out_specs=(pl.BlockSpec((tm, d), row), pl.BlockSpec((tm, LANES), row),
                   pl.BlockSpec((tm, LANES), row)),
        scratch_shapes=[pltpu.VMEM((tm, d), jnp.float32)],
        compiler_params=_params("parallel", "arbitrary"),
    )(mix, w_out, x, ln_g.reshape(1, d), ln_b.reshape(1, d), wr, br)


MOE_TILE = 256
COMBINE_TILE = 128


def _moe_kernel(rt_ref, te_ref, nu_ref, x_hbm, wg_ref, wl_ref, bg_ref, bl_ref, wd_ref, bd_ref, o_ref,
                xbuf, sem):
    i = pl.program_id(0)
    n_used = nu_ref[0]

    def row_copy(tile, slot, r):
        tok = rt_ref[tile * MOE_TILE + r]
        return pltpu.make_async_copy(x_hbm.at[pl.ds(tok, 1)], xbuf.at[slot, pl.ds(r, 1)], sem.at[slot])

    def gather(tile, slot):
        def body(r, carry):
            row_copy(tile, slot, r).start()
            return carry
        lax.fori_loop(0, MOE_TILE, body, 0)

    @pl.when(i == 0)
    def _():
        gather(0, 0)

    @pl.when(i + 1 < n_used)
    def _():
        gather(i + 1, (i + 1) % 2)

    @pl.when(i < n_used)
    def _():
        slot = i % 2
        pltpu.make_async_copy(x_hbm.at[pl.ds(0, MOE_TILE)], xbuf.at[slot], sem.at[slot]).wait()
        x = xbuf[slot].astype(MXU_DTYPE)
        h_glu = jnp.minimum(_dot(x, wg_ref[0]) + bg_ref[0], SWIGLU_LIMIT)
        h_lin = jnp.clip(_dot(x, wl_ref[0]) + bl_ref[0], -SWIGLU_LIMIT, SWIGLU_LIMIT)
        act = h_glu * (1.0 / (1.0 + jnp.exp(-SWIGLU_ALPHA * h_glu))) * (h_lin + 1.0)
        o_ref[...] = _dot(act.astype(MXU_DTYPE), wd_ref[0]) + bd_ref[0]

    @pl.when(i >= n_used)
    def _():
        o_ref[...] = jnp.zeros(o_ref.shape, o_ref.dtype)


def _moe_experts(x1, row_token, tile_expert, n_used, wg, wl, bg, bl, wd, bd):
    t, d = x1.shape
    f = wg.shape[2]
    n_tiles = tile_expert.shape[0]
    wmap = lambda i, rt, te, nu: (te[i], 0, 0)
    grid_spec = pltpu.PrefetchScalarGridSpec(
        num_scalar_prefetch=3,
        grid=(n_tiles,),
        in_specs=[pl.BlockSpec(memory_space=pl.ANY),
                  pl.BlockSpec((1, d, f), wmap), pl.BlockSpec((1, d, f), wmap),
                  pl.BlockSpec((1, 1, f), wmap), pl.BlockSpec((1, 1, f), wmap),
                  pl.BlockSpec((1, f, d), wmap), pl.BlockSpec((1, 1, d), wmap)],
        out_specs=pl.BlockSpec((MOE_TILE, d), lambda i, rt, te, nu: (i, 0)),
        scratch_shapes=[pltpu.VMEM((2, MOE_TILE, d), jnp.float32), pltpu.SemaphoreType.DMA((2,))])
    return pl.pallas_call(
        _moe_kernel,
        out_shape=jax.ShapeDtypeStruct((n_tiles * MOE_TILE, d), jnp.float32),
        grid_spec=grid_spec, compiler_params=_params("arbitrary"),
    )(row_token, tile_expert, n_used, x1, wg, wl, bg, bl, wd, bd)


def _combine_kernel(pos_ref, y_hbm, x_ref, gate_ref, g_ref, b_ref, o_ref, ob_ref, ybuf, sem, *, alpha):
    i = pl.program_id(0)
    n_rows = COMBINE_TILE * TOP_K

    def gather(tile, slot):
        def body(r, carry):
            src = pos_ref[tile * n_rows + r]
            pltpu.make_async_copy(y_hbm.at[pl.ds(src, 1)], ybuf.at[slot, pl.ds(r, 1)], sem.at[slot]).start()
            return carry
        lax.fori_loop(0, n_rows, body, 0)

    @pl.when(i == 0)
    def _():
        gather(0, 0)

    @pl.when(i + 1 < pl.num_programs(0))
    def _():
        gather(i + 1, (i + 1) % 2)

    slot = i % 2
    pltpu.make_async_copy(y_hbm.at[pl.ds(0, n_rows)], ybuf.at[slot], sem.at[slot]).wait()
    gate = gate_ref[...]
    ffn = jnp.zeros(x_ref.shape, jnp.float32)
    for k in range(TOP_K):
        ffn = ffn + gate[:, k:k + 1] * ybuf[slot, pl.ds(k * COMBINE_TILE, COMBINE_TILE)]
    x2 = _layer_norm(alpha * x_ref[...] + ffn, g_ref[...], b_ref[...])
    o_ref[...] = x2
    ob_ref[...] = x2.astype(ob_ref.dtype)


def _combine_ln(y, pos, x1, gate, ln_g, ln_b, alpha):
    t, d = x1.shape
    row = lambda i, p: (i, 0)
    const = lambda i, p: (0, 0)
    grid_spec = pltpu.PrefetchScalarGridSpec(
        num_scalar_prefetch=1,
        grid=(t // COMBINE_TILE,),
        in_specs=[pl.BlockSpec(memory_space=pl.ANY),
                  pl.BlockSpec((COMBINE_TILE, d), row), pl.BlockSpec((COMBINE_TILE, LANES), row),
                  pl.BlockSpec((1, d), const), pl.BlockSpec((1, d), const)],
        out_specs=(pl.BlockSpec((COMBINE_TILE, d), row), pl.BlockSpec((COMBINE_TILE, d), row)),
        scratch_shapes=[pltpu.VMEM((2, COMBINE_TILE * TOP_K, d), jnp.float32),
                        pltpu.SemaphoreType.DMA((2,))])
    return pl.pallas_call(
        functools.partial(_combine_kernel, alpha=alpha),
        out_shape=(jax.ShapeDtypeStruct((t, d), jnp.float32), jax.ShapeDtypeStruct((t, d), MXU_DTYPE)),
        grid_spec=grid_spec, compiler_params=_params("arbitrary"),
    )(pos, y, x1, gate, ln_g.reshape(1, d), ln_b.reshape(1, d))


def _route(top_idx, n_experts):
    t = top_idx.shape[0]
    n_assign = t * TOP_K
    n_tiles = n_assign // MOE_TILE + n_experts
    expert = top_idx.reshape(n_assign)
    onehot = (expert[:, None] == jnp.arange(n_experts, dtype=jnp.int32)[None, :]).astype(jnp.int32)
    running = jnp.cumsum(onehot, axis=0)
    counts = running[-1]
    rank = jnp.take_along_axis(running, expert[:, None], axis=1)[:, 0] - 1
    padded = (counts + MOE_TILE - 1) // MOE_TILE * MOE_TILE
    pad_end = jnp.cumsum(padded)
    dest = (pad_end - padded)[expert] + rank
    slot = dest.reshape(t // COMBINE_TILE, COMBINE_TILE, TOP_K).transpose(0, 2, 1).reshape(n_assign)
    token = jnp.arange(n_assign, dtype=jnp.int32) // TOP_K
    row_token = jnp.zeros((n_tiles * MOE_TILE,), jnp.int32).at[dest].set(token)
    tile_expert = jnp.minimum(
        jnp.searchsorted(pad_end, jnp.arange(n_tiles, dtype=jnp.int32) * MOE_TILE, side='right'),
        n_experts - 1).astype(jnp.int32)
    n_used = (pad_end[-1:] // MOE_TILE).astype(jnp.int32)
    return row_token, tile_expert, n_used, slot.astype(jnp.int32)


def _rope_tables(positions):
    inv_freq = 1.0 / (ROPE_THETA ** (jnp.arange(0, ROT_DIM, 2, dtype=jnp.float32) / ROT_DIM))
    ang = positions.astype(jnp.float32).reshape(-1)[:, None] * inv_freq
    cos, sin = jnp.cos(ang), jnp.sin(ang)
    rest = HEAD_DIM - ROT_DIM
    n = ang.shape[0]
    ct = jnp.concatenate([cos, cos, jnp.ones((n, rest), jnp.float32)], axis=1)
    sa = jnp.concatenate([jnp.zeros_like(sin), sin, jnp.zeros((n, rest), jnp.float32)], axis=1)
    sb = jnp.concatenate([-sin, jnp.zeros_like(sin), jnp.zeros((n, rest), jnp.float32)], axis=1)
    return ct, sa, sb


def _gate_weight(w_in_l, lay):
    d = w_in_l.shape[0]
    kv = lay['kv']
    o = lay['ref_off']['ngate']
    wg = w_in_l[:, o:o + lay['size']['ngate']].reshape(d, kv, NSA_GROUP * 3)
    wg = jnp.pad(wg, ((0, 0), (0, 0), (0, LANES - NSA_GROUP * 3)))
    return wg.reshape(d, kv * LANES).astype(MXU_DTYPE)


def kernel(x, positions, w_in, nsa_cmp_pos, nsa_cmp_w1, nsa_cmp_w2, diff_lambda, diff_subln_g, w_out,
           ln1_g, ln1_b, w_router, b_router, w_gate_up, b_gate_up, w_down, b_down, ln2_g, ln2_b):
    b, s, d = x.shape
    depth = w_in.shape[0]
    n_experts = w_router.shape[2]
    lay = _layout(d)
    alpha = (2 * depth) ** 0.25
    ct, sa, sb = _rope_tables(positions)
    xf = x.reshape(b * s, d).astype(jnp.float32)
    xb = xf.astype(MXU_DTYPE)
    for layer in range(depth):
        w_l = w_in[layer]
        w_perm = jnp.concatenate(
            [w_l[:, lay['ref_off'][n]:lay['ref_off'][n] + lay['size'][n]] for n in ROPED + PLAIN],
            axis=1).astype(MXU_DTYPE)
        hp = _project(xb, w_perm, ct, sa, sb, lay['n_roped'])
        gate_logits = _gate_logits(xb, _gate_weight(w_l, lay))

        kcv = _nsa_compress(hp, nsa_cmp_pos[layer], nsa_cmp_w1[layer], nsa_cmp_w2[layer], lay, b, s)
        o_cmp, sel = _nsa_cmp_select(hp, kcv, lay, b, s)
        y_nsa = _nsa_main(hp, sel, o_cmp, gate_logits, lay, b, s)
        lambda_init = 0.8 - 0.6 * math.exp(-0.3 * layer)
        y_diff = _diff_attention(hp, diff_lambda[layer], diff_subln_g[layer], lay, b, s, lambda_init)
        y_moba = _moba_attention(hp, lay, b, s)
        mix = jnp.concatenate([y_nsa, y_diff, y_moba], axis=1)

        x1, top_idx, top_gate = _out_proj_ln_router(
            mix, w_out[layer].astype(MXU_DTYPE), xf, ln1_g[layer], ln1_b[layer],
            w_router[layer], b_router[layer], alpha)

        row_token, tile_expert, n_used, slot = _route(top_idx[:, :TOP_K], n_experts)
        wgu = w_gate_up[layer]
        bgu = b_gate_up[layer].astype(jnp.float32)
        f = wgu.shape[2] // 2
        y = _moe_experts(
            x1, row_token, tile_expert, n_used,
            wgu[:, :, 0::2].astype(MXU_DTYPE), wgu[:, :, 1::2].astype(MXU_DTYPE),
            bgu[:, 0::2].reshape(n_experts, 1, f), bgu[:, 1::2].reshape(n_experts, 1, f),
            w_down[layer].astype(MXU_DTYPE), b_down[layer].astype(jnp.float32).reshape(n_experts, 1, d))
        xf, xb = _combine_ln(y, slot, x1, top_gate, ln2_g[layer], ln2_b[layer], alpha)
    return xf.reshape(b, s, d).astype(x.dtype)
```

```python
import functools
import math

import numpy as np
import jax
import jax.numpy as jnp
from jax import lax
from jax.experimental import pallas as pl
from jax.experimental.pallas import tpu as pltpu

HEAD_DIM = 128
ROT_DIM = HEAD_DIM // 4
ROT_HALF = ROT_DIM // 2
ROPE_THETA = 500000.0
ATTN_SCALE = HEAD_DIM ** -0.5
NEG_INF = -1e30
LN_EPS = 1e-5

NSA_GROUP = 4
CMP_LEN = 32
CMP_STRIDE = 16
SEL_LEN = 64
SEL_TOPN = 16
WINDOW = 512
CMP_PER_SEL = SEL_LEN // CMP_STRIDE

DIFF_VDIM = 2 * HEAD_DIM
MOBA_BLOCK = 256
MOBA_TOPK = 3

TOP_K = 4
SWIGLU_LIMIT = 7.0
SWIGLU_ALPHA = 1.702

LANES = 128
MXU_DTYPE = jnp.bfloat16
VMEM_LIMIT = 56 * 1024 * 1024

ROPED = ('nq', 'nkc', 'nks', 'nkw', 'dq', 'dk', 'mq', 'mk')
PLAIN = ('nvc', 'nvs', 'nvw', 'dv', 'mv')
REF_ORDER = ('nq', 'nkc', 'nvc', 'nks', 'nvs', 'nkw', 'nvw', 'ngate', 'dq', 'dk', 'dv', 'mq', 'mk', 'mv')


def _layout(d):
    nsa_h = d // 256
    kv = nsa_h // NSA_GROUP
    diff_h = d // 1024
    moba_h = d // 512
    size = dict(nq=nsa_h * HEAD_DIM, nkc=kv * HEAD_DIM, nvc=kv * HEAD_DIM, nks=kv * HEAD_DIM,
                nvs=kv * HEAD_DIM, nkw=kv * HEAD_DIM, nvw=kv * HEAD_DIM, ngate=3 * nsa_h,
                dq=2 * diff_h * HEAD_DIM, dk=2 * diff_h * HEAD_DIM, dv=diff_h * DIFF_VDIM,
                mq=moba_h * HEAD_DIM, mk=moba_h * HEAD_DIM, mv=moba_h * HEAD_DIM)
    ref_off, o = {}, 0
    for n in REF_ORDER:
        ref_off[n] = o
        o += size[n]
    off, o = {}, 0
    for n in ROPED + PLAIN:
        off[n] = o
        o += size[n]
    n_roped = sum(size[n] for n in ROPED)
    return dict(nsa_h=nsa_h, kv=kv, diff_h=diff_h, moba_h=moba_h, size=size, ref_off=ref_off,
                off=off, n_roped=n_roped, n_cols=o)


def _params(*sem):
    return pltpu.CompilerParams(dimension_semantics=sem, vmem_limit_bytes=VMEM_LIMIT)


def _dot(a, b):
    return jnp.dot(a, b, preferred_element_type=jnp.float32)


def _dot_nt(a, b):
    return lax.dot_general(a, b, (((1,), (1,)), ((), ())), preferred_element_type=jnp.float32)


def _iota(shape, dim):
    return lax.broadcasted_iota(jnp.int32, shape, dim)


def _first_lane(hit, lane):
    first = jnp.min(jnp.where(hit, lane.astype(jnp.float32), float(LANES)), axis=-1, keepdims=True)
    return first.astype(jnp.int32)


def _proj_kernel(x_ref, w_ref, ct_ref, sa_ref, sb_ref, o_ref, *, n_rope_blocks, tn):
    j = pl.program_id(1)
    acc = _dot(x_ref[...], w_ref[...])

    @pl.when(j < n_rope_blocks)
    def _():
        ct, sa, sb = ct_ref[...], sa_ref[...], sb_ref[...]
        for c in range(tn // HEAD_DIM):
            a = acc[:, c * HEAD_DIM:(c + 1) * HEAD_DIM]
            r = (a * ct + pltpu.roll(a, ROT_HALF, 1) * sa
                 + pltpu.roll(a, HEAD_DIM - ROT_HALF, 1) * sb)
            o_ref[:, c * HEAD_DIM:(c + 1) * HEAD_DIM] = r.astype(o_ref.dtype)

    @pl.when(j >= n_rope_blocks)
    def _():
        o_ref[...] = acc.astype(o_ref.dtype)


def _project(xb, w, ct, sa, sb, n_roped):
    m, k = xb.shape
    n = w.shape[1]
    tm = min(1024, m)
    tn = next(t for t in (512, 256, 128) if n % t == 0 and n_roped % t == 0)
    kern = functools.partial(_proj_kernel, n_rope_blocks=n_roped // tn, tn=tn)
    return pl.pallas_call(
        kern,
        out_shape=jax.ShapeDtypeStruct((m, n), MXU_DTYPE),
        grid=(m // tm, n // tn),
        in_specs=[pl.BlockSpec((tm, k), lambda i, j: (i, 0)),
                  pl.BlockSpec((k, tn), lambda i, j: (0, j)),
                  pl.BlockSpec((tm, HEAD_DIM), lambda i, j: (i, 0)),
                  pl.BlockSpec((tm, HEAD_DIM), lambda i, j: (i, 0)),
                  pl.BlockSpec((tm, HEAD_DIM), lambda i, j: (i, 0))],
        out_specs=pl.BlockSpec((tm, tn), lambda i, j: (i, j)),
        compiler_params=_params("parallel", "arbitrary"),
    )(xb, w, ct, sa, sb)


def _gate_kernel(x_ref, w_ref, o_ref):
    o_ref[...] = _dot(x_ref[...], w_ref[...])


def _gate_logits(xb, wg):
    m, k = xb.shape
    n = wg.shape[1]
    tm = min(1024, m)
    return pl.pallas_call(
        _gate_kernel,
        out_shape=jax.ShapeDtypeStruct((m, n), jnp.float32),
        grid=(m // tm,),
        in_specs=[pl.BlockSpec((tm, k), lambda i: (i, 0)),
                  pl.BlockSpec((k, n), lambda i: (0, 0))],
        out_specs=pl.BlockSpec((tm, n), lambda i: (i, 0)),
        compiler_params=_params("parallel"),
    )(xb, wg)


EXP2_SCALE = ATTN_SCALE * math.log2(math.e)


def _mask_bias(mask):
    return jnp.where(mask, 0.0, -jnp.inf)


def _flash_step(s, bias, v, m_ref, l_ref, acc_ref, idx):
    m_old = m_ref[idx]
    if bias is not None:
        s = s + bias
    m_new = jnp.maximum(m_old, jnp.max(s, axis=-1, keepdims=True))
    p = jnp.exp2((s - m_new) * EXP2_SCALE)
    alpha = jnp.exp2((m_old - m_new) * EXP2_SCALE)
    l_ref[idx] = alpha * l_ref[idx] + jnp.sum(p, axis=-1, keepdims=True)
    acc_ref[idx] = alpha * acc_ref[idx] + _dot(p.astype(v.dtype), v)
    m_ref[idx] = m_new


def _init_state(m_ref, l_ref, acc_ref):
    m_ref[...] = jnp.full(m_ref.shape, NEG_INF, jnp.float32)
    l_ref[...] = jnp.zeros(l_ref.shape, jnp.float32)
    acc_ref[...] = jnp.zeros(acc_ref.shape, jnp.float32)


def _causal_pairs(nq, tq, tk):
    qi, ki = [], []
    for q in range(nq):
        last = (q * tq + tq - 1) // tk
        for k in range(last + 1):
            qi.append(q)
            ki.append(k)
    return np.asarray(qi, np.int32), np.asarray(ki, np.int32)


def _diff_kernel(qi_tab, ki_tab, lam_ref, g_ref, q_ref, k_ref, v_ref, o_ref, m_ref, l_ref, acc_ref,
                 *, tq, tk, lambda_init):
    p_id = pl.program_id(2)
    qi = qi_tab[p_id]
    ki = ki_tab[p_id]
    last = (qi * tq + tq - 1) // tk

    @pl.when(ki == 0)
    def _():
        _init_state(m_ref, l_ref, acc_ref)

    def step(masked):
        bias = None
        if masked:
            t = qi * tq + _iota((tq, tk), 0)
            u = ki * tk + _iota((tq, tk), 1)
            bias = _mask_bias(u <= t)
        v = v_ref[...]
        for mp in range(2):
            q = q_ref[:, mp * HEAD_DIM:(mp + 1) * HEAD_DIM]
            k = k_ref[:, mp * HEAD_DIM:(mp + 1) * HEAD_DIM]
            _flash_step(_dot_nt(q, k), bias, v, m_ref, l_ref, acc_ref, mp)

    below = (ki + 1) * tk - 1 <= qi * tq

    @pl.when(below)
    def _():
        step(False)

    @pl.when(jnp.logical_not(below))
    def _():
        step(True)

    @pl.when(ki == last)
    def _():
        lv = lam_ref[...]
        lam = (jnp.exp(jnp.sum(lv[0:1] * lv[1:2], axis=-1, keepdims=True))
               - jnp.exp(jnp.sum(lv[2:3] * lv[3:4], axis=-1, keepdims=True)) + lambda_init)
        o0 = acc_ref[0] / jnp.maximum(l_ref[0], 1e-30)
        o1 = acc_ref[1] / jnp.maximum(l_ref[1], 1e-30)
        o = o0 - lam * o1
        o = o * lax.rsqrt(jnp.mean(jnp.square(o), axis=-1, keepdims=True) + LN_EPS) * g_ref[...]
        o_ref[...] = (o * (1.0 - lambda_init)).astype(o_ref.dtype)


def _diff_attention(hp, lam_vecs, subln_g, lay, b, s, lambda_init):
    hd = lay['diff_h']
    tq = tk = min(512, s)
    nq = s // tq
    nk = s // tk
    qi_tab, ki_tab = _causal_pairs(nq, tq, tk)
    qb, kb, vb = (lay['off'][n] // DIFF_VDIM for n in ('dq', 'dk', 'dv'))
    assert all(lay['off'][n] % DIFF_VDIM == 0 for n in ('dq', 'dk', 'dv'))
    kern = functools.partial(_diff_kernel, tq=tq, tk=tk, lambda_init=lambda_init)
    grid_spec = pltpu.PrefetchScalarGridSpec(
        num_scalar_prefetch=2,
        grid=(b, hd, len(qi_tab)),
        in_specs=[pl.BlockSpec((4, HEAD_DIM), lambda bi, h, p, qt, kt: (0, 0)),
                  pl.BlockSpec((1, DIFF_VDIM), lambda bi, h, p, qt, kt: (0, 0)),
                  pl.BlockSpec((tq, DIFF_VDIM), lambda bi, h, p, qt, kt: (bi * nq + qt[p], qb + h)),
                  pl.BlockSpec((tk, DIFF_VDIM), lambda bi, h, p, qt, kt: (bi * nk + kt[p], kb + h)),
                  pl.BlockSpec((tk, DIFF_VDIM), lambda bi, h, p, qt, kt: (bi * nk + kt[p], vb + h))],
        out_specs=pl.BlockSpec((tq, DIFF_VDIM), lambda bi, h, p, qt, kt: (bi * nq + qt[p], h)),
        scratch_shapes=[pltpu.VMEM((2, tq, 1), jnp.float32), pltpu.VMEM((2, tq, 1), jnp.float32),
                        pltpu.VMEM((2, tq, DIFF_VDIM), jnp.float32)])
    return pl.pallas_call(
        kern, out_shape=jax.ShapeDtypeStruct((b * s, hd * DIFF_VDIM), MXU_DTYPE),
        grid_spec=grid_spec, compiler_params=_params("parallel", "parallel", "arbitrary"),
    )(jnp.asarray(qi_tab), jnp.asarray(ki_tab), lam_vecs.astype(jnp.float32),
      subln_g.reshape(1, DIFF_VDIM).astype(jnp.float32), hp, hp, hp)


def _kmean_kernel(k_ref, o_ref, *, n_blk):
    k = k_ref[...].astype(jnp.float32).reshape(n_blk, MOBA_BLOCK, HEAD_DIM)
    o_ref[0, 0] = jnp.mean(k, axis=1)


def _moba_kmean(hp, lay, b, s):
    h = lay['moba_h']
    n_blk = s // MOBA_BLOCK
    kb = lay['off']['mk'] // HEAD_DIM
    return pl.pallas_call(
        functools.partial(_kmean_kernel, n_blk=n_blk),
        out_shape=jax.ShapeDtypeStruct((b, h, n_blk, HEAD_DIM), jnp.float32),
        grid=(b, h),
        in_specs=[pl.BlockSpec((s, HEAD_DIM), lambda bi, hi: (bi, kb + hi))],
        out_specs=pl.BlockSpec((1, 1, n_blk, HEAD_DIM), lambda bi, hi: (bi, hi, 0, 0)),
        compiler_params=_params("parallel", "parallel"),
    )(hp)


def _lane_column(x, n):
    return jnp.sum(jnp.where(_iota(x.shape, 1) == n, x, 0.0), axis=-1, keepdims=True)


def _moba_kernel(qi_tab, ki_tab, q_ref, k_ref, v_ref, km_ref, o_ref, sel_ref, m_ref, l_ref, acc_ref,
                 *, tq, tk):
    p_id = pl.program_id(2)
    qi = qi_tab[p_id]
    ki = ki_tab[p_id]
    last = (qi * tq + tq - 1) // tk
    q = q_ref[...]

    @pl.when(ki == 0)
    def _():
        _init_state(m_ref, l_ref, acc_ref)
        score = _dot_nt(q, km_ref[0, 0].astype(q.dtype))
        blk = _iota((tq, LANES), 1)
        own = (qi * tq + _iota((tq, LANES), 0)) // MOBA_BLOCK
        work = jnp.where(blk < own, score, -jnp.inf)
        sel = jnp.where(blk == own, 1.0, 0.0)
        for r in range(MOBA_TOPK):
            mx = jnp.max(work, axis=-1, keepdims=True)
            first = _first_lane(work == mx, blk)
            pick = blk == first
            sel = jnp.where(pick & (own > r), 1.0, sel)
            work = jnp.where(pick, -jnp.inf, work)
        sel_ref[...] = sel

    t = qi * tq + _iota((tq, tk), 0)
    u = ki * tk + _iota((tq, tk), 1)
    sel = sel_ref[...]
    cols = []
    for c in range(tk // MOBA_BLOCK):
        col = _lane_column(sel, ki * (tk // MOBA_BLOCK) + c)
        cols.append(jnp.broadcast_to(col, (tq, MOBA_BLOCK)))
    allowed = cols[0] if len(cols) == 1 else jnp.concatenate(cols, axis=1)
    bias = _mask_bias((allowed > 0.5) & (u <= t))
    _flash_step(_dot_nt(q, k_ref[...]), bias, v_ref[...], m_ref, l_ref, acc_ref, 0)

    @pl.when(ki == last)
    def _():
        o_ref[...] = (acc_ref[0] / jnp.maximum(l_ref[0], 1e-30)).astype(o_ref.dtype)


def _moba_attention(hp, lay, b, s):
    h = lay['moba_h']
    n_blk = s // MOBA_BLOCK
    assert s % MOBA_BLOCK == 0 and MOBA_TOPK <= n_blk <= LANES
    km = _moba_kmean(hp, lay, b, s)
    km = jnp.pad(km, ((0, 0), (0, 0), (0, LANES - n_blk), (0, 0)))
    tq = min(1024, s)
    tk = min(512, s)
    nq, nk = s // tq, s // tk
    qi_tab, ki_tab = _causal_pairs(nq, tq, tk)
    qb, kb, vb = (lay['off'][n] // HEAD_DIM for n in ('mq', 'mk', 'mv'))
    grid_spec = pltpu.PrefetchScalarGridSpec(
        num_scalar_prefetch=2,
        grid=(b, h, len(qi_tab)),
        in_specs=[pl.BlockSpec((tq, HEAD_DIM), lambda bi, hi, p, qt, kt: (bi * nq + qt[p], qb + hi)),
                  pl.BlockSpec((tk, HEAD_DIM), lambda bi, hi, p, qt, kt: (bi * nk + kt[p], kb + hi)),
                  pl.BlockSpec((tk, HEAD_DIM), lambda bi, hi, p, qt, kt: (bi * nk + kt[p], vb + hi)),
                  pl.BlockSpec((1, 1, LANES, HEAD_DIM), lambda bi, hi, p, qt, kt: (bi, hi, 0, 0))],
        out_specs=pl.BlockSpec((tq, HEAD_DIM), lambda bi, hi, p, qt, kt: (bi * nq + qt[p], hi)),
        scratch_shapes=[pltpu.VMEM((tq, LANES), jnp.float32),
                        pltpu.VMEM((1, tq, 1), jnp.float32), pltpu.VMEM((1, tq, 1), jnp.float32),
                        pltpu.VMEM((1, tq, HEAD_DIM), jnp.float32)])
    return pl.pallas_call(
        functools.partial(_moba_kernel, tq=tq, tk=tk),
        out_shape=jax.ShapeDtypeStruct((b * s, h * HEAD_DIM), MXU_DTYPE),
        grid_spec=grid_spec, compiler_params=_params("parallel", "parallel", "arbitrary"),
    )(jnp.asarray(qi_tab), jnp.asarray(ki_tab), hp, hp, hp, km)


def _gelu_tanh(x):
    return 0.5 * x * (1.0 + jnp.tanh(math.sqrt(2.0 / math.pi) * (x + 0.044715 * (x * x * x))))


def _compress_kernel(seg_ref, plo_ref, phi_ref, w1lo_ref, w1hi_ref, w2_ref, o_ref, *, n_seg):
    seg = seg_ref[0, 0, 0].astype(jnp.float32)
    lo = _dot((seg + plo_ref[0]).astype(MXU_DTYPE), w1lo_ref[0])
    hi = _dot((seg + phi_ref[0]).astype(MXU_DTYPE), w1hi_ref[0])
    pre = lo + pltpu.roll(hi, n_seg - 1, 0)
    o_ref[0, 0, 0] = _dot(_gelu_tanh(pre).astype(MXU_DTYPE), w2_ref[0]).astype(o_ref.dtype)


def _nsa_compress(hp, cmp_pos, cmp_w1, cmp_w2, lay, b, s):
    kv = lay['kv']
    n_seg = s // CMP_STRIDE
    half = CMP_STRIDE * HEAD_DIM

    def segments(name):
        o = lay['off'][name]
        t = hp[:, o:o + kv * HEAD_DIM].reshape(b, n_seg, CMP_STRIDE, kv, HEAD_DIM)
        return t.transpose(0, 3, 1, 2, 4).reshape(b, kv, n_seg, half)

    seg = jnp.stack([segments('nkc'), segments('nvc')], axis=1)
    pos = cmp_pos.astype(jnp.float32).reshape(2, CMP_LEN * HEAD_DIM)
    plo = pos[:, :half].reshape(2, 1, half)
    phi = pos[:, half:].reshape(2, 1, half)
    w1 = cmp_w1.astype(MXU_DTYPE)
    w1lo, w1hi = w1[:, :half], w1[:, half:]
    w2 = cmp_w2.astype(MXU_DTYPE)
    return pl.pallas_call(
        functools.partial(_compress_kernel, n_seg=n_seg),
        out_shape=jax.ShapeDtypeStruct((b, 2, kv, n_seg, HEAD_DIM), MXU_DTYPE),
        grid=(b, 2, kv),
        in_specs=[pl.BlockSpec((1, 1, 1, n_seg, half), lambda bi, c, k: (bi, c, k, 0, 0)),
                  pl.BlockSpec((1, 1, half), lambda bi, c, k: (c, 0, 0)),
                  pl.BlockSpec((1, 1, half), lambda bi, c, k: (c, 0, 0)),
                  pl.BlockSpec((1, half, HEAD_DIM), lambda bi, c, k: (c, 0, 0)),
                  pl.BlockSpec((1, half, HEAD_DIM), lambda bi, c, k: (c, 0, 0)),
                  pl.BlockSpec((1, HEAD_DIM, HEAD_DIM), lambda bi, c, k: (c, 0, 0))],
        out_specs=pl.BlockSpec((1, 1, 1, n_seg, HEAD_DIM), lambda bi, c, k: (bi, c, k, 0, 0)),
        compiler_params=_params("parallel", "parallel", "parallel"),
    )(seg, plo, phi, w1lo, w1hi, w2)


def _nsa_cmp_kernel(q_ref, kc_ref, vc_ref, o_ref, sel_ref, *, tq, n_sel, n_cmp):
    qi = pl.program_id(2)
    width = CMP_PER_SEL * LANES
    t = qi * tq + _iota((tq, width), 0)
    pos = _iota((tq, width), 1)
    m_idx = pos % LANES
    n_idx = CMP_PER_SEL * m_idx + pos // LANES
    valid = (m_idx < n_sel) & (n_idx < n_cmp) & (n_idx * CMP_STRIDE + CMP_LEN - 1 <= t)
    kc = kc_ref[0, 0, 0]
    vc = vc_ref[0, 0, 0]
    p_grp = jnp.zeros((tq, width), jnp.float32)
    for g in range(NSA_GROUP):
        q = q_ref[:, g * HEAD_DIM:(g + 1) * HEAD_DIM]
        s = jnp.where(valid, _dot_nt(q, kc) * ATTN_SCALE, NEG_INF)
        p = jnp.where(valid, jnp.exp(s - jnp.max(s, axis=-1, keepdims=True)), 0.0)
        p = p / jnp.maximum(jnp.sum(p, axis=-1, keepdims=True), 1e-30)
        o_ref[:, g * HEAD_DIM:(g + 1) * HEAD_DIM] = _dot(p.astype(vc.dtype), vc)
        p_grp = p_grp + p

    slabs = [p_grp[:, r * LANES:(r + 1) * LANES] for r in range(CMP_PER_SEL)]
    blk = _iota((tq, LANES), 1)
    prev = jnp.where(blk == 0, 0.0, pltpu.roll(slabs[CMP_PER_SEL - 1], 1, 1))
    imp = prev
    for r in range(CMP_PER_SEL):
        imp = imp + slabs[r]
    cur = (qi * tq + _iota((tq, LANES), 0)) // SEL_LEN
    forced = (blk == 0) | (blk == cur) | (blk == cur - 1)
    work = jnp.where(forced, jnp.inf, imp)
    work = jnp.where(blk <= cur, work, -jnp.inf)
    sel = jnp.zeros((tq, LANES), jnp.float32)
    for _ in range(SEL_TOPN):
        mx = jnp.max(work, axis=-1, keepdims=True)
        first = _first_lane(work == mx, blk)
        pick = blk == first
        sel = jnp.where(pick, 1.0, sel)
        work = jnp.where(pick, -jnp.inf, work)
    sel_ref[0, 0] = jnp.where(blk <= cur, sel, 0.0).astype(sel_ref.dtype)


def _nsa_cmp_select(hp, kcv, lay, b, s):
    kv = lay['kv']
    n_seg = s // CMP_STRIDE
    n_sel = s // SEL_LEN
    n_cmp = (s - CMP_LEN) // CMP_STRIDE + 1
    assert SEL_TOPN <= n_sel <= LANES
    width = CMP_PER_SEL * LANES
    kcp = kcv.reshape(b, 2, kv, n_sel, CMP_PER_SEL, HEAD_DIM).transpose(0, 1, 2, 4, 3, 5)
    kcp = jnp.pad(kcp, ((0, 0),) * 4 + ((0, LANES - n_sel), (0, 0))).reshape(b, 2, kv, width, HEAD_DIM)
    tq = min(256, s)
    nq = s // tq
    gw = NSA_GROUP * HEAD_DIM
    return pl.pallas_call(
        functools.partial(_nsa_cmp_kernel, tq=tq, n_sel=n_sel, n_cmp=n_cmp),
        out_shape=(jax.ShapeDtypeStruct((b * s, lay['nsa_h'] * HEAD_DIM), jnp.float32),
                   jax.ShapeDtypeStruct((b, kv, s, LANES), MXU_DTYPE)),
        grid=(b, kv, nq),
        in_specs=[pl.BlockSpec((tq, gw), lambda bi, k, qi: (bi * nq + qi, k)),
                  pl.BlockSpec((1, 1, 1, width, HEAD_DIM), lambda bi, k, qi: (bi, 0, k, 0, 0)),
                  pl.BlockSpec((1, 1, 1, width, HEAD_DIM), lambda bi, k, qi: (bi, 1, k, 0, 0))],
        out_specs=(pl.BlockSpec((tq, gw), lambda bi, k, qi: (bi * nq + qi, k)),
                   pl.BlockSpec((1, 1, tq, LANES), lambda bi, k, qi: (bi, k, qi, 0))),
        compiler_params=_params("parallel", "parallel", "parallel"),
    )(hp, kcp, kcp)


def _nsa_main_kernel(qi_tab, ki_tab, q_ref, ks_ref, vs_ref, kw_ref, vw_ref, sel_ref, e_ref, oc_ref,
                     gl_ref, o_ref, ms_ref, ls_ref, as_ref, mw_ref, lw_ref, aw_ref, *, tq, tk):
    p_id = pl.program_id(2)
    qi = qi_tab[p_id]
    ki = ki_tab[p_id]
    last = (qi * tq + tq - 1) // tk
    first_win = jnp.maximum((qi * tq - WINDOW + 1) // tk, 0)

    @pl.when(ki == 0)
    def _():
        _init_state(ms_ref, ls_ref, as_ref)
        _init_state(mw_ref, lw_ref, aw_ref)

    t = qi * tq + _iota((tq, tk), 0)
    u = ki * tk + _iota((tq, tk), 1)
    chosen = _dot(sel_ref[0, 0], e_ref[...])
    b_sel = _mask_bias((chosen > 0.5) & (u <= t))
    ks, vs = ks_ref[...], vs_ref[...]
    for g in range(NSA_GROUP):
        q = q_ref[:, g * HEAD_DIM:(g + 1) * HEAD_DIM]
        _flash_step(_dot_nt(q, ks), b_sel, vs, ms_ref, ls_ref, as_ref, g)

    @pl.when(ki >= first_win)
    def _():
        b_win = _mask_bias((u <= t) & (u > t - WINDOW))
        kw, vw = kw_ref[...], vw_ref[...]
        for g in range(NSA_GROUP):
            q = q_ref[:, g * HEAD_DIM:(g + 1) * HEAD_DIM]
            _flash_step(_dot_nt(q, kw), b_win, vw, mw_ref, lw_ref, aw_ref, g)

    @pl.when(ki == last)
    def _():
        gate = 1.0 / (1.0 + jnp.exp(-gl_ref[...]))
        for g in range(NSA_GROUP):
            o_cmp = oc_ref[:, g * HEAD_DIM:(g + 1) * HEAD_DIM]
            o_slc = as_ref[g] / jnp.maximum(ls_ref[g], 1e-30)
            o_win = aw_ref[g] / jnp.maximum(lw_ref[g], 1e-30)
            out = (gate[:, 3 * g:3 * g + 1] * o_cmp + gate[:, 3 * g + 1:3 * g + 2] * o_slc
                   + gate[:, 3 * g + 2:3 * g + 3] * o_win)
            o_ref[:, g * HEAD_DIM:(g + 1) * HEAD_DIM] = out.astype(o_ref.dtype)


def _nsa_main(hp, sel, o_cmp, gate_logits, lay, b, s):
    kv = lay['kv']
    tq = min(256, s)
    tk = min(512, s)
    nq, nk = s // tq, s // tk
    qi_tab, ki_tab = _causal_pairs(nq, tq, tk)
    gw = NSA_GROUP * HEAD_DIM
    ksb, vsb, kwb, vwb = (lay['off'][n] // HEAD_DIM for n in ('nks', 'nvs', 'nkw', 'nvw'))
    expand = (np.arange(s)[None, :] // SEL_LEN == np.arange(LANES)[:, None])
    expand = jnp.asarray(expand, MXU_DTYPE)

    def win_blk(qt, kt, p):
        return jnp.maximum(kt[p], jnp.maximum((qt[p] * tq - WINDOW + 1) // tk, 0))

    grid_spec = pltpu.PrefetchScalarGridSpec(
        num_scalar_prefetch=2,
        grid=(b, kv, len(qi_tab)),
        in_specs=[pl.BlockSpec((tq, gw), lambda bi, k, p, qt, kt: (bi * nq + qt[p], k)),
                  pl.BlockSpec((tk, HEAD_DIM), lambda bi, k, p, qt, kt: (bi * nk + kt[p], ksb + k)),
                  pl.BlockSpec((tk, HEAD_DIM), lambda bi, k, p, qt, kt: (bi * nk + kt[p], vsb + k)),
                  pl.BlockSpec((tk, HEAD_DIM),
                               lambda bi, k, p, qt, kt: (bi * nk + win_blk(qt, kt, p), kwb + k)),
                  pl.BlockSpec((tk, HEAD_DIM),
                               lambda bi, k, p, qt, kt: (bi * nk + win_blk(qt, kt, p), vwb + k)),
                  pl.BlockSpec((1, 1, tq, LANES), lambda bi, k, p, qt, kt: (bi, k, qt[p], 0)),
                  pl.BlockSpec((LANES, tk), lambda bi, k, p, qt, kt: (0, kt[p])),
                  pl.BlockSpec((tq, gw), lambda bi, k, p, qt, kt: (bi * nq + qt[p], k)),
                  pl.BlockSpec((tq, LANES), lambda bi, k, p, qt, kt: (bi * nq + qt[p], k))],
        out_specs=pl.BlockSpec((tq, gw), lambda bi, k, p, qt, kt: (bi * nq + qt[p], k)),
        scratch_shapes=[pltpu.VMEM((NSA_GROUP, tq, 1), jnp.float32),
                        pltpu.VMEM((NSA_GROUP, tq, 1), jnp.float32),
                        pltpu.VMEM((NSA_GROUP, tq, HEAD_DIM), jnp.float32),
                        pltpu.VMEM((NSA_GROUP, tq, 1), jnp.float32),
                        pltpu.VMEM((NSA_GROUP, tq, 1), jnp.float32),
                        pltpu.VMEM((NSA_GROUP, tq, HEAD_DIM), jnp.float32)])
    return pl.pallas_call(
        functools.partial(_nsa_main_kernel, tq=tq, tk=tk),
        out_shape=jax.ShapeDtypeStruct((b * s, lay['nsa_h'] * HEAD_DIM), MXU_DTYPE),
        grid_spec=grid_spec, compiler_params=_params("parallel", "parallel", "arbitrary"),
    )(jnp.asarray(qi_tab), jnp.asarray(ki_tab), hp, hp, hp, hp, hp, sel, expand, o_cmp, gate_logits)


def _layer_norm(y, g, b):
    mu = jnp.mean(y, axis=-1, keepdims=True)
    var = jnp.mean(jnp.square(y - mu), axis=-1, keepdims=True)
    return (y - mu) * lax.rsqrt(var + LN_EPS) * g + b


def _out_proj_kernel(mix_ref, w_ref, x_ref, g_ref, b_ref, wr_ref, br_ref, x1_ref, idx_ref, gate_ref,
                     acc_ref, *, alpha, n_experts):
    k = pl.program_id(1)

    @pl.when(k == 0)
    def _():
        acc_ref[...] = jnp.zeros(acc_ref.shape, jnp.float32)

    acc_ref[...] += _dot(mix_ref[...], w_ref[...])

    @pl.when(k == pl.num_programs(1) - 1)
    def _():
        x1 = _layer_norm(alpha * x_ref[...] + acc_ref[...], g_ref[...], b_ref[...])
        x1_ref[...] = x1
        logits = _dot(x1.astype(MXU_DTYPE), wr_ref[...]) + br_ref[...]
        lane = _iota(logits.shape, 1)
        work = jnp.where(lane < n_experts, logits, -jnp.inf)
        idx_out = jnp.zeros(logits.shape, jnp.int32)
        val_out = jnp.zeros(logits.shape, jnp.float32)
        top = None
        for r in range(TOP_K):
            mx = jnp.max(work, axis=-1, keepdims=True)
            first = _first_lane(work == mx, lane)
            top = mx if top is None else top
            idx_out = jnp.where(lane == r, first, idx_out)
            val_out = jnp.where(lane == r, jnp.exp(mx - top), val_out)
            work = jnp.where(lane == first, -jnp.inf, work)
        idx_ref[...] = idx_out
        gate_ref[...] = val_out / jnp.sum(val_out, axis=-1, keepdims=True)


def _out_proj_ln_router(mix, w_out, x, ln_g, ln_b, w_router, b_router, alpha):
    t, kdim = mix.shape
    d = w_out.shape[1]
    n_experts = w_router.shape[1]
    tm = min(256, t)
    tk = min(512, kdim)
    wr = jnp.pad(w_router, ((0, 0), (0, LANES - n_experts))).astype(MXU_DTYPE)
    br = jnp.pad(b_router.astype(jnp.float32), (0, LANES - n_experts)).reshape(1, LANES)
    row = lambda i, k: (i, 0)
    const = lambda i, k: (0, 0)
    return pl.pallas_call(
        functools.partial(_out_proj_kernel, alpha=alpha, n_experts=n_experts),
        out_shape=(jax.ShapeDtypeStruct((t, d), jnp.float32),
                   jax.ShapeDtypeStruct((t, LANES), jnp.int32),
                   jax.ShapeDtypeStruct((t, LANES), jnp.float32)),
        grid=(t // tm, kdim // tk),
        in_specs=[pl.BlockSpec((tm, tk), lambda i, k: (i, k)),
                  pl.BlockSpec((tk, d), lambda i, k: (k, 0)),
                  pl.BlockSpec((tm, d), row),
                  pl.BlockSpec((1, d), const), pl.BlockSpec((1, d), const),
                  pl.BlockSpec((d, LANES), const), pl.BlockSpec((1, LANES), const)],
        out_specs=(pl.BlockSpec((tm, d), row), pl.BlockSpec((tm, LANES), row),
                   pl.BlockSpec((tm, LANES), row)),
        scratch_shapes=[pltpu.VMEM((tm, d), jnp.float32)],
        compiler_params=_params("parallel", "arbitrary"),
    )(mix, w_out, x, ln_g.reshape(1, d), ln_b.reshape(1, d), wr, br)


MOE_TILE = 256
COMBINE_TILE = 128


GLU_GROUP = 2 * LANES


def _regroup_kernel(w_ref, p_ref, o_ref):
    w = w_ref[0].astype(MXU_DTYPE)
    for c in range(w.shape[1] // GLU_GROUP):
        cols = slice(c * GLU_GROUP, (c + 1) * GLU_GROUP)
        o_ref[0, :, cols] = _dot(w[:, cols], p_ref[...]).astype(o_ref.dtype)


def _regroup_gate_up(w_gate_up):
    e, d, f2 = w_gate_up.shape
    j = np.arange(GLU_GROUP)
    src = np.where(j < LANES, 2 * j, 2 * (j - LANES) + 1)
    perm = np.zeros((GLU_GROUP, GLU_GROUP), np.float32)
    perm[src, j] = 1.0
    tk = min(1024, d)
    return pl.pallas_call(
        _regroup_kernel,
        out_shape=jax.ShapeDtypeStruct((e, d, f2), MXU_DTYPE),
        grid=(e, d // tk),
        in_specs=[pl.BlockSpec((1, tk, f2), lambda ei, ki: (ei, ki, 0)),
                  pl.BlockSpec((GLU_GROUP, GLU_GROUP), lambda ei, ki: (0, 0))],
        out_specs=pl.BlockSpec((1, tk, f2), lambda ei, ki: (ei, ki, 0)),
        compiler_params=_params("parallel", "parallel"),
    )(w_gate_up, jnp.asarray(perm, MXU_DTYPE))


def _regroup_bias(b_gate_up):
    e, f2 = b_gate_up.shape
    b = b_gate_up.astype(jnp.float32).reshape(e, f2 // GLU_GROUP, LANES, 2)
    return b.transpose(0, 1, 3, 2).reshape(e, 1, f2)


def _moe_kernel(rt_ref, te_ref, nu_ref, x_hbm, wgu_ref, bgu_ref, wd_ref, bd_ref, o_ref, xbuf, sem):
    i = pl.program_id(0)
    n_used = nu_ref[0]

    def row_copy(tile, slot, r):
        tok = rt_ref[tile * MOE_TILE + r]
        return pltpu.make_async_copy(x_hbm.at[pl.ds(tok, 1)], xbuf.at[slot, pl.ds(r, 1)], sem.at[slot])

    def gather(tile, slot):
        def body(r, carry):
            row_copy(tile, slot, r).start()
            return carry
        lax.fori_loop(0, MOE_TILE, body, 0)

    @pl.when(i == 0)
    def _():
        gather(0, 0)

    @pl.when(i + 1 < n_used)
    def _():
        gather(i + 1, (i + 1) % 2)

    @pl.when(i < n_used)
    def _():
        slot = i % 2
        pltpu.make_async_copy(x_hbm.at[pl.ds(0, MOE_TILE)], xbuf.at[slot], sem.at[slot]).wait()
        x = xbuf[slot].astype(MXU_DTYPE)
        hid = _dot(x, wgu_ref[0]) + bgu_ref[0]
        acts = []
        for c in range(hid.shape[1] // GLU_GROUP):
            h_glu = jnp.minimum(hid[:, c * GLU_GROUP:c * GLU_GROUP + LANES], SWIGLU_LIMIT)
            h_lin = jnp.clip(hid[:, c * GLU_GROUP + LANES:(c + 1) * GLU_GROUP], -SWIGLU_LIMIT, SWIGLU_LIMIT)
            acts.append(h_glu * (1.0 / (1.0 + jnp.exp(-SWIGLU_ALPHA * h_glu))) * (h_lin + 1.0))
        act = jnp.concatenate(acts, axis=1)
        o_ref[...] = _dot(act.astype(MXU_DTYPE), wd_ref[0]) + bd_ref[0]

    @pl.when(i >= n_used)
    def _():
        o_ref[...] = jnp.zeros(o_ref.shape, o_ref.dtype)


def _moe_experts(x1, row_token, tile_expert, n_used, wgu, bgu, wd, bd):
    t, d = x1.shape
    f2 = wgu.shape[2]
    n_tiles = tile_expert.shape[0]
    wmap = lambda i, rt, te, nu: (te[i], 0, 0)
    grid_spec = pltpu.PrefetchScalarGridSpec(
        num_scalar_prefetch=3,
        grid=(n_tiles,),
        in_specs=[pl.BlockSpec(memory_space=pl.ANY),
                  pl.BlockSpec((1, d, f2), wmap), pl.BlockSpec((1, 1, f2), wmap),
                  pl.BlockSpec((1, f2 // 2, d), wmap), pl.BlockSpec((1, 1, d), wmap)],
        out_specs=pl.BlockSpec((MOE_TILE, d), lambda i, rt, te, nu: (i, 0)),
        scratch_shapes=[pltpu.VMEM((2, MOE_TILE, d), jnp.float32), pltpu.SemaphoreType.DMA((2,))])
    return pl.pallas_call(
        _moe_kernel,
        out_shape=jax.ShapeDtypeStruct((n_tiles * MOE_TILE, d), jnp.float32),
        grid_spec=grid_spec, compiler_params=_params("arbitrary"),
    )(row_token, tile_expert, n_used, x1, wgu, bgu, wd, bd)


def _combine_kernel(pos_ref, y_hbm, x_ref, gate_ref, g_ref, b_ref, o_ref, ob_ref, ybuf, sem, *, alpha):
    i = pl.program_id(0)
    n_rows = COMBINE_TILE * TOP_K

    def gather(tile, slot):
        def body(r, carry):
            src = pos_ref[tile * n_rows + r]
            pltpu.make_async_copy(y_hbm.at[pl.ds(src, 1)], ybuf.at[slot, pl.ds(r, 1)], sem.at[slot]).start()
            return carry
        lax.fori_loop(0, n_rows, body, 0)

    @pl.when(i == 0)
    def _():
        gather(0, 0)

    @pl.when(i + 1 < pl.num_programs(0))
    def _():
        gather(i + 1, (i + 1) % 2)

    slot = i % 2
    pltpu.make_async_copy(y_hbm.at[pl.ds(0, n_rows)], ybuf.at[slot], sem.at[slot]).wait()
    gate = gate_ref[...]
    ffn = jnp.zeros(x_ref.shape, jnp.float32)
    for k in range(TOP_K):
        ffn = ffn + gate[:, k:k + 1] * ybuf[slot, pl.ds(k * COMBINE_TILE, COMBINE_TILE)]
    x2 = _layer_norm(alpha * x_ref[...] + ffn, g_ref[...], b_ref[...])
    o_ref[...] = x2
    ob_ref[...] = x2.astype(ob_ref.dtype)


def _combine_ln(y, pos, x1, gate, ln_g, ln_b, alpha):
    t, d = x1.shape
    row = lambda i, p: (i, 0)
    const = lambda i, p: (0, 0)
    grid_spec = pltpu.PrefetchScalarGridSpec(
        num_scalar_prefetch=1,
        grid=(t // COMBINE_TILE,),
        in_specs=[pl.BlockSpec(memory_space=pl.ANY),
                  pl.BlockSpec((COMBINE_TILE, d), row), pl.BlockSpec((COMBINE_TILE, LANES), row),
                  pl.BlockSpec((1, d), const), pl.BlockSpec((1, d), const)],
        out_specs=(pl.BlockSpec((COMBINE_TILE, d), row), pl.BlockSpec((COMBINE_TILE, d), row)),
        scratch_shapes=[pltpu.VMEM((2, COMBINE_TILE * TOP_K, d), jnp.float32),
                        pltpu.SemaphoreType.DMA((2,))])
    return pl.pallas_call(
        functools.partial(_combine_kernel, alpha=alpha),
        out_shape=(jax.ShapeDtypeStruct((t, d), jnp.float32), jax.ShapeDtypeStruct((t, d), MXU_DTYPE)),
        grid_spec=grid_spec, compiler_params=_params("arbitrary"),
    )(pos, y, x1, gate, ln_g.reshape(1, d), ln_b.reshape(1, d))


def _route(top_idx, n_experts):
    t = top_idx.shape[0]
    n_assign = t * TOP_K
    n_tiles = n_assign // MOE_TILE + n_experts
    expert = top_idx.reshape(n_assign)
    onehot = (expert[:, None] == jnp.arange(n_experts, dtype=jnp.int32)[None, :]).astype(jnp.int32)
    running = jnp.cumsum(onehot, axis=0)
    counts = running[-1]
    rank = jnp.take_along_axis(running, expert[:, None], axis=1)[:, 0] - 1
    padded = (counts + MOE_TILE - 1) // MOE_TILE * MOE_TILE
    pad_end = jnp.cumsum(padded)
    dest = (pad_end - padded)[expert] + rank
    slot = dest.reshape(t // COMBINE_TILE, COMBINE_TILE, TOP_K).transpose(0, 2, 1).reshape(n_assign)
    token = jnp.arange(n_assign, dtype=jnp.int32) // TOP_K
    row_token = jnp.zeros((n_tiles * MOE_TILE,), jnp.int32).at[dest].set(token)
    tile_start = jnp.arange(n_tiles, dtype=jnp.int32) * MOE_TILE
    tile_expert = jnp.minimum(
        jnp.sum((pad_end[None, :] <= tile_start[:, None]).astype(jnp.int32), axis=1), n_experts - 1)
    n_used = (pad_end[-1:] // MOE_TILE).astype(jnp.int32)
    return row_token, tile_expert, n_used, slot.astype(jnp.int32)


def _rope_tables(positions):
    inv_freq = 1.0 / (ROPE_THETA ** (jnp.arange(0, ROT_DIM, 2, dtype=jnp.float32) / ROT_DIM))
    ang = positions.astype(jnp.float32).reshape(-1)[:, None] * inv_freq
    cos, sin = jnp.cos(ang), jnp.sin(ang)
    rest = HEAD_DIM - ROT_DIM
    n = ang.shape[0]
    ct = jnp.concatenate([cos, cos, jnp.ones((n, rest), jnp.float32)], axis=1)
    sa = jnp.concatenate([jnp.zeros_like(sin), sin, jnp.zeros((n, rest), jnp.float32)], axis=1)
    sb = jnp.concatenate([-sin, jnp.zeros_like(sin), jnp.zeros((n, rest), jnp.float32)], axis=1)
    return ct, sa, sb


def _gate_weight(w_in_l, lay):
    d = w_in_l.shape[0]
    kv = lay['kv']
    o = lay['ref_off']['ngate']
    wg = w_in_l[:, o:o + lay['size']['ngate']].reshape(d, kv, NSA_GROUP * 3)
    wg = jnp.pad(wg, ((0, 0), (0, 0), (0, LANES - NSA_GROUP * 3)))
    return wg.reshape(d, kv * LANES).astype(MXU_DTYPE)


def kernel(x, positions, w_in, nsa_cmp_pos, nsa_cmp_w1, nsa_cmp_w2, diff_lambda, diff_subln_g, w_out,
           ln1_g, ln1_b, w_router, b_router, w_gate_up, b_gate_up, w_down, b_down, ln2_g, ln2_b):
    b, s, d = x.shape
    depth = w_in.shape[0]
    n_experts = w_router.shape[2]
    lay = _layout(d)
    alpha = (2 * depth) ** 0.25
    ct, sa, sb = _rope_tables(positions)
    xf = x.reshape(b * s, d).astype(jnp.float32)
    xb = xf.astype(MXU_DTYPE)
    for layer in range(depth):
        w_l = w_in[layer]
        w_perm = jnp.concatenate(
            [w_l[:, lay['ref_off'][n]:lay['ref_off'][n] + lay['size'][n]] for n in ROPED + PLAIN],
            axis=1).astype(MXU_DTYPE)
        hp = _project(xb, w_perm, ct, sa, sb, lay['n_roped'])
        gate_logits = _gate_logits(xb, _gate_weight(w_l, lay))

        kcv = _nsa_compress(hp, nsa_cmp_pos[layer], nsa_cmp_w1[layer], nsa_cmp_w2[layer], lay, b, s)
        o_cmp, sel = _nsa_cmp_select(hp, kcv, lay, b, s)
        y_nsa = _nsa_main(hp, sel, o_cmp, gate_logits, lay, b, s)
        lambda_init = 0.8 - 0.6 * math.exp(-0.3 * layer)
        y_diff = _diff_attention(hp, diff_lambda[layer], diff_subln_g[layer], lay, b, s, lambda_init)
        y_moba = _moba_attention(hp, lay, b, s)
        mix = jnp.concatenate([y_nsa, y_diff, y_moba], axis=1)

        x1, top_idx, top_gate = _out_proj_ln_router(
            mix, w_out[layer].astype(MXU_DTYPE), xf, ln1_g[layer], ln1_b[layer],
            w_router[layer], b_router[layer], alpha)

        row_token, tile_expert, n_used, slot = _route(top_idx[:, :TOP_K], n_experts)
        y = _moe_experts(
            x1, row_token, tile_expert, n_used,
            _regroup_gate_up(w_gate_up[layer]), _regroup_bias(b_gate_up[layer]),
            w_down[layer].astype(MXU_DTYPE), b_down[layer].astype(jnp.float32).reshape(n_experts, 1, d))
        xf, xb = _combine_ln(y, slot, x1, top_gate, ln2_g[layer], ln2_b[layer], alpha)
    return xf.reshape(b, s, d).astype(x.dtype)
```

```python
import functools
import math

import numpy as np
import jax
import jax.numpy as jnp
from jax import lax
from jax.experimental import pallas as pl
from jax.experimental.pallas import tpu as pltpu

HEAD_DIM = 128
ROT_DIM = HEAD_DIM // 4
ROT_HALF = ROT_DIM // 2
ROPE_THETA = 500000.0
ATTN_SCALE = HEAD_DIM ** -0.5
NEG_INF = -1e30
LN_EPS = 1e-5

NSA_GROUP = 4
CMP_LEN = 32
CMP_STRIDE = 16
SEL_LEN = 64
SEL_TOPN = 16
WINDOW = 512
CMP_PER_SEL = SEL_LEN // CMP_STRIDE

DIFF_VDIM = 2 * HEAD_DIM
MOBA_BLOCK = 256
MOBA_TOPK = 3

TOP_K = 4
SWIGLU_LIMIT = 7.0
SWIGLU_ALPHA = 1.702

LANES = 128
MXU_DTYPE = jnp.bfloat16
VMEM_LIMIT = 56 * 1024 * 1024

ROPED = ('nq', 'nkc', 'nks', 'nkw', 'dq', 'dk', 'mq', 'mk')
PLAIN = ('nvc', 'nvs', 'nvw', 'dv', 'mv')
REF_ORDER = ('nq', 'nkc', 'nvc', 'nks', 'nvs', 'nkw', 'nvw', 'ngate', 'dq', 'dk', 'dv', 'mq', 'mk', 'mv')


def _layout(d):
    nsa_h = d // 256
    kv = nsa_h // NSA_GROUP
    diff_h = d // 1024
    moba_h = d // 512
    size = dict(nq=nsa_h * HEAD_DIM, nkc=kv * HEAD_DIM, nvc=kv * HEAD_DIM, nks=kv * HEAD_DIM,
                nvs=kv * HEAD_DIM, nkw=kv * HEAD_DIM, nvw=kv * HEAD_DIM, ngate=3 * nsa_h,
                dq=2 * diff_h * HEAD_DIM, dk=2 * diff_h * HEAD_DIM, dv=diff_h * DIFF_VDIM,
                mq=moba_h * HEAD_DIM, mk=moba_h * HEAD_DIM, mv=moba_h * HEAD_DIM)
    ref_off, o = {}, 0
    for n in REF_ORDER:
        ref_off[n] = o
        o += size[n]
    off, o = {}, 0
    for n in ROPED + PLAIN:
        off[n] = o
        o += size[n]
    n_roped = sum(size[n] for n in ROPED)
    return dict(nsa_h=nsa_h, kv=kv, diff_h=diff_h, moba_h=moba_h, size=size, ref_off=ref_off,
                off=off, n_roped=n_roped, n_cols=o)


def _params(*sem):
    return pltpu.CompilerParams(dimension_semantics=sem, vmem_limit_bytes=VMEM_LIMIT)


def _dot(a, b):
    return jnp.dot(a, b, preferred_element_type=jnp.float32)


def _dot_nt(a, b):
    return lax.dot_general(a, b, (((1,), (1,)), ((), ())), preferred_element_type=jnp.float32)


def _iota(shape, dim):
    return lax.broadcasted_iota(jnp.int32, shape, dim)


def _lane_index(shape):
    return _iota(shape, 1).astype(jnp.float32)


def _first_lane(hit, lane):
    return jnp.min(jnp.where(hit, lane, float(LANES)), axis=-1, keepdims=True)


def _proj_kernel(x_ref, w_ref, ct_ref, sa_ref, sb_ref, o_ref, *, n_rope_blocks, tn):
    j = pl.program_id(1)
    acc = _dot(x_ref[...], w_ref[...])

    @pl.when(j < n_rope_blocks)
    def _():
        ct, sa, sb = ct_ref[...], sa_ref[...], sb_ref[...]
        for c in range(tn // HEAD_DIM):
            a = acc[:, c * HEAD_DIM:(c + 1) * HEAD_DIM]
            r = (a * ct + pltpu.roll(a, ROT_HALF, 1) * sa
                 + pltpu.roll(a, HEAD_DIM - ROT_HALF, 1) * sb)
            o_ref[:, c * HEAD_DIM:(c + 1) * HEAD_DIM] = r.astype(o_ref.dtype)

    @pl.when(j >= n_rope_blocks)
    def _():
        o_ref[...] = acc.astype(o_ref.dtype)


def _project(xb, w, ct, sa, sb, n_roped):
    m, k = xb.shape
    n = w.shape[1]
    tm = min(1024, m)
    tn = next(t for t in (512, 256, 128) if n % t == 0 and n_roped % t == 0)
    kern = functools.partial(_proj_kernel, n_rope_blocks=n_roped // tn, tn=tn)
    return pl.pallas_call(
        kern,
        out_shape=jax.ShapeDtypeStruct((m, n), MXU_DTYPE),
        grid=(m // tm, n // tn),
        in_specs=[pl.BlockSpec((tm, k), lambda i, j: (i, 0)),
                  pl.BlockSpec((k, tn), lambda i, j: (0, j)),
                  pl.BlockSpec((tm, HEAD_DIM), lambda i, j: (i, 0)),
                  pl.BlockSpec((tm, HEAD_DIM), lambda i, j: (i, 0)),
                  pl.BlockSpec((tm, HEAD_DIM), lambda i, j: (i, 0))],
        out_specs=pl.BlockSpec((tm, tn), lambda i, j: (i, j)),
        compiler_params=_params("parallel", "arbitrary"),
    )(xb, w, ct, sa, sb)


def _gate_kernel(x_ref, w_ref, o_ref):
    o_ref[...] = _dot(x_ref[...], w_ref[...])


def _gate_logits(xb, wg):
    m, k = xb.shape
    n = wg.shape[1]
    tm = min(1024, m)
    return pl.pallas_call(
        _gate_kernel,
        out_shape=jax.ShapeDtypeStruct((m, n), jnp.float32),
        grid=(m // tm,),
        in_specs=[pl.BlockSpec((tm, k), lambda i: (i, 0)),
                  pl.BlockSpec((k, n), lambda i: (0, 0))],
        out_specs=pl.BlockSpec((tm, n), lambda i: (i, 0)),
        compiler_params=_params("parallel"),
    )(xb, wg)


EXP2_SCALE = ATTN_SCALE * math.log2(math.e)


def _mask_bias(mask):
    return jnp.where(mask, 0.0, -jnp.inf)


def _lanes(x, n):
    return x if n == LANES else jnp.concatenate([x] * (n // LANES), axis=1)


def _flash_step(s, v, m_ref, l_ref, acc_ref):
    m_old = m_ref[...]
    m_new = jnp.maximum(m_old, jnp.max(s, axis=-1, keepdims=True))
    p = jnp.exp2((s - _lanes(m_new, s.shape[1])) * EXP2_SCALE)
    alpha = jnp.exp2((m_old - m_new) * EXP2_SCALE)
    l_ref[...] = alpha * l_ref[...] + jnp.sum(p, axis=-1, keepdims=True)
    acc_ref[...] = _lanes(alpha, acc_ref.shape[1]) * acc_ref[...] + _dot(p.astype(v.dtype), v)
    m_ref[...] = m_new


def _flash_result(l_ref, acc_ref, rows):
    denom = jnp.maximum(l_ref[rows, :], 1e-30)
    return acc_ref[rows, :] / _lanes(denom, acc_ref.shape[1])


def _init_state(m_ref, l_ref, acc_ref):
    m_ref[...] = jnp.full(m_ref.shape, NEG_INF, jnp.float32)
    l_ref[...] = jnp.zeros(l_ref.shape, jnp.float32)
    acc_ref[...] = jnp.zeros(acc_ref.shape, jnp.float32)


def _causal_pairs(nq, tq, tk):
    qi, ki = [], []
    for q in range(nq):
        last = (q * tq + tq - 1) // tk
        for k in range(last + 1):
            qi.append(q)
            ki.append(k)
    return np.asarray(qi, np.int32), np.asarray(ki, np.int32)


def _diff_kernel(qi_tab, ki_tab, lam_ref, g_ref, q_ref, k_ref, v_ref, o_ref, m_ref, l_ref, acc_ref,
                 *, tq, tk, lambda_init):
    p_id = pl.program_id(2)
    qi = qi_tab[p_id]
    ki = ki_tab[p_id]
    last = (qi * tq + tq - 1) // tk

    @pl.when(ki == 0)
    def _():
        _init_state(m_ref, l_ref, acc_ref)

    def step(masked):
        maps = []
        for mp in range(2):
            q = q_ref[:, mp * HEAD_DIM:(mp + 1) * HEAD_DIM]
            k = k_ref[:, mp * HEAD_DIM:(mp + 1) * HEAD_DIM]
            maps.append(_dot_nt(q, k))
        if masked:
            t = qi * tq + _iota((tq, tk), 0)
            u = ki * tk + _iota((tq, tk), 1)
            bias = _mask_bias(u <= t)
            maps = [s + bias for s in maps]
        _flash_step(jnp.concatenate(maps, axis=0), v_ref[...], m_ref, l_ref, acc_ref)

    below = (ki + 1) * tk - 1 <= qi * tq

    @pl.when(below)
    def _():
        step(False)

    @pl.when(jnp.logical_not(below))
    def _():
        step(True)

    @pl.when(ki == last)
    def _():
        lv = lam_ref[...]
        lam = (jnp.exp(jnp.sum(lv[0:1] * lv[1:2], axis=-1, keepdims=True))
               - jnp.exp(jnp.sum(lv[2:3] * lv[3:4], axis=-1, keepdims=True)) + lambda_init)
        o = (_flash_result(l_ref, acc_ref, pl.ds(0, tq))
             - lam * _flash_result(l_ref, acc_ref, pl.ds(tq, tq)))
        o = o * lax.rsqrt(jnp.mean(jnp.square(o), axis=-1, keepdims=True) + LN_EPS) * g_ref[...]
        o_ref[...] = (o * (1.0 - lambda_init)).astype(o_ref.dtype)


def _diff_attention(hp, lam_vecs, subln_g, lay, b, s, lambda_init):
    hd = lay['diff_h']
    tq = tk = min(512, s)
    nq = s // tq
    nk = s // tk
    qi_tab, ki_tab = _causal_pairs(nq, tq, tk)
    qb, kb, vb = (lay['off'][n] // DIFF_VDIM for n in ('dq', 'dk', 'dv'))
    assert all(lay['off'][n] % DIFF_VDIM == 0 for n in ('dq', 'dk', 'dv'))
    kern = functools.partial(_diff_kernel, tq=tq, tk=tk, lambda_init=lambda_init)
    grid_spec = pltpu.PrefetchScalarGridSpec(
        num_scalar_prefetch=2,
        grid=(b, hd, len(qi_tab)),
        in_specs=[pl.BlockSpec((4, HEAD_DIM), lambda bi, h, p, qt, kt: (0, 0)),
                  pl.BlockSpec((1, DIFF_VDIM), lambda bi, h, p, qt, kt: (0, 0)),
                  pl.BlockSpec((tq, DIFF_VDIM), lambda bi, h, p, qt, kt: (bi * nq + qt[p], qb + h)),
                  pl.BlockSpec((tk, DIFF_VDIM), lambda bi, h, p, qt, kt: (bi * nk + kt[p], kb + h)),
                  pl.BlockSpec((tk, DIFF_VDIM), lambda bi, h, p, qt, kt: (bi * nk + kt[p], vb + h))],
        out_specs=pl.BlockSpec((tq, DIFF_VDIM), lambda bi, h, p, qt, kt: (bi * nq + qt[p], h)),
        scratch_shapes=[pltpu.VMEM((2 * tq, LANES), jnp.float32), pltpu.VMEM((2 * tq, LANES), jnp.float32),
                        pltpu.VMEM((2 * tq, DIFF_VDIM), jnp.float32)])
    return pl.pallas_call(
        kern, out_shape=jax.ShapeDtypeStruct((b * s, hd * DIFF_VDIM), MXU_DTYPE),
        grid_spec=grid_spec, compiler_params=_params("parallel", "parallel", "arbitrary"),
    )(jnp.asarray(qi_tab), jnp.asarray(ki_tab), lam_vecs.astype(jnp.float32),
      subln_g.reshape(1, DIFF_VDIM).astype(jnp.float32), hp, hp, hp)


def _kmean_kernel(k_ref, o_ref, *, n_blk):
    k = k_ref[...].astype(jnp.float32).reshape(n_blk, MOBA_BLOCK, HEAD_DIM)
    o_ref[0, 0] = jnp.mean(k, axis=1)


def _moba_kmean(hp, lay, b, s):
    h = lay['moba_h']
    n_blk = s // MOBA_BLOCK
    kb = lay['off']['mk'] // HEAD_DIM
    return pl.pallas_call(
        functools.partial(_kmean_kernel, n_blk=n_blk),
        out_shape=jax.ShapeDtypeStruct((b, h, n_blk, HEAD_DIM), jnp.float32),
        grid=(b, h),
        in_specs=[pl.BlockSpec((s, HEAD_DIM), lambda bi, hi: (bi, kb + hi))],
        out_specs=pl.BlockSpec((1, 1, n_blk, HEAD_DIM), lambda bi, hi: (bi, hi, 0, 0)),
        compiler_params=_params("parallel", "parallel"),
    )(hp)


def _lane_column(x, n):
    return jnp.sum(jnp.where(_iota(x.shape, 1) == n, x, 0.0), axis=-1, keepdims=True)


def _moba_kernel(qi_tab, ki_tab, q_ref, k_ref, v_ref, km_ref, o_ref, sel_ref, m_ref, l_ref, acc_ref,
                 *, tq, tk):
    p_id = pl.program_id(2)
    qi = qi_tab[p_id]
    ki = ki_tab[p_id]
    last = (qi * tq + tq - 1) // tk
    q = q_ref[...]

    @pl.when(ki == 0)
    def _():
        _init_state(m_ref, l_ref, acc_ref)
        score = _dot_nt(q, km_ref[0, 0].astype(q.dtype))
        blk = _lane_index((tq, LANES))
        own = ((qi * tq + _iota((tq, LANES), 0)) // MOBA_BLOCK).astype(jnp.float32)
        work = jnp.where(blk < own, score, -jnp.inf)
        sel = jnp.where(blk == own, 1.0, 0.0)
        for r in range(MOBA_TOPK):
            mx = jnp.max(work, axis=-1, keepdims=True)
            first = _first_lane(work == mx, blk)
            pick = blk == first
            sel = jnp.where(pick & (own > r), 1.0, sel)
            work = jnp.where(pick, -jnp.inf, work)
        sel_ref[...] = sel

    t = qi * tq + _iota((tq, tk), 0)
    u = ki * tk + _iota((tq, tk), 1)
    sel = sel_ref[...]
    cols = []
    for c in range(tk // MOBA_BLOCK):
        col = _lane_column(sel, ki * (tk // MOBA_BLOCK) + c)
        cols.append(jnp.broadcast_to(col, (tq, MOBA_BLOCK)))
    allowed = cols[0] if len(cols) == 1 else jnp.concatenate(cols, axis=1)
    bias = _mask_bias((allowed > 0.5) & (u <= t))
    _flash_step(_dot_nt(q, k_ref[...]) + bias, v_ref[...], m_ref, l_ref, acc_ref)

    @pl.when(ki == last)
    def _():
        o_ref[...] = _flash_result(l_ref, acc_ref, pl.ds(0, tq)).astype(o_ref.dtype)


def _moba_attention(hp, lay, b, s):
    h = lay['moba_h']
    n_blk = s // MOBA_BLOCK
    assert s % MOBA_BLOCK == 0 and MOBA_TOPK <= n_blk <= LANES
    km = _moba_kmean(hp, lay, b, s)
    km = jnp.pad(km, ((0, 0), (0, 0), (0, LANES - n_blk), (0, 0)))
    tq = min(1024, s)
    tk = min(512, s)
    nq, nk = s // tq, s // tk
    qi_tab, ki_tab = _causal_pairs(nq, tq, tk)
    qb, kb, vb = (lay['off'][n] // HEAD_DIM for n in ('mq', 'mk', 'mv'))
    grid_spec = pltpu.PrefetchScalarGridSpec(
        num_scalar_prefetch=2,
        grid=(b, h, len(qi_tab)),
        in_specs=[pl.BlockSpec((tq, HEAD_DIM), lambda bi, hi, p, qt, kt: (bi * nq + qt[p], qb + hi)),
                  pl.BlockSpec((tk, HEAD_DIM), lambda bi, hi, p, qt, kt: (bi * nk + kt[p], kb + hi)),
                  pl.BlockSpec((tk, HEAD_DIM), lambda bi, hi, p, qt, kt: (bi * nk + kt[p], vb + hi)),
                  pl.BlockSpec((1, 1, LANES, HEAD_DIM), lambda bi, hi, p, qt, kt: (bi, hi, 0, 0))],
        out_specs=pl.BlockSpec((tq, HEAD_DIM), lambda bi, hi, p, qt, kt: (bi * nq + qt[p], hi)),
        scratch_shapes=[pltpu.VMEM((tq, LANES), jnp.float32),
                        pltpu.VMEM((tq, LANES), jnp.float32), pltpu.VMEM((tq, LANES), jnp.float32),
                        pltpu.VMEM((tq, HEAD_DIM), jnp.float32)])
    return pl.pallas_call(
        functools.partial(_moba_kernel, tq=tq, tk=tk),
        out_shape=jax.ShapeDtypeStruct((b * s, h * HEAD_DIM), MXU_DTYPE),
        grid_spec=grid_spec, compiler_params=_params("parallel", "parallel", "arbitrary"),
    )(jnp.asarray(qi_tab), jnp.asarray(ki_tab), hp, hp, hp, km)


def _gelu_tanh(x):
    return 0.5 * x * (1.0 + jnp.tanh(math.sqrt(2.0 / math.pi) * (x + 0.044715 * (x * x * x))))


def _compress_kernel(seg_ref, plo_ref, phi_ref, w1lo_ref, w1hi_ref, w2_ref, o_ref, *, n_seg):
    seg = seg_ref[0, 0, 0].astype(jnp.float32)
    lo = _dot((seg + plo_ref[0]).astype(MXU_DTYPE), w1lo_ref[0])
    hi = _dot((seg + phi_ref[0]).astype(MXU_DTYPE), w1hi_ref[0])
    pre = lo + pltpu.roll(hi, n_seg - 1, 0)
    o_ref[0, 0, 0] = _dot(_gelu_tanh(pre).astype(MXU_DTYPE), w2_ref[0]).astype(o_ref.dtype)


def _nsa_compress(hp, cmp_pos, cmp_w1, cmp_w2, lay, b, s):
    kv = lay['kv']
    n_seg = s // CMP_STRIDE
    half = CMP_STRIDE * HEAD_DIM

    def segments(name):
        o = lay['off'][name]
        t = hp[:, o:o + kv * HEAD_DIM].reshape(b, n_seg, CMP_STRIDE, kv, HEAD_DIM)
        return t.transpose(0, 3, 1, 2, 4).reshape(b, kv, n_seg, half)

    seg = jnp.stack([segments('nkc'), segments('nvc')], axis=1)
    pos = cmp_pos.astype(jnp.float32).reshape(2, CMP_LEN * HEAD_DIM)
    plo = pos[:, :half].reshape(2, 1, half)
    phi = pos[:, half:].reshape(2, 1, half)
    w1 = cmp_w1.astype(MXU_DTYPE)
    w1lo, w1hi = w1[:, :half], w1[:, half:]
    w2 = cmp_w2.astype(MXU_DTYPE)
    return pl.pallas_call(
        functools.partial(_compress_kernel, n_seg=n_seg),
        out_shape=jax.ShapeDtypeStruct((b, 2, kv, n_seg, HEAD_DIM), MXU_DTYPE),
        grid=(b, 2, kv),
        in_specs=[pl.BlockSpec((1, 1, 1, n_seg, half), lambda bi, c, k: (bi, c, k, 0, 0)),
                  pl.BlockSpec((1, 1, half), lambda bi, c, k: (c, 0, 0)),
                  pl.BlockSpec((1, 1, half), lambda bi, c, k: (c, 0, 0)),
                  pl.BlockSpec((1, half, HEAD_DIM), lambda bi, c, k: (c, 0, 0)),
                  pl.BlockSpec((1, half, HEAD_DIM), lambda bi, c, k: (c, 0, 0)),
                  pl.BlockSpec((1, HEAD_DIM, HEAD_DIM), lambda bi, c, k: (c, 0, 0))],
        out_specs=pl.BlockSpec((1, 1, 1, n_seg, HEAD_DIM), lambda bi, c, k: (bi, c, k, 0, 0)),
        compiler_params=_params("parallel", "parallel", "parallel"),
    )(seg, plo, phi, w1lo, w1hi, w2)


def _nsa_cmp_kernel(q_ref, kc_ref, vc_ref, o_ref, sel_ref, *, tq, n_sel, n_cmp):
    qi = pl.program_id(2)
    width = CMP_PER_SEL * LANES
    t = qi * tq + _iota((tq, width), 0)
    pos = _iota((tq, width), 1)
    m_idx = pos % LANES
    n_idx = CMP_PER_SEL * m_idx + pos // LANES
    valid = (m_idx < n_sel) & (n_idx < n_cmp) & (n_idx * CMP_STRIDE + CMP_LEN - 1 <= t)
    kc = kc_ref[0, 0, 0]
    vc = vc_ref[0, 0, 0]
    p_grp = jnp.zeros((tq, width), jnp.float32)
    for g in range(NSA_GROUP):
        q = q_ref[:, g * HEAD_DIM:(g + 1) * HEAD_DIM]
        s = jnp.where(valid, _dot_nt(q, kc) * ATTN_SCALE, NEG_INF)
        p = jnp.where(valid, jnp.exp(s - jnp.max(s, axis=-1, keepdims=True)), 0.0)
        p = p / jnp.maximum(jnp.sum(p, axis=-1, keepdims=True), 1e-30)
        o_ref[:, g * HEAD_DIM:(g + 1) * HEAD_DIM] = _dot(p.astype(vc.dtype), vc)
        p_grp = p_grp + p

    slabs = [p_grp[:, r * LANES:(r + 1) * LANES] for r in range(CMP_PER_SEL)]
    blk = _lane_index((tq, LANES))
    prev = jnp.where(blk == 0, 0.0, pltpu.roll(slabs[CMP_PER_SEL - 1], 1, 1))
    imp = prev
    for r in range(CMP_PER_SEL):
        imp = imp + slabs[r]
    cur = ((qi * tq + _iota((tq, LANES), 0)) // SEL_LEN).astype(jnp.float32)
    forced = (blk == 0) | (blk == cur) | (blk == cur - 1)
    work = jnp.where(forced, jnp.inf, imp)
    work = jnp.where(blk <= cur, work, -jnp.inf)
    sel = jnp.zeros((tq, LANES), jnp.float32)
    for _ in range(SEL_TOPN):
        mx = jnp.max(work, axis=-1, keepdims=True)
        first = _first_lane(work == mx, blk)
        pick = blk == first
        sel = jnp.where(pick, 1.0, sel)
        work = jnp.where(pick, -jnp.inf, work)
    sel_ref[0, 0] = jnp.where(blk <= cur, sel, 0.0).astype(sel_ref.dtype)


def _nsa_cmp_select(hp, kcv, lay, b, s):
    kv = lay['kv']
    n_seg = s // CMP_STRIDE
    n_sel = s // SEL_LEN
    n_cmp = (s - CMP_LEN) // CMP_STRIDE + 1
    assert SEL_TOPN <= n_sel <= LANES
    width = CMP_PER_SEL * LANES
    kcp = kcv.reshape(b, 2, kv, n_sel, CMP_PER_SEL, HEAD_DIM).transpose(0, 1, 2, 4, 3, 5)
    kcp = jnp.pad(kcp, ((0, 0),) * 4 + ((0, LANES - n_sel), (0, 0))).reshape(b, 2, kv, width, HEAD_DIM)
    tq = min(512, s)
    nq = s // tq
    gw = NSA_GROUP * HEAD_DIM
    return pl.pallas_call(
        functools.partial(_nsa_cmp_kernel, tq=tq, n_sel=n_sel, n_cmp=n_cmp),
        out_shape=(jax.ShapeDtypeStruct((b * s, lay['nsa_h'] * HEAD_DIM), jnp.float32),
                   jax.ShapeDtypeStruct((b, kv, s, LANES), MXU_DTYPE)),
        grid=(b, kv, nq),
        in_specs=[pl.BlockSpec((tq, gw), lambda bi, k, qi: (bi * nq + qi, k)),
                  pl.BlockSpec((1, 1, 1, width, HEAD_DIM), lambda bi, k, qi: (bi, 0, k, 0, 0)),
                  pl.BlockSpec((1, 1, 1, width, HEAD_DIM), lambda bi, k, qi: (bi, 1, k, 0, 0))],
        out_specs=(pl.BlockSpec((tq, gw), lambda bi, k, qi: (bi * nq + qi, k)),
                   pl.BlockSpec((1, 1, tq, LANES), lambda bi, k, qi: (bi, k, qi, 0))),
        compiler_params=_params("parallel", "parallel", "parallel"),
    )(hp, kcp, kcp)


def _nsa_main_kernel(qi_tab, ki_tab, q_ref, ks_ref, vs_ref, kw_ref, vw_ref, sel_ref, e_ref, oc_ref,
                     gl_ref, o_ref, ms_ref, ls_ref, as_ref, mw_ref, lw_ref, aw_ref, *, tq, tk):
    p_id = pl.program_id(2)
    qi = qi_tab[p_id]
    ki = ki_tab[p_id]
    last = (qi * tq + tq - 1) // tk
    first_win = jnp.maximum((qi * tq - WINDOW + 1) // tk, 0)

    @pl.when(ki == 0)
    def _():
        _init_state(ms_ref, ls_ref, as_ref)
        _init_state(mw_ref, lw_ref, aw_ref)

    t = qi * tq + _iota((tq, tk), 0)
    u = ki * tk + _iota((tq, tk), 1)
    chosen = _dot(sel_ref[0, 0], e_ref[...])
    b_sel = _mask_bias((chosen > 0.5) & (u <= t))
    q_all = jnp.concatenate([q_ref[:, g * HEAD_DIM:(g + 1) * HEAD_DIM] for g in range(NSA_GROUP)], axis=0)

    def scores(k, bias):
        s = _dot_nt(q_all, k).reshape(NSA_GROUP, tq, tk) + bias[None]
        return s.reshape(NSA_GROUP * tq, tk)

    _flash_step(scores(ks_ref[...], b_sel), vs_ref[...], ms_ref, ls_ref, as_ref)

    @pl.when(ki >= first_win)
    def _():
        b_win = _mask_bias((u <= t) & (u > t - WINDOW))
        _flash_step(scores(kw_ref[...], b_win), vw_ref[...], mw_ref, lw_ref, aw_ref)

    @pl.when(ki == last)
    def _():
        gate = 1.0 / (1.0 + jnp.exp(-gl_ref[...]))
        for g in range(NSA_GROUP):
            o_cmp = oc_ref[:, g * HEAD_DIM:(g + 1) * HEAD_DIM]
            o_slc = _flash_result(ls_ref, as_ref, pl.ds(g * tq, tq))
            o_win = _flash_result(lw_ref, aw_ref, pl.ds(g * tq, tq))
            out = (gate[:, 3 * g:3 * g + 1] * o_cmp + gate[:, 3 * g + 1:3 * g + 2] * o_slc
                   + gate[:, 3 * g + 2:3 * g + 3] * o_win)
            o_ref[:, g * HEAD_DIM:(g + 1) * HEAD_DIM] = out.astype(o_ref.dtype)


def _nsa_main(hp, sel, o_cmp, gate_logits, lay, b, s):
    kv = lay['kv']
    tq = min(256, s)
    tk = min(512, s)
    nq, nk = s // tq, s // tk
    qi_tab, ki_tab = _causal_pairs(nq, tq, tk)
    gw = NSA_GROUP * HEAD_DIM
    ksb, vsb, kwb, vwb = (lay['off'][n] // HEAD_DIM for n in ('nks', 'nvs', 'nkw', 'nvw'))
    expand = (np.arange(s)[None, :] // SEL_LEN == np.arange(LANES)[:, None])
    expand = jnp.asarray(expand, MXU_DTYPE)

    def win_blk(qt, kt, p):
        return jnp.maximum(kt[p], jnp.maximum((qt[p] * tq - WINDOW + 1) // tk, 0))

    grid_spec = pltpu.PrefetchScalarGridSpec(
        num_scalar_prefetch=2,
        grid=(b, kv, len(qi_tab)),
        in_specs=[pl.BlockSpec((tq, gw), lambda bi, k, p, qt, kt: (bi * nq + qt[p], k)),
                  pl.BlockSpec((tk, HEAD_DIM), lambda bi, k, p, qt, kt: (bi * nk + kt[p], ksb + k)),
                  pl.BlockSpec((tk, HEAD_DIM), lambda bi, k, p, qt, kt: (bi * nk + kt[p], vsb + k)),
                  pl.BlockSpec((tk, HEAD_DIM),
                               lambda bi, k, p, qt, kt: (bi * nk + win_blk(qt, kt, p), kwb + k)),
                  pl.BlockSpec((tk, HEAD_DIM),
                               lambda bi, k, p, qt, kt: (bi * nk + win_blk(qt, kt, p), vwb + k)),
                  pl.BlockSpec((1, 1, tq, LANES), lambda bi, k, p, qt, kt: (bi, k, qt[p], 0)),
                  pl.BlockSpec((LANES, tk), lambda bi, k, p, qt, kt: (0, kt[p])),
                  pl.BlockSpec((tq, gw), lambda bi, k, p, qt, kt: (bi * nq + qt[p], k)),
                  pl.BlockSpec((tq, LANES), lambda bi, k, p, qt, kt: (bi * nq + qt[p], k))],
        out_specs=pl.BlockSpec((tq, gw), lambda bi, k, p, qt, kt: (bi * nq + qt[p], k)),
        scratch_shapes=[pltpu.VMEM((NSA_GROUP * tq, LANES), jnp.float32),
                        pltpu.VMEM((NSA_GROUP * tq, LANES), jnp.float32),
                        pltpu.VMEM((NSA_GROUP * tq, HEAD_DIM), jnp.float32)] * 2)
    return pl.pallas_call(
        functools.partial(_nsa_main_kernel, tq=tq, tk=tk),
        out_shape=jax.ShapeDtypeStruct((b * s, lay['nsa_h'] * HEAD_DIM), MXU_DTYPE),
        grid_spec=grid_spec, compiler_params=_params("parallel", "parallel", "arbitrary"),
    )(jnp.asarray(qi_tab), jnp.asarray(ki_tab), hp, hp, hp, hp, hp, sel, expand, o_cmp, gate_logits)


def _layer_norm(y, g, b):
    mu = jnp.mean(y, axis=-1, keepdims=True)
    var = jnp.mean(jnp.square(y - mu), axis=-1, keepdims=True)
    return (y - mu) * lax.rsqrt(var + LN_EPS) * g + b


def _out_proj_kernel(mix_ref, w_ref, x_ref, g_ref, b_ref, wr_ref, br_ref, x1_ref, idx_ref, gate_ref,
                     acc_ref, *, alpha, n_experts):
    k = pl.program_id(1)

    @pl.when(k == 0)
    def _():
        acc_ref[...] = jnp.zeros(acc_ref.shape, jnp.float32)

    acc_ref[...] += _dot(mix_ref[...], w_ref[...])

    @pl.when(k == pl.num_programs(1) - 1)
    def _():
        x1 = _layer_norm(alpha * x_ref[...] + acc_ref[...], g_ref[...], b_ref[...])
        x1_ref[...] = x1
        logits = _dot(x1.astype(MXU_DTYPE), wr_ref[...]) + br_ref[...]
        lane = _lane_index(logits.shape)
        work = jnp.where(lane < n_experts, logits, -jnp.inf)
        idx_out = jnp.zeros(logits.shape, jnp.float32)
        val_out = jnp.zeros(logits.shape, jnp.float32)
        top = None
        for r in range(TOP_K):
            mx = jnp.max(work, axis=-1, keepdims=True)
            first = _first_lane(work == mx, lane)
            top = mx if top is None else top
            idx_out = jnp.where(lane == r, first, idx_out)
            val_out = jnp.where(lane == r, jnp.exp(mx - top), val_out)
            work = jnp.where(lane == first, -jnp.inf, work)
        idx_ref[...] = idx_out.astype(jnp.int32)
        gate_ref[...] = val_out / jnp.sum(val_out, axis=-1, keepdims=True)


def _out_proj_ln_router(mix, w_out, x, ln_g, ln_b, w_router, b_router, alpha):
    t, kdim = mix.shape
    d = w_out.shape[1]
    n_experts = w_router.shape[1]
    tm = min(256, t)
    tk = min(512, kdim)
    wr = jnp.pad(w_router, ((0, 0), (0, LANES - n_experts))).astype(MXU_DTYPE)
    br = jnp.pad(b_router.astype(jnp.float32), (0, LANES - n_experts)).reshape(1, LANES)
    row = lambda i, k: (i, 0)
    const = lambda i, k: (0, 0)
    return pl.pallas_call(
        functools.partial(_out_proj_kernel, alpha=alpha, n_experts=n_experts),
        out_shape=(jax.ShapeDtypeStruct((t, d), jnp.float32),
                   jax.ShapeDtypeStruct((t, LANES), jnp.int32),
                   jax.ShapeDtypeStruct((t, LANES), jnp.float32)),
        grid=(t // tm, kdim // tk),
        in_specs=[pl.BlockSpec((tm, tk), lambda i, k: (i, k)),
                  pl.BlockSpec((tk, d), lambda i, k: (k, 0)),
                  pl.BlockSpec((tm, d), row),
                  pl.BlockSpec((1, d), const), pl.BlockSpec((1, d), const),
                  pl.BlockSpec((d, LANES), const), pl.BlockSpec((1, LANES), const)],
        out_specs=(pl.BlockSpec((tm, d), row), pl.BlockSpec((tm, LANES), row),
                   pl.BlockSpec((tm, LANES), row)),
        scratch_shapes=[pltpu.VMEM((tm, d), jnp.float32)],
        compiler_params=_params("parallel", "arbitrary"),
    )(mix, w_out, x, ln_g.reshape(1, d), ln_b.reshape(1, d), wr, br)


MOE_TILE = 256
COMBINE_TILE = 128
GATHER_UNROLL = 8


GLU_GROUP = 2 * LANES


def _regroup_kernel(w_ref, p_ref, o_ref):
    w = w_ref[0].astype(MXU_DTYPE)
    for c in range(w.shape[1] // GLU_GROUP):
        cols = slice(c * GLU_GROUP, (c + 1) * GLU_GROUP)
        o_ref[0, :, cols] = _dot(w[:, cols], p_ref[...]).astype(o_ref.dtype)


def _regroup_gate_up(w_gate_up):
    e, d, f2 = w_gate_up.shape
    j = np.arange(GLU_GROUP)
    src = np.where(j < LANES, 2 * j, 2 * (j - LANES) + 1)
    perm = np.zeros((GLU_GROUP, GLU_GROUP), np.float32)
    perm[src, j] = 1.0
    tk = min(1024, d)
    return pl.pallas_call(
        _regroup_kernel,
        out_shape=jax.ShapeDtypeStruct((e, d, f2), MXU_DTYPE),
        grid=(e, d // tk),
        in_specs=[pl.BlockSpec((1, tk, f2), lambda ei, ki: (ei, ki, 0)),
                  pl.BlockSpec((GLU_GROUP, GLU_GROUP), lambda ei, ki: (0, 0))],
        out_specs=pl.BlockSpec((1, tk, f2), lambda ei, ki: (ei, ki, 0)),
        compiler_params=_params("parallel", "parallel"),
    )(w_gate_up, jnp.asarray(perm, MXU_DTYPE))


def _regroup_bias(b_gate_up):
    e, f2 = b_gate_up.shape
    b = b_gate_up.astype(jnp.float32).reshape(e, f2 // GLU_GROUP, LANES, 2)
    return b.transpose(0, 1, 3, 2).reshape(e, 1, f2)


def _moe_kernel(rt_ref, te_ref, nu_ref, x_hbm, wgu_ref, bgu_ref, wd_ref, bd_ref, o_ref, xbuf, sem):
    i = pl.program_id(0)
    n_used = nu_ref[0]

    def row_copy(tile, slot, r):
        tok = rt_ref[tile * MOE_TILE + r]
        return pltpu.make_async_copy(x_hbm.at[pl.ds(tok, 1)], xbuf.at[slot, pl.ds(r, 1)], sem.at[slot])

    def gather(tile, slot):
        def body(r, carry):
            row_copy(tile, slot, r).start()
            return carry
        lax.fori_loop(0, MOE_TILE, body, 0, unroll=GATHER_UNROLL)

    @pl.when(i == 0)
    def _():
        gather(0, 0)

    @pl.when(i + 1 < n_used)
    def _():
        gather(i + 1, (i + 1) % 2)

    @pl.when(i < n_used)
    def _():
        slot = i % 2
        pltpu.make_async_copy(x_hbm.at[pl.ds(0, MOE_TILE)], xbuf.at[slot], sem.at[slot]).wait()
        x = xbuf[slot].astype(MXU_DTYPE)
        hid = _dot(x, wgu_ref[0]) + bgu_ref[0]
        acts = []
        for c in range(hid.shape[1] // GLU_GROUP):
            h_glu = jnp.minimum(hid[:, c * GLU_GROUP:c * GLU_GROUP + LANES], SWIGLU_LIMIT)
            h_lin = jnp.clip(hid[:, c * GLU_GROUP + LANES:(c + 1) * GLU_GROUP], -SWIGLU_LIMIT, SWIGLU_LIMIT)
            acts.append(h_glu * (1.0 / (1.0 + jnp.exp(-SWIGLU_ALPHA * h_glu))) * (h_lin + 1.0))
        act = jnp.concatenate(acts, axis=1)
        o_ref[...] = _dot(act.astype(MXU_DTYPE), wd_ref[0]) + bd_ref[0]

    @pl.when(i >= n_used)
    def _():
        o_ref[...] = jnp.zeros(o_ref.shape, o_ref.dtype)


def _moe_experts(x1, row_token, tile_expert, n_used, wgu, bgu, wd, bd):
    t, d = x1.shape
    f2 = wgu.shape[2]
    n_tiles = tile_expert.shape[0]
    wmap = lambda i, rt, te, nu: (te[i], 0, 0)
    grid_spec = pltpu.PrefetchScalarGridSpec(
        num_scalar_prefetch=3,
        grid=(n_tiles,),
        in_specs=[pl.BlockSpec(memory_space=pl.ANY),
                  pl.BlockSpec((1, d, f2), wmap), pl.BlockSpec((1, 1, f2), wmap),
                  pl.BlockSpec((1, f2 // 2, d), wmap), pl.BlockSpec((1, 1, d), wmap)],
        out_specs=pl.BlockSpec((MOE_TILE, d), lambda i, rt, te, nu: (i, 0)),
        scratch_shapes=[pltpu.VMEM((2, MOE_TILE, d), jnp.float32), pltpu.SemaphoreType.DMA((2,))])
    return pl.pallas_call(
        _moe_kernel,
        out_shape=jax.ShapeDtypeStruct((n_tiles * MOE_TILE, d), jnp.float32),
        grid_spec=grid_spec, compiler_params=_params("arbitrary"),
    )(row_token, tile_expert, n_used, x1, wgu, bgu, wd, bd)


def _combine_kernel(pos_ref, y_hbm, x_ref, gate_ref, g_ref, b_ref, o_ref, ob_ref, ybuf, sem, *, alpha):
    i = pl.program_id(0)
    n_rows = COMBINE_TILE * TOP_K

    def gather(tile, slot):
        def body(r, carry):
            src = pos_ref[tile * n_rows + r]
            pltpu.make_async_copy(y_hbm.at[pl.ds(src, 1)], ybuf.at[slot, pl.ds(r, 1)], sem.at[slot]).start()
            return carry
        lax.fori_loop(0, n_rows, body, 0, unroll=GATHER_UNROLL)

    @pl.when(i == 0)
    def _():
        gather(0, 0)

    @pl.when(i + 1 < pl.num_programs(0))
    def _():
        gather(i + 1, (i + 1) % 2)

    slot = i % 2
    pltpu.make_async_copy(y_hbm.at[pl.ds(0, n_rows)], ybuf.at[slot], sem.at[slot]).wait()
    gate = gate_ref[...]
    ffn = jnp.zeros(x_ref.shape, jnp.float32)
    for k in range(TOP_K):
        ffn = ffn + gate[:, k:k + 1] * ybuf[slot, pl.ds(k * COMBINE_TILE, COMBINE_TILE)]
    x2 = _layer_norm(alpha * x_ref[...] + ffn, g_ref[...], b_ref[...])
    o_ref[...] = x2
    ob_ref[...] = x2.astype(ob_ref.dtype)


def _combine_ln(y, pos, x1, gate, ln_g, ln_b, alpha):
    t, d = x1.shape
    row = lambda i, p: (i, 0)
    const = lambda i, p: (0, 0)
    grid_spec = pltpu.PrefetchScalarGridSpec(
        num_scalar_prefetch=1,
        grid=(t // COMBINE_TILE,),
        in_specs=[pl.BlockSpec(memory_space=pl.ANY),
                  pl.BlockSpec((COMBINE_TILE, d), row), pl.BlockSpec((COMBINE_TILE, LANES), row),
                  pl.BlockSpec((1, d), const), pl.BlockSpec((1, d), const)],
        out_specs=(pl.BlockSpec((COMBINE_TILE, d), row), pl.BlockSpec((COMBINE_TILE, d), row)),
        scratch_shapes=[pltpu.VMEM((2, COMBINE_TILE * TOP_K, d), jnp.float32),
                        pltpu.SemaphoreType.DMA((2,))])
    return pl.pallas_call(
        functools.partial(_combine_kernel, alpha=alpha),
        out_shape=(jax.ShapeDtypeStruct((t, d), jnp.float32), jax.ShapeDtypeStruct((t, d), MXU_DTYPE)),
        grid_spec=grid_spec, compiler_params=_params("arbitrary"),
    )(pos, y, x1, gate, ln_g.reshape(1, d), ln_b.reshape(1, d))


def _route(top_idx, n_experts):
    t = top_idx.shape[0]
    n_assign = t * TOP_K
    n_tiles = n_assign // MOE_TILE + n_experts
    expert = top_idx.reshape(n_assign)
    onehot = (expert[:, None] == jnp.arange(n_experts, dtype=jnp.int32)[None, :]).astype(jnp.int32)
    running = jnp.cumsum(onehot, axis=0)
    counts = running[-1]
    rank = jnp.take_along_axis(running, expert[:, None], axis=1)[:, 0] - 1
    padded = (counts + MOE_TILE - 1) // MOE_TILE * MOE_TILE
    pad_end = jnp.cumsum(padded)
    dest = (pad_end - padded)[expert] + rank
    slot = dest.reshape(t // COMBINE_TILE, COMBINE_TILE, TOP_K).transpose(0, 2, 1).reshape(n_assign)
    token = jnp.arange(n_assign, dtype=jnp.int32) // TOP_K
    row_token = jnp.zeros((n_tiles * MOE_TILE,), jnp.int32).at[dest].set(token)
    tile_start = jnp.arange(n_tiles, dtype=jnp.int32) * MOE_TILE
    tile_expert = jnp.minimum(
        jnp.sum((pad_end[None, :] <= tile_start[:, None]).astype(jnp.int32), axis=1), n_experts - 1)
    n_used = (pad_end[-1:] // MOE_TILE).astype(jnp.int32)
    return row_token, tile_expert, n_used, slot.astype(jnp.int32)


def _rope_tables(positions):
    inv_freq = 1.0 / (ROPE_THETA ** (jnp.arange(0, ROT_DIM, 2, dtype=jnp.float32) / ROT_DIM))
    ang = positions.astype(jnp.float32).reshape(-1)[:, None] * inv_freq
    cos, sin = jnp.cos(ang), jnp.sin(ang)
    rest = HEAD_DIM - ROT_DIM
    n = ang.shape[0]
    ct = jnp.concatenate([cos, cos, jnp.ones((n, rest), jnp.float32)], axis=1)
    sa = jnp.concatenate([jnp.zeros_like(sin), sin, jnp.zeros((n, rest), jnp.float32)], axis=1)
    sb = jnp.concatenate([-sin, jnp.zeros_like(sin), jnp.zeros((n, rest), jnp.float32)], axis=1)
    return ct, sa, sb


def _gate_weight(w_in_l, lay):
    d = w_in_l.shape[0]
    kv = lay['kv']
    o = lay['ref_off']['ngate']
    wg = w_in_l[:, o:o + lay['size']['ngate']].reshape(d, kv, NSA_GROUP * 3)
    wg = jnp.pad(wg, ((0, 0), (0, 0), (0, LANES - NSA_GROUP * 3)))
    return wg.reshape(d, kv * LANES).astype(MXU_DTYPE)


def kernel(x, positions, w_in, nsa_cmp_pos, nsa_cmp_w1, nsa_cmp_w2, diff_lambda, diff_subln_g, w_out,
           ln1_g, ln1_b, w_router, b_router, w_gate_up, b_gate_up, w_down, b_down, ln2_g, ln2_b):
    b, s, d = x.shape
    depth = w_in.shape[0]
    n_experts = w_router.shape[2]
    lay = _layout(d)
    alpha = (2 * depth) ** 0.25
    ct, sa, sb = _rope_tables(positions)
    xf = x.reshape(b * s, d).astype(jnp.float32)
    xb = xf.astype(MXU_DTYPE)
    for layer in range(depth):
        w_l = w_in[layer]
        w_perm = jnp.concatenate(
            [w_l[:, lay['ref_off'][n]:lay['ref_off'][n] + lay['size'][n]] for n in ROPED + PLAIN],
            axis=1).astype(MXU_DTYPE)
        hp = _project(xb, w_perm, ct, sa, sb, lay['n_roped'])
        gate_logits = _gate_logits(xb, _gate_weight(w_l, lay))

        kcv = _nsa_compress(hp, nsa_cmp_pos[layer], nsa_cmp_w1[layer], nsa_cmp_w2[layer], lay, b, s)
        o_cmp, sel = _nsa_cmp_select(hp, kcv, lay, b, s)
        y_nsa = _nsa_main(hp, sel, o_cmp, gate_logits, lay, b, s)
        lambda_init = 0.8 - 0.6 * math.exp(-0.3 * layer)
        y_diff = _diff_attention(hp, diff_lambda[layer], diff_subln_g[layer], lay, b, s, lambda_init)
        y_moba = _moba_attention(hp, lay, b, s)
        mix = jnp.concatenate([y_nsa, y_diff, y_moba], axis=1)

        x1, top_idx, top_gate = _out_proj_ln_router(
            mix, w_out[layer].astype(MXU_DTYPE), xf, ln1_g[layer], ln1_b[layer],
            w_router[layer], b_router[layer], alpha)

        row_token, tile_expert, n_used, slot = _route(top_idx[:, :TOP_K], n_experts)
        y = _moe_experts(
            x1, row_token, tile_expert, n_used,
            _regroup_gate_up(w_gate_up[layer]), _regroup_bias(b_gate_up[layer]),
            w_down[layer].astype(MXU_DTYPE), b_down[layer].astype(jnp.float32).reshape(n_experts, 1, d))
        xf, xb = _combine_ln(y, slot, x1, top_gate, ln2_g[layer], ln2_b[layer], alpha)
    return xf.reshape(b, s, d).astype(x.dtype)
```

```python
import functools
import math

import numpy as np
import jax
import jax.numpy as jnp
from jax import lax
from jax.experimental import pallas as pl
from jax.experimental.pallas import tpu as pltpu

HEAD_DIM = 128
ROT_DIM = HEAD_DIM // 4
ROT_HALF = ROT_DIM // 2
ROPE_THETA = 500000.0
ATTN_SCALE = HEAD_DIM ** -0.5
NEG_INF = -1e30
LN_EPS = 1e-5

NSA_GROUP = 4
CMP_LEN = 32
CMP_STRIDE = 16
SEL_LEN = 64
SEL_TOPN = 16
WINDOW = 512
CMP_PER_SEL = SEL_LEN // CMP_STRIDE

DIFF_VDIM = 2 * HEAD_DIM
MOBA_BLOCK = 256
MOBA_TOPK = 3

TOP_K = 4
SWIGLU_LIMIT = 7.0
SWIGLU_ALPHA = 1.702

LANES = 128
MXU_DTYPE = jnp.bfloat16
VMEM_LIMIT = 56 * 1024 * 1024

ROPED = ('nq', 'nkc', 'nks', 'nkw', 'dq', 'dk', 'mq', 'mk')
PLAIN = ('nvc', 'nvs', 'nvw', 'dv', 'mv')
REF_ORDER = ('nq', 'nkc', 'nvc', 'nks', 'nvs', 'nkw', 'nvw', 'ngate', 'dq', 'dk', 'dv', 'mq', 'mk', 'mv')


def _layout(d):
    nsa_h = d // 256
    kv = nsa_h // NSA_GROUP
    diff_h = d // 1024
    moba_h = d // 512
    size = dict(nq=nsa_h * HEAD_DIM, nkc=kv * HEAD_DIM, nvc=kv * HEAD_DIM, nks=kv * HEAD_DIM,
                nvs=kv * HEAD_DIM, nkw=kv * HEAD_DIM, nvw=kv * HEAD_DIM, ngate=3 * nsa_h,
                dq=2 * diff_h * HEAD_DIM, dk=2 * diff_h * HEAD_DIM, dv=diff_h * DIFF_VDIM,
                mq=moba_h * HEAD_DIM, mk=moba_h * HEAD_DIM, mv=moba_h * HEAD_DIM)
    ref_off, o = {}, 0
    for n in REF_ORDER:
        ref_off[n] = o
        o += size[n]
    off, o = {}, 0
    for n in ROPED + PLAIN:
        off[n] = o
        o += size[n]
    n_roped = sum(size[n] for n in ROPED)
    return dict(nsa_h=nsa_h, kv=kv, diff_h=diff_h, moba_h=moba_h, size=size, ref_off=ref_off,
                off=off, n_roped=n_roped, n_cols=o)


def _params(*sem):
    return pltpu.CompilerParams(dimension_semantics=sem, vmem_limit_bytes=VMEM_LIMIT)


def _dot(a, b):
    return jnp.dot(a, b, preferred_element_type=jnp.float32)


def _dot_nt(a, b):
    return lax.dot_general(a, b, (((1,), (1,)), ((), ())), preferred_element_type=jnp.float32)


def _iota(shape, dim):
    return lax.broadcasted_iota(jnp.int32, shape, dim)


def _lane_index(shape):
    return _iota(shape, 1).astype(jnp.float32)


def _first_lane(hit, lane):
    return jnp.min(jnp.where(hit, lane, float(LANES)), axis=-1, keepdims=True)


def _proj_kernel(x_ref, w_ref, ct_ref, sa_ref, sb_ref, o_ref, *, n_rope_blocks, tn):
    j = pl.program_id(1)
    acc = _dot(x_ref[...], w_ref[...])

    @pl.when(j < n_rope_blocks)
    def _():
        ct, sa, sb = ct_ref[...], sa_ref[...], sb_ref[...]
        for c in range(tn // HEAD_DIM):
            a = acc[:, c * HEAD_DIM:(c + 1) * HEAD_DIM]
            r = (a * ct + pltpu.roll(a, ROT_HALF, 1) * sa
                 + pltpu.roll(a, HEAD_DIM - ROT_HALF, 1) * sb)
            o_ref[:, c * HEAD_DIM:(c + 1) * HEAD_DIM] = r.astype(o_ref.dtype)

    @pl.when(j >= n_rope_blocks)
    def _():
        o_ref[...] = acc.astype(o_ref.dtype)


def _project(xb, w, ct, sa, sb, n_roped):
    m, k = xb.shape
    n = w.shape[1]
    tm = min(1024, m)
    tn = next(t for t in (512, 256, 128) if n % t == 0 and n_roped % t == 0)
    kern = functools.partial(_proj_kernel, n_rope_blocks=n_roped // tn, tn=tn)
    return pl.pallas_call(
        kern,
        out_shape=jax.ShapeDtypeStruct((m, n), MXU_DTYPE),
        grid=(m // tm, n // tn),
        in_specs=[pl.BlockSpec((tm, k), lambda i, j: (i, 0)),
                  pl.BlockSpec((k, tn), lambda i, j: (0, j)),
                  pl.BlockSpec((tm, HEAD_DIM), lambda i, j: (i, 0)),
                  pl.BlockSpec((tm, HEAD_DIM), lambda i, j: (i, 0)),
                  pl.BlockSpec((tm, HEAD_DIM), lambda i, j: (i, 0))],
        out_specs=pl.BlockSpec((tm, tn), lambda i, j: (i, j)),
        compiler_params=_params("parallel", "arbitrary"),
    )(xb, w, ct, sa, sb)


def _gate_kernel(x_ref, w_ref, o_ref):
    o_ref[...] = _dot(x_ref[...], w_ref[...])


def _gate_logits(xb, wg):
    m, k = xb.shape
    n = wg.shape[1]
    tm = min(1024, m)
    return pl.pallas_call(
        _gate_kernel,
        out_shape=jax.ShapeDtypeStruct((m, n), jnp.float32),
        grid=(m // tm,),
        in_specs=[pl.BlockSpec((tm, k), lambda i: (i, 0)),
                  pl.BlockSpec((k, n), lambda i: (0, 0))],
        out_specs=pl.BlockSpec((tm, n), lambda i: (i, 0)),
        compiler_params=_params("parallel"),
    )(xb, wg)


EXP2_SCALE = ATTN_SCALE * math.log2(math.e)


def _mask_bias(mask):
    return jnp.where(mask, 0.0, -jnp.inf)


def _lanes(x, n):
    return x if n == LANES else jnp.concatenate([x] * (n // LANES), axis=1)


def _flash_step(s, v, m_ref, acc_ref):
    dv = v.shape[1]
    m_old = m_ref[...]
    m_new = jnp.maximum(m_old, jnp.max(s, axis=-1, keepdims=True))
    p = jnp.exp2((s - _lanes(m_new, s.shape[1])) * EXP2_SCALE)
    alpha = jnp.exp2((m_old - m_new) * EXP2_SCALE)
    acc_ref[:, :dv] = _lanes(alpha, dv) * acc_ref[:, :dv] + _dot(p.astype(v.dtype), v)
    acc_ref[:, dv:] = alpha * acc_ref[:, dv:] + jnp.sum(p, axis=-1, keepdims=True)
    m_ref[...] = m_new


def _flash_result(acc_ref, rows):
    dv = acc_ref.shape[1] - LANES
    denom = jnp.maximum(acc_ref[rows, dv:], 1e-30)
    return acc_ref[rows, :dv] / _lanes(denom, dv)


def _init_state(m_ref, acc_ref):
    m_ref[...] = jnp.full(m_ref.shape, NEG_INF, jnp.float32)
    acc_ref[...] = jnp.zeros(acc_ref.shape, jnp.float32)


def _causal_pairs(nq, tq, tk):
    qi, ki = [], []
    for q in range(nq):
        last = (q * tq + tq - 1) // tk
        for k in range(last + 1):
            qi.append(q)
            ki.append(k)
    return np.asarray(qi, np.int32), np.asarray(ki, np.int32)


def _diff_kernel(qi_tab, ki_tab, lam_ref, g_ref, q_ref, k_ref, v_ref, o_ref, m_ref, acc_ref,
                 *, tq, tk, lambda_init):
    p_id = pl.program_id(2)
    qi = qi_tab[p_id]
    ki = ki_tab[p_id]
    last = (qi * tq + tq - 1) // tk

    @pl.when(ki == 0)
    def _():
        _init_state(m_ref, acc_ref)

    def step(masked):
        maps = []
        for mp in range(2):
            q = q_ref[:, mp * HEAD_DIM:(mp + 1) * HEAD_DIM]
            k = k_ref[:, mp * HEAD_DIM:(mp + 1) * HEAD_DIM]
            maps.append(_dot_nt(q, k))
        if masked:
            t = qi * tq + _iota((tq, tk), 0)
            u = ki * tk + _iota((tq, tk), 1)
            bias = _mask_bias(u <= t)
            maps = [s + bias for s in maps]
        _flash_step(jnp.concatenate(maps, axis=0), v_ref[...], m_ref, acc_ref)

    below = (ki + 1) * tk - 1 <= qi * tq

    @pl.when(below)
    def _():
        step(False)

    @pl.when(jnp.logical_not(below))
    def _():
        step(True)

    @pl.when(ki == last)
    def _():
        lv = lam_ref[...]
        lam = (jnp.exp(jnp.sum(lv[0:1] * lv[1:2], axis=-1, keepdims=True))
               - jnp.exp(jnp.sum(lv[2:3] * lv[3:4], axis=-1, keepdims=True)) + lambda_init)
        o = _flash_result(acc_ref, pl.ds(0, tq)) - lam * _flash_result(acc_ref, pl.ds(tq, tq))
        o = o * lax.rsqrt(jnp.mean(jnp.square(o), axis=-1, keepdims=True) + LN_EPS) * g_ref[...]
        o_ref[...] = (o * (1.0 - lambda_init)).astype(o_ref.dtype)


def _diff_attention(hp, lam_vecs, subln_g, lay, b, s, lambda_init):
    hd = lay['diff_h']
    tq = tk = min(512, s)
    nq = s // tq
    nk = s // tk
    qi_tab, ki_tab = _causal_pairs(nq, tq, tk)
    qb, kb, vb = (lay['off'][n] // DIFF_VDIM for n in ('dq', 'dk', 'dv'))
    assert all(lay['off'][n] % DIFF_VDIM == 0 for n in ('dq', 'dk', 'dv'))
    kern = functools.partial(_diff_kernel, tq=tq, tk=tk, lambda_init=lambda_init)
    grid_spec = pltpu.PrefetchScalarGridSpec(
        num_scalar_prefetch=2,
        grid=(b, hd, len(qi_tab)),
        in_specs=[pl.BlockSpec((4, HEAD_DIM), lambda bi, h, p, qt, kt: (0, 0)),
                  pl.BlockSpec((1, DIFF_VDIM), lambda bi, h, p, qt, kt: (0, 0)),
                  pl.BlockSpec((tq, DIFF_VDIM), lambda bi, h, p, qt, kt: (bi * nq + qt[p], qb + h)),
                  pl.BlockSpec((tk, DIFF_VDIM), lambda bi, h, p, qt, kt: (bi * nk + kt[p], kb + h)),
                  pl.BlockSpec((tk, DIFF_VDIM), lambda bi, h, p, qt, kt: (bi * nk + kt[p], vb + h))],
        out_specs=pl.BlockSpec((tq, DIFF_VDIM), lambda bi, h, p, qt, kt: (bi * nq + qt[p], h)),
        scratch_shapes=[pltpu.VMEM((2 * tq, LANES), jnp.float32),
                        pltpu.VMEM((2 * tq, DIFF_VDIM + LANES), jnp.float32)])
    return pl.pallas_call(
        kern, out_shape=jax.ShapeDtypeStruct((b * s, hd * DIFF_VDIM), MXU_DTYPE),
        grid_spec=grid_spec, compiler_params=_params("parallel", "parallel", "arbitrary"),
    )(jnp.asarray(qi_tab), jnp.asarray(ki_tab), lam_vecs.astype(jnp.float32),
      subln_g.reshape(1, DIFF_VDIM).astype(jnp.float32), hp, hp, hp)


def _kmean_kernel(k_ref, o_ref, *, n_blk):
    k = k_ref[...].astype(jnp.float32).reshape(n_blk, MOBA_BLOCK, HEAD_DIM)
    o_ref[0, 0] = jnp.mean(k, axis=1)


def _moba_kmean(hp, lay, b, s):
    h = lay['moba_h']
    n_blk = s // MOBA_BLOCK
    kb = lay['off']['mk'] // HEAD_DIM
    return pl.pallas_call(
        functools.partial(_kmean_kernel, n_blk=n_blk),
        out_shape=jax.ShapeDtypeStruct((b, h, n_blk, HEAD_DIM), jnp.float32),
        grid=(b, h),
        in_specs=[pl.BlockSpec((s, HEAD_DIM), lambda bi, hi: (bi, kb + hi))],
        out_specs=pl.BlockSpec((1, 1, n_blk, HEAD_DIM), lambda bi, hi: (bi, hi, 0, 0)),
        compiler_params=_params("parallel", "parallel"),
    )(hp)


def _block_expansion(s, block):
    return (np.arange(s)[None, :] // block == np.arange(LANES)[:, None]).astype(np.float32)


def _lane_column(x, n):
    return jnp.sum(jnp.where(_iota(x.shape, 1) == n, x, 0.0), axis=-1, keepdims=True)


def _moba_kernel(qi_tab, ki_tab, q_ref, k_ref, v_ref, km_ref, o_ref, bias_ref, m_ref, acc_ref,
                 *, tq, tk):
    p_id = pl.program_id(2)
    qi = qi_tab[p_id]
    ki = ki_tab[p_id]
    last = (qi * tq + tq - 1) // tk
    q = q_ref[...]

    @pl.when(ki == 0)
    def _():
        _init_state(m_ref, acc_ref)
        score = _dot_nt(q, km_ref[0, 0].astype(q.dtype))
        blk = _lane_index((tq, LANES))
        own = ((qi * tq + _iota((tq, LANES), 0)) // MOBA_BLOCK).astype(jnp.float32)
        work = jnp.where(blk < own, score, -jnp.inf)
        sel = jnp.where(blk == own, 1.0, 0.0)
        for r in range(MOBA_TOPK):
            mx = jnp.max(work, axis=-1, keepdims=True)
            first = _first_lane(work == mx, blk)
            pick = blk == first
            sel = jnp.where(pick & (own > r), 1.0, sel)
            work = jnp.where(pick, -jnp.inf, work)
        bias_ref[...] = _mask_bias(sel > 0.5)

    def step(causal):
        cols = []
        for c in range(tk // MOBA_BLOCK):
            col = _lane_column(bias_ref[...], ki * (tk // MOBA_BLOCK) + c)
            cols.append(jnp.broadcast_to(col, (tq, MOBA_BLOCK)))
        bias = cols[0] if len(cols) == 1 else jnp.concatenate(cols, axis=1)
        if causal:
            t = qi * tq + _iota((tq, tk), 0)
            u = ki * tk + _iota((tq, tk), 1)
            bias = jnp.where(u <= t, bias, -jnp.inf)
        _flash_step(_dot_nt(q, k_ref[...]) + bias, v_ref[...], m_ref, acc_ref)

    diagonal = (ki + 1) * tk > qi * tq

    @pl.when(diagonal)
    def _():
        step(True)

    @pl.when(jnp.logical_not(diagonal))
    def _():
        step(False)

    @pl.when(ki == last)
    def _():
        o_ref[...] = _flash_result(acc_ref, pl.ds(0, tq)).astype(o_ref.dtype)


def _moba_attention(hp, lay, b, s):
    h = lay['moba_h']
    n_blk = s // MOBA_BLOCK
    assert s % MOBA_BLOCK == 0 and MOBA_TOPK <= n_blk <= LANES
    km = _moba_kmean(hp, lay, b, s)
    km = jnp.pad(km, ((0, 0), (0, 0), (0, LANES - n_blk), (0, 0)))
    tq = min(1024, s)
    tk = min(512, s)
    nq, nk = s // tq, s // tk
    qi_tab, ki_tab = _causal_pairs(nq, tq, tk)
    qb, kb, vb = (lay['off'][n] // HEAD_DIM for n in ('mq', 'mk', 'mv'))
    grid_spec = pltpu.PrefetchScalarGridSpec(
        num_scalar_prefetch=2,
        grid=(b, h, len(qi_tab)),
        in_specs=[pl.BlockSpec((tq, HEAD_DIM), lambda bi, hi, p, qt, kt: (bi * nq + qt[p], qb + hi)),
                  pl.BlockSpec((tk, HEAD_DIM), lambda bi, hi, p, qt, kt: (bi * nk + kt[p], kb + hi)),
                  pl.BlockSpec((tk, HEAD_DIM), lambda bi, hi, p, qt, kt: (bi * nk + kt[p], vb + hi)),
                  pl.BlockSpec((1, 1, LANES, HEAD_DIM), lambda bi, hi, p, qt, kt: (bi, hi, 0, 0))],
        out_specs=pl.BlockSpec((tq, HEAD_DIM), lambda bi, hi, p, qt, kt: (bi * nq + qt[p], hi)),
        scratch_shapes=[pltpu.VMEM((tq, LANES), jnp.float32), pltpu.VMEM((tq, LANES), jnp.float32),
                        pltpu.VMEM((tq, HEAD_DIM + LANES), jnp.float32)])
    return pl.pallas_call(
        functools.partial(_moba_kernel, tq=tq, tk=tk),
        out_shape=jax.ShapeDtypeStruct((b * s, h * HEAD_DIM), MXU_DTYPE),
        grid_spec=grid_spec, compiler_params=_params("parallel", "parallel", "arbitrary"),
    )(jnp.asarray(qi_tab), jnp.asarray(ki_tab), hp, hp, hp, km)


def _gelu_tanh(x):
    return 0.5 * x * (1.0 + jnp.tanh(math.sqrt(2.0 / math.pi) * (x + 0.044715 * (x * x * x))))


def _compress_kernel(seg_ref, plo_ref, phi_ref, w1lo_ref, w1hi_ref, w2_ref, o_ref, *, n_seg):
    seg = seg_ref[0, 0, 0].astype(jnp.float32)
    lo = _dot((seg + plo_ref[0]).astype(MXU_DTYPE), w1lo_ref[0])
    hi = _dot((seg + phi_ref[0]).astype(MXU_DTYPE), w1hi_ref[0])
    pre = lo + pltpu.roll(hi, n_seg - 1, 0)
    o_ref[0, 0, 0] = _dot(_gelu_tanh(pre).astype(MXU_DTYPE), w2_ref[0]).astype(o_ref.dtype)


def _nsa_compress(hp, cmp_pos, cmp_w1, cmp_w2, lay, b, s):
    kv = lay['kv']
    n_seg = s // CMP_STRIDE
    half = CMP_STRIDE * HEAD_DIM

    def segments(name):
        o = lay['off'][name]
        t = hp[:, o:o + kv * HEAD_DIM].reshape(b, n_seg, CMP_STRIDE, kv, HEAD_DIM)
        return t.transpose(0, 3, 1, 2, 4).reshape(b, kv, n_seg, half)

    seg = jnp.stack([segments('nkc'), segments('nvc')], axis=1)
    pos = cmp_pos.astype(jnp.float32).reshape(2, CMP_LEN * HEAD_DIM)
    plo = pos[:, :half].reshape(2, 1, half)
    phi = pos[:, half:].reshape(2, 1, half)
    w1 = cmp_w1.astype(MXU_DTYPE)
    w1lo, w1hi = w1[:, :half], w1[:, half:]
    w2 = cmp_w2.astype(MXU_DTYPE)
    return pl.pallas_call(
        functools.partial(_compress_kernel, n_seg=n_seg),
        out_shape=jax.ShapeDtypeStruct((b, 2, kv, n_seg, HEAD_DIM), MXU_DTYPE),
        grid=(b, 2, kv),
        in_specs=[pl.BlockSpec((1, 1, 1, n_seg, half), lambda bi, c, k: (bi, c, k, 0, 0)),
                  pl.BlockSpec((1, 1, half), lambda bi, c, k: (c, 0, 0)),
                  pl.BlockSpec((1, 1, half), lambda bi, c, k: (c, 0, 0)),
                  pl.BlockSpec((1, half, HEAD_DIM), lambda bi, c, k: (c, 0, 0)),
                  pl.BlockSpec((1, half, HEAD_DIM), lambda bi, c, k: (c, 0, 0)),
                  pl.BlockSpec((1, HEAD_DIM, HEAD_DIM), lambda bi, c, k: (c, 0, 0))],
        out_specs=pl.BlockSpec((1, 1, 1, n_seg, HEAD_DIM), lambda bi, c, k: (bi, c, k, 0, 0)),
        compiler_params=_params("parallel", "parallel", "parallel"),
    )(seg, plo, phi, w1lo, w1hi, w2)


def _nsa_cmp_kernel(q_ref, kc_ref, vc_ref, o_ref, sel_ref, *, tq, n_sel, n_cmp):
    qi = pl.program_id(2)
    width = CMP_PER_SEL * LANES
    t = qi * tq + _iota((tq, width), 0)
    pos = _iota((tq, width), 1)
    m_idx = pos % LANES
    n_idx = CMP_PER_SEL * m_idx + pos // LANES
    valid = (m_idx < n_sel) & (n_idx < n_cmp) & (n_idx * CMP_STRIDE + CMP_LEN - 1 <= t)
    kc = kc_ref[0, 0, 0]
    vc = vc_ref[0, 0, 0]
    p_grp = jnp.zeros((tq, width), jnp.float32)
    for g in range(NSA_GROUP):
        q = q_ref[:, g * HEAD_DIM:(g + 1) * HEAD_DIM]
        s = jnp.where(valid, _dot_nt(q, kc) * ATTN_SCALE, NEG_INF)
        p = jnp.where(valid, jnp.exp(s - jnp.max(s, axis=-1, keepdims=True)), 0.0)
        p = p / jnp.maximum(jnp.sum(p, axis=-1, keepdims=True), 1e-30)
        o_ref[:, g * HEAD_DIM:(g + 1) * HEAD_DIM] = _dot(p.astype(vc.dtype), vc)
        p_grp = p_grp + p

    slabs = [p_grp[:, r * LANES:(r + 1) * LANES] for r in range(CMP_PER_SEL)]
    blk = _lane_index((tq, LANES))
    prev = jnp.where(blk == 0, 0.0, pltpu.roll(slabs[CMP_PER_SEL - 1], 1, 1))
    imp = prev
    for r in range(CMP_PER_SEL):
        imp = imp + slabs[r]
    cur = ((qi * tq + _iota((tq, LANES), 0)) // SEL_LEN).astype(jnp.float32)
    forced = (blk == 0) | (blk == cur) | (blk == cur - 1)
    work = jnp.where(forced, jnp.inf, imp)
    work = jnp.where(blk <= cur, work, -jnp.inf)
    sel = jnp.zeros((tq, LANES), jnp.float32)
    for _ in range(SEL_TOPN):
        mx = jnp.max(work, axis=-1, keepdims=True)
        first = _first_lane(work == mx, blk)
        pick = blk == first
        sel = jnp.where(pick, 1.0, sel)
        work = jnp.where(pick, -jnp.inf, work)
    sel_ref[0, 0] = jnp.where(blk <= cur, sel, 0.0).astype(sel_ref.dtype)


def _nsa_cmp_select(hp, kcv, lay, b, s):
    kv = lay['kv']
    n_seg = s // CMP_STRIDE
    n_sel = s // SEL_LEN
    n_cmp = (s - CMP_LEN) // CMP_STRIDE + 1
    assert SEL_TOPN <= n_sel <= LANES
    width = CMP_PER_SEL * LANES
    kcp = kcv.reshape(b, 2, kv, n_sel, CMP_PER_SEL, HEAD_DIM).transpose(0, 1, 2, 4, 3, 5)
    kcp = jnp.pad(kcp, ((0, 0),) * 4 + ((0, LANES - n_sel), (0, 0))).reshape(b, 2, kv, width, HEAD_DIM)
    tq = min(512, s)
    nq = s // tq
    gw = NSA_GROUP * HEAD_DIM
    return pl.pallas_call(
        functools.partial(_nsa_cmp_kernel, tq=tq, n_sel=n_sel, n_cmp=n_cmp),
        out_shape=(jax.ShapeDtypeStruct((b * s, lay['nsa_h'] * HEAD_DIM), jnp.float32),
                   jax.ShapeDtypeStruct((b, kv, s, LANES), MXU_DTYPE)),
        grid=(b, kv, nq),
        in_specs=[pl.BlockSpec((tq, gw), lambda bi, k, qi: (bi * nq + qi, k)),
                  pl.BlockSpec((1, 1, 1, width, HEAD_DIM), lambda bi, k, qi: (bi, 0, k, 0, 0)),
                  pl.BlockSpec((1, 1, 1, width, HEAD_DIM), lambda bi, k, qi: (bi, 1, k, 0, 0))],
        out_specs=(pl.BlockSpec((tq, gw), lambda bi, k, qi: (bi * nq + qi, k)),
                   pl.BlockSpec((1, 1, tq, LANES), lambda bi, k, qi: (bi, k, qi, 0))),
        compiler_params=_params("parallel", "parallel", "parallel"),
    )(hp, kcp, kcp)


def _nsa_main_kernel(qi_tab, ki_tab, q_ref, ks_ref, vs_ref, kw_ref, vw_ref, sel_ref, e_ref, oc_ref,
                     gl_ref, o_ref, ms_ref, as_ref, mw_ref, aw_ref, *, tq, tk):
    p_id = pl.program_id(2)
    qi = qi_tab[p_id]
    ki = ki_tab[p_id]
    last = (qi * tq + tq - 1) // tk
    first_win = jnp.maximum((qi * tq - WINDOW + 1) // tk, 0)

    @pl.when(ki == 0)
    def _():
        _init_state(ms_ref, as_ref)
        _init_state(mw_ref, aw_ref)

    q_all = jnp.concatenate([q_ref[:, g * HEAD_DIM:(g + 1) * HEAD_DIM] for g in range(NSA_GROUP)], axis=0)

    def scores(k, bias):
        s = _dot_nt(q_all, k).reshape(NSA_GROUP, tq, tk) + bias[None]
        return s.reshape(NSA_GROUP * tq, tk)

    def positions():
        return qi * tq + _iota((tq, tk), 0), ki * tk + _iota((tq, tk), 1)

    def selected(causal):
        ok = _dot(sel_ref[0, 0], e_ref[...]) > 0.5
        if causal:
            t, u = positions()
            ok = ok & (u <= t)
        _flash_step(scores(ks_ref[...], _mask_bias(ok)), vs_ref[...], ms_ref, as_ref)

    diagonal = (ki + 1) * tk > qi * tq

    @pl.when(diagonal)
    def _():
        selected(True)

    @pl.when(jnp.logical_not(diagonal))
    def _():
        selected(False)

    @pl.when(ki >= first_win)
    def _():
        t, u = positions()
        b_win = _mask_bias((u <= t) & (u > t - WINDOW))
        _flash_step(scores(kw_ref[...], b_win), vw_ref[...], mw_ref, aw_ref)

    @pl.when(ki == last)
    def _():
        gate = 1.0 / (1.0 + jnp.exp(-gl_ref[...]))
        for g in range(NSA_GROUP):
            o_cmp = oc_ref[:, g * HEAD_DIM:(g + 1) * HEAD_DIM]
            o_slc = _flash_result(as_ref, pl.ds(g * tq, tq))
            o_win = _flash_result(aw_ref, pl.ds(g * tq, tq))
            out = (gate[:, 3 * g:3 * g + 1] * o_cmp + gate[:, 3 * g + 1:3 * g + 2] * o_slc
                   + gate[:, 3 * g + 2:3 * g + 3] * o_win)
            o_ref[:, g * HEAD_DIM:(g + 1) * HEAD_DIM] = out.astype(o_ref.dtype)


def _nsa_main(hp, sel, o_cmp, gate_logits, lay, b, s):
    kv = lay['kv']
    tq = min(256, s)
    tk = min(512, s)
    nq, nk = s // tq, s // tk
    qi_tab, ki_tab = _causal_pairs(nq, tq, tk)
    gw = NSA_GROUP * HEAD_DIM
    ksb, vsb, kwb, vwb = (lay['off'][n] // HEAD_DIM for n in ('nks', 'nvs', 'nkw', 'nvw'))
    expand = jnp.asarray(_block_expansion(s, SEL_LEN), MXU_DTYPE)

    def win_blk(qt, kt, p):
        return jnp.maximum(kt[p], jnp.maximum((qt[p] * tq - WINDOW + 1) // tk, 0))

    grid_spec = pltpu.PrefetchScalarGridSpec(
        num_scalar_prefetch=2,
        grid=(b, kv, len(qi_tab)),
        in_specs=[pl.BlockSpec((tq, gw), lambda bi, k, p, qt, kt: (bi * nq + qt[p], k)),
                  pl.BlockSpec((tk, HEAD_DIM), lambda bi, k, p, qt, kt: (bi * nk + kt[p], ksb + k)),
                  pl.BlockSpec((tk, HEAD_DIM), lambda bi, k, p, qt, kt: (bi * nk + kt[p], vsb + k)),
                  pl.BlockSpec((tk, HEAD_DIM),
                               lambda bi, k, p, qt, kt: (bi * nk + win_blk(qt, kt, p), kwb + k)),
                  pl.BlockSpec((tk, HEAD_DIM),
                               lambda bi, k, p, qt, kt: (bi * nk + win_blk(qt, kt, p), vwb + k)),
                  pl.BlockSpec((1, 1, tq, LANES), lambda bi, k, p, qt, kt: (bi, k, qt[p], 0)),
                  pl.BlockSpec((LANES, tk), lambda bi, k, p, qt, kt: (0, kt[p])),
                  pl.BlockSpec((tq, gw), lambda bi, k, p, qt, kt: (bi * nq + qt[p], k)),
                  pl.BlockSpec((tq, LANES), lambda bi, k, p, qt, kt: (bi * nq + qt[p], k))],
        out_specs=pl.BlockSpec((tq, gw), lambda bi, k, p, qt, kt: (bi * nq + qt[p], k)),
        scratch_shapes=[pltpu.VMEM((NSA_GROUP * tq, LANES), jnp.float32),
                        pltpu.VMEM((NSA_GROUP * tq, HEAD_DIM + LANES), jnp.float32)] * 2)
    return pl.pallas_call(
        functools.partial(_nsa_main_kernel, tq=tq, tk=tk),
        out_shape=jax.ShapeDtypeStruct((b * s, lay['nsa_h'] * HEAD_DIM), MXU_DTYPE),
        grid_spec=grid_spec, compiler_params=_params("parallel", "parallel", "arbitrary"),
    )(jnp.asarray(qi_tab), jnp.asarray(ki_tab), hp, hp, hp, hp, hp, sel, expand, o_cmp, gate_logits)


def _layer_norm(y, g, b):
    mu = jnp.mean(y, axis=-1, keepdims=True)
    var = jnp.mean(jnp.square(y - mu), axis=-1, keepdims=True)
    return (y - mu) * lax.rsqrt(var + LN_EPS) * g + b


def _out_proj_kernel(mix_ref, w_ref, x_ref, g_ref, b_ref, wr_ref, br_ref, x1_ref, idx_ref, gate_ref,
                     acc_ref, *, alpha, n_experts):
    k = pl.program_id(1)

    @pl.when(k == 0)
    def _():
        acc_ref[...] = jnp.zeros(acc_ref.shape, jnp.float32)

    acc_ref[...] += _dot(mix_ref[...], w_ref[...])

    @pl.when(k == pl.num_programs(1) - 1)
    def _():
        x1 = _layer_norm(alpha * x_ref[...] + acc_ref[...], g_ref[...], b_ref[...])
        x1_ref[...] = x1
        logits = _dot(x1.astype(MXU_DTYPE), wr_ref[...]) + br_ref[...]
        lane = _lane_index(logits.shape)
        work = jnp.where(lane < n_experts, logits, -jnp.inf)
        idx_out = jnp.zeros(logits.shape, jnp.float32)
        val_out = jnp.zeros(logits.shape, jnp.float32)
        top = None
        for r in range(TOP_K):
            mx = jnp.max(work, axis=-1, keepdims=True)
            first = _first_lane(work == mx, lane)
            top = mx if top is None else top
            idx_out = jnp.where(lane == r, first, idx_out)
            val_out = jnp.where(lane == r, jnp.exp(mx - top), val_out)
            work = jnp.where(lane == first, -jnp.inf, work)
        idx_ref[...] = idx_out.astype(jnp.int32)
        gate_ref[...] = val_out / jnp.sum(val_out, axis=-1, keepdims=True)


def _out_proj_ln_router(mix, w_out, x, ln_g, ln_b, w_router, b_router, alpha):
    t, kdim = mix.shape
    d = w_out.shape[1]
    n_experts = w_router.shape[1]
    tm = min(256, t)
    tk = min(512, kdim)
    wr = jnp.pad(w_router, ((0, 0), (0, LANES - n_experts))).astype(MXU_DTYPE)
    br = jnp.pad(b_router.astype(jnp.float32), (0, LANES - n_experts)).reshape(1, LANES)
    row = lambda i, k: (i, 0)
    const = lambda i, k: (0, 0)
    return pl.pallas_call(
        functools.partial(_out_proj_kernel, alpha=alpha, n_experts=n_experts),
        out_shape=(jax.ShapeDtypeStruct((t, d), jnp.float32),
                   jax.ShapeDtypeStruct((t, LANES), jnp.int32),
                   jax.ShapeDtypeStruct((t, LANES), jnp.float32)),
        grid=(t // tm, kdim // tk),
        in_specs=[pl.BlockSpec((tm, tk), lambda i, k: (i, k)),
                  pl.BlockSpec((tk, d), lambda i, k: (k, 0)),
                  pl.BlockSpec((tm, d), row),
                  pl.BlockSpec((1, d), const), pl.BlockSpec((1, d), const),
                  pl.BlockSpec((d, LANES), const), pl.BlockSpec((1, LANES), const)],
        out_specs=(pl.BlockSpec((tm, d), row), pl.BlockSpec((tm, LANES), row),
                   pl.BlockSpec((tm, LANES), row)),
        scratch_shapes=[pltpu.VMEM((tm, d), jnp.float32)],
        compiler_params=_params("parallel", "arbitrary"),
    )(mix, w_out, x, ln_g.reshape(1, d), ln_b.reshape(1, d), wr, br)


MOE_TILE = 256
COMBINE_TILE = 128
GATHER_UNROLL = 8


GLU_GROUP = 2 * LANES


def _regroup_kernel(w_ref, p_ref, o_ref):
    w = w_ref[0].astype(MXU_DTYPE)
    for c in range(w.shape[1] // GLU_GROUP):
        cols = slice(c * GLU_GROUP, (c + 1) * GLU_GROUP)
        o_ref[0, :, cols] = _dot(w[:, cols], p_ref[...]).astype(o_ref.dtype)


def _regroup_gate_up(w_gate_up):
    e, d, f2 = w_gate_up.shape
    j = np.arange(GLU_GROUP)
    src = np.where(j < LANES, 2 * j, 2 * (j - LANES) + 1)
    perm = np.zeros((GLU_GROUP, GLU_GROUP), np.float32)
    perm[src, j] = 1.0
    tk = min(1024, d)
    return pl.pallas_call(
        _regroup_kernel,
        out_shape=jax.ShapeDtypeStruct((e, d, f2), MXU_DTYPE),
        grid=(e, d // tk),
        in_specs=[pl.BlockSpec((1, tk, f2), lambda ei, ki: (ei, ki, 0)),
                  pl.BlockSpec((GLU_GROUP, GLU_GROUP), lambda ei, ki: (0, 0))],
        out_specs=pl.BlockSpec((1, tk, f2), lambda ei, ki: (ei, ki, 0)),
        compiler_params=_params("parallel", "parallel"),
    )(w_gate_up, jnp.asarray(perm, MXU_DTYPE))


def _regroup_bias(b_gate_up):
    e, f2 = b_gate_up.shape
    b = b_gate_up.astype(jnp.float32).reshape(e, f2 // GLU_GROUP, LANES, 2)
    return b.transpose(0, 1, 3, 2).reshape(e, 1, f2)


def _moe_kernel(rt_ref, te_ref, nu_ref, x_hbm, wgu_ref, bgu_ref, wd_ref, bd_ref, o_ref, xbuf, sem):
    i = pl.program_id(0)
    n_used = nu_ref[0]

    def row_copy(tile, slot, r):
        tok = rt_ref[tile * MOE_TILE + r]
        return pltpu.make_async_copy(x_hbm.at[pl.ds(tok, 1)], xbuf.at[slot, pl.ds(r, 1)], sem.at[slot])

    def gather(tile, slot):
        def body(r, carry):
            row_copy(tile, slot, r).start()
            return carry
        lax.fori_loop(0, MOE_TILE, body, 0, unroll=GATHER_UNROLL)

    @pl.when(i == 0)
    def _():
        gather(0, 0)

    @pl.when(i + 1 < n_used)
    def _():
        gather(i + 1, (i + 1) % 2)

    @pl.when(i < n_used)
    def _():
        slot = i % 2
        pltpu.make_async_copy(x_hbm.at[pl.ds(0, MOE_TILE)], xbuf.at[slot], sem.at[slot]).wait()
        x = xbuf[slot].astype(MXU_DTYPE)
        hid = _dot(x, wgu_ref[0]) + bgu_ref[0]
        acts = []
        for c in range(hid.shape[1] // GLU_GROUP):
            h_glu = jnp.minimum(hid[:, c * GLU_GROUP:c * GLU_GROUP + LANES], SWIGLU_LIMIT)
            h_lin = jnp.clip(hid[:, c * GLU_GROUP + LANES:(c + 1) * GLU_GROUP], -SWIGLU_LIMIT, SWIGLU_LIMIT)
            acts.append(h_glu * (1.0 / (1.0 + jnp.exp(-SWIGLU_ALPHA * h_glu))) * (h_lin + 1.0))
        act = jnp.concatenate(acts, axis=1)
        o_ref[...] = _dot(act.astype(MXU_DTYPE), wd_ref[0]) + bd_ref[0]

    @pl.when(i >= n_used)
    def _():
        o_ref[...] = jnp.zeros(o_ref.shape, o_ref.dtype)


def _moe_experts(x1, row_token, tile_expert, n_used, wgu, bgu, wd, bd):
    t, d = x1.shape
    f2 = wgu.shape[2]
    n_tiles = tile_expert.shape[0]
    wmap = lambda i, rt, te, nu: (te[i], 0, 0)
    grid_spec = pltpu.PrefetchScalarGridSpec(
        num_scalar_prefetch=3,
        grid=(n_tiles,),
        in_specs=[pl.BlockSpec(memory_space=pl.ANY),
                  pl.BlockSpec((1, d, f2), wmap), pl.BlockSpec((1, 1, f2), wmap),
                  pl.BlockSpec((1, f2 // 2, d), wmap), pl.BlockSpec((1, 1, d), wmap)],
        out_specs=pl.BlockSpec((MOE_TILE, d), lambda i, rt, te, nu: (i, 0)),
        scratch_shapes=[pltpu.VMEM((2, MOE_TILE, d), jnp.float32), pltpu.SemaphoreType.DMA((2,))])
    return pl.pallas_call(
        _moe_kernel,
        out_shape=jax.ShapeDtypeStruct((n_tiles * MOE_TILE, d), jnp.float32),
        grid_spec=grid_spec, compiler_params=_params("arbitrary"),
    )(row_token, tile_expert, n_used, x1, wgu, bgu, wd, bd)


def _combine_kernel(pos_ref, y_hbm, x_ref, gate_ref, g_ref, b_ref, o_ref, ob_ref, ybuf, sem, *, alpha):
    i = pl.program_id(0)
    n_rows = COMBINE_TILE * TOP_K

    def gather(tile, slot):
        def body(r, carry):
            src = pos_ref[tile * n_rows + r]
            pltpu.make_async_copy(y_hbm.at[pl.ds(src, 1)], ybuf.at[slot, pl.ds(r, 1)], sem.at[slot]).start()
            return carry
        lax.fori_loop(0, n_rows, body, 0, unroll=GATHER_UNROLL)

    @pl.when(i == 0)
    def _():
        gather(0, 0)

    @pl.when(i + 1 < pl.num_programs(0))
    def _():
        gather(i + 1, (i + 1) % 2)

    slot = i % 2
    pltpu.make_async_copy(y_hbm.at[pl.ds(0, n_rows)], ybuf.at[slot], sem.at[slot]).wait()
    gate = gate_ref[...]
    ffn = jnp.zeros(x_ref.shape, jnp.float32)
    for k in range(TOP_K):
        ffn = ffn + gate[:, k:k + 1] * ybuf[slot, pl.ds(k * COMBINE_TILE, COMBINE_TILE)]
    x2 = _layer_norm(alpha * x_ref[...] + ffn, g_ref[...], b_ref[...])
    o_ref[...] = x2
    ob_ref[...] = x2.astype(ob_ref.dtype)


def _combine_ln(y, pos, x1, gate, ln_g, ln_b, alpha):
    t, d = x1.shape
    row = lambda i, p: (i, 0)
    const = lambda i, p: (0, 0)
    grid_spec = pltpu.PrefetchScalarGridSpec(
        num_scalar_prefetch=1,
        grid=(t // COMBINE_TILE,),
        in_specs=[pl.BlockSpec(memory_space=pl.ANY),
                  pl.BlockSpec((COMBINE_TILE, d), row), pl.BlockSpec((COMBINE_TILE, LANES), row),
                  pl.BlockSpec((1, d), const), pl.BlockSpec((1, d), const)],
        out_specs=(pl.BlockSpec((COMBINE_TILE, d), row), pl.BlockSpec((COMBINE_TILE, d), row)),
        scratch_shapes=[pltpu.VMEM((2, COMBINE_TILE * TOP_K, d), jnp.float32),
                        pltpu.SemaphoreType.DMA((2,))])
    return pl.pallas_call(
        functools.partial(_combine_kernel, alpha=alpha),
        out_shape=(jax.ShapeDtypeStruct((t, d), jnp.float32), jax.ShapeDtypeStruct((t, d), MXU_DTYPE)),
        grid_spec=grid_spec, compiler_params=_params("arbitrary"),
    )(pos, y, x1, gate, ln_g.reshape(1, d), ln_b.reshape(1, d))


def _route(top_idx, n_experts):
    t = top_idx.shape[0]
    n_assign = t * TOP_K
    n_tiles = n_assign // MOE_TILE + n_experts
    expert = top_idx.reshape(n_assign)
    onehot = (expert[:, None] == jnp.arange(n_experts, dtype=jnp.int32)[None, :]).astype(jnp.int32)
    running = jnp.cumsum(onehot, axis=0)
    counts = running[-1]
    rank = jnp.take_along_axis(running, expert[:, None], axis=1)[:, 0] - 1
    padded = (counts + MOE_TILE - 1) // MOE_TILE * MOE_TILE
    pad_end = jnp.cumsum(padded)
    dest = (pad_end - padded)[expert] + rank
    slot = dest.reshape(t // COMBINE_TILE, COMBINE_TILE, TOP_K).transpose(0, 2, 1).reshape(n_assign)
    token = jnp.arange(n_assign, dtype=jnp.int32) // TOP_K
    row_token = jnp.zeros((n_tiles * MOE_TILE,), jnp.int32).at[dest].set(token)
    tile_start = jnp.arange(n_tiles, dtype=jnp.int32) * MOE_TILE
    tile_expert = jnp.minimum(
        jnp.sum((pad_end[None, :] <= tile_start[:, None]).astype(jnp.int32), axis=1), n_experts - 1)
    n_used = (pad_end[-1:] // MOE_TILE).astype(jnp.int32)
    return row_token, tile_expert, n_used, slot.astype(jnp.int32)


def _rope_tables(positions):
    inv_freq = 1.0 / (ROPE_THETA ** (jnp.arange(0, ROT_DIM, 2, dtype=jnp.float32) / ROT_DIM))
    ang = positions.astype(jnp.float32).reshape(-1)[:, None] * inv_freq
    cos, sin = jnp.cos(ang), jnp.sin(ang)
    rest = HEAD_DIM - ROT_DIM
    n = ang.shape[0]
    ct = jnp.concatenate([cos, cos, jnp.ones((n, rest), jnp.float32)], axis=1)
    sa = jnp.concatenate([jnp.zeros_like(sin), sin, jnp.zeros((n, rest), jnp.float32)], axis=1)
    sb = jnp.concatenate([-sin, jnp.zeros_like(sin), jnp.zeros((n, rest), jnp.float32)], axis=1)
    return ct, sa, sb


def _gate_weight(w_in_l, lay):
    d = w_in_l.shape[0]
    kv = lay['kv']
    o = lay['ref_off']['ngate']
    wg = w_in_l[:, o:o + lay['size']['ngate']].reshape(d, kv, NSA_GROUP * 3)
    wg = jnp.pad(wg, ((0, 0), (0, 0), (0, LANES - NSA_GROUP * 3)))
    return wg.reshape(d, kv * LANES).astype(MXU_DTYPE)


def kernel(x, positions, w_in, nsa_cmp_pos, nsa_cmp_w1, nsa_cmp_w2, diff_lambda, diff_subln_g, w_out,
           ln1_g, ln1_b, w_router, b_router, w_gate_up, b_gate_up, w_down, b_down, ln2_g, ln2_b):
    b, s, d = x.shape
    depth = w_in.shape[0]
    n_experts = w_router.shape[2]
    lay = _layout(d)
    alpha = (2 * depth) ** 0.25
    ct, sa, sb = _rope_tables(positions)
    xf = x.reshape(b * s, d).astype(jnp.float32)
    xb = xf.astype(MXU_DTYPE)
    for layer in range(depth):
        w_l = w_in[layer]
        w_perm = jnp.concatenate(
            [w_l[:, lay['ref_off'][n]:lay['ref_off'][n] + lay['size'][n]] for n in ROPED + PLAIN],
            axis=1).astype(MXU_DTYPE)
        hp = _project(xb, w_perm, ct, sa, sb, lay['n_roped'])
        gate_logits = _gate_logits(xb, _gate_weight(w_l, lay))

        kcv = _nsa_compress(hp, nsa_cmp_pos[layer], nsa_cmp_w1[layer], nsa_cmp_w2[layer], lay, b, s)
        o_cmp, sel = _nsa_cmp_select(hp, kcv, lay, b, s)
        y_nsa = _nsa_main(hp, sel, o_cmp, gate_logits, lay, b, s)
        lambda_init = 0.8 - 0.6 * math.exp(-0.3 * layer)
        y_diff = _diff_attention(hp, diff_lambda[layer], diff_subln_g[layer], lay, b, s, lambda_init)
        y_moba = _moba_attention(hp, lay, b, s)
        mix = jnp.concatenate([y_nsa, y_diff, y_moba], axis=1)

        x1, top_idx, top_gate = _out_proj_ln_router(
            mix, w_out[layer].astype(MXU_DTYPE), xf, ln1_g[layer], ln1_b[layer],
            w_router[layer], b_router[layer], alpha)

        row_token, tile_expert, n_used, slot = _route(top_idx[:, :TOP_K], n_experts)
        y = _moe_experts(
            x1, row_token, tile_expert, n_used,
            _regroup_gate_up(w_gate_up[layer]), _regroup_bias(b_gate_up[layer]),
            w_down[layer].astype(MXU_DTYPE), b_down[layer].astype(jnp.float32).reshape(n_experts, 1, d))
        xf, xb = _combine_ln(y, slot, x1, top_gate, ln2_g[layer], ln2_b[layer], alpha)
    return xf.reshape(b, s, d).astype(x.dtype)
```

```python
import functools
import math

import numpy as np
import jax
import jax.numpy as jnp
from jax import lax
from jax.experimental import pallas as pl
from jax.experimental.pallas import tpu as pltpu

HEAD_DIM = 128
ROT_DIM = HEAD_DIM // 4
ROT_HALF = ROT_DIM // 2
ROPE_THETA = 500000.0
ATTN_SCALE = HEAD_DIM ** -0.5
NEG_INF = -1e30
LN_EPS = 1e-5

NSA_GROUP = 4
CMP_LEN = 32
CMP_STRIDE = 16
SEL_LEN = 64
SEL_TOPN = 16
WINDOW = 512
CMP_PER_SEL = SEL_LEN // CMP_STRIDE

DIFF_VDIM = 2 * HEAD_DIM
MOBA_BLOCK = 256
MOBA_TOPK = 3

TOP_K = 4
SWIGLU_LIMIT = 7.0
SWIGLU_ALPHA = 1.702

LANES = 128
MXU_DTYPE = jnp.bfloat16
VMEM_LIMIT = 56 * 1024 * 1024

ROPED = ('nq', 'nkc', 'nks', 'nkw', 'dq', 'dk', 'mq', 'mk')
PLAIN = ('nvc', 'nvs', 'nvw', 'dv', 'mv')
REF_ORDER = ('nq', 'nkc', 'nvc', 'nks', 'nvs', 'nkw', 'nvw', 'ngate', 'dq', 'dk', 'dv', 'mq', 'mk', 'mv')


def _layout(d):
    nsa_h = d // 256
    kv = nsa_h // NSA_GROUP
    diff_h = d // 1024
    moba_h = d // 512
    size = dict(nq=nsa_h * HEAD_DIM, nkc=kv * HEAD_DIM, nvc=kv * HEAD_DIM, nks=kv * HEAD_DIM,
                nvs=kv * HEAD_DIM, nkw=kv * HEAD_DIM, nvw=kv * HEAD_DIM, ngate=3 * nsa_h,
                dq=2 * diff_h * HEAD_DIM, dk=2 * diff_h * HEAD_DIM, dv=diff_h * DIFF_VDIM,
                mq=moba_h * HEAD_DIM, mk=moba_h * HEAD_DIM, mv=moba_h * HEAD_DIM)
    ref_off, o = {}, 0
    for n in REF_ORDER:
        ref_off[n] = o
        o += size[n]
    off, o = {}, 0
    for n in ROPED + PLAIN:
        off[n] = o
        o += size[n]
    n_roped = sum(size[n] for n in ROPED)
    return dict(nsa_h=nsa_h, kv=kv, diff_h=diff_h, moba_h=moba_h, size=size, ref_off=ref_off,
                off=off, n_roped=n_roped, n_cols=o)


def _params(*sem):
    return pltpu.CompilerParams(dimension_semantics=sem, vmem_limit_bytes=VMEM_LIMIT)


def _dot(a, b):
    return jnp.dot(a, b, preferred_element_type=jnp.float32)


def _dot_nt(a, b):
    return lax.dot_general(a, b, (((1,), (1,)), ((), ())), preferred_element_type=jnp.float32)


def _iota(shape, dim):
    return lax.broadcasted_iota(jnp.int32, shape, dim)


def _lane_index(shape):
    return _iota(shape, 1).astype(jnp.float32)


def _first_lane(hit, lane):
    return jnp.min(jnp.where(hit, lane, float(LANES)), axis=-1, keepdims=True)


def _proj_kernel(x_ref, w_ref, ct_ref, sa_ref, sb_ref, o_ref, *, n_rope_blocks, tn):
    j = pl.program_id(1)
    acc = _dot(x_ref[...], w_ref[...])

    @pl.when(j < n_rope_blocks)
    def _():
        ct, sa, sb = ct_ref[...], sa_ref[...], sb_ref[...]
        for c in range(tn // HEAD_DIM):
            a = acc[:, c * HEAD_DIM:(c + 1) * HEAD_DIM]
            r = (a * ct + pltpu.roll(a, ROT_HALF, 1) * sa
                 + pltpu.roll(a, HEAD_DIM - ROT_HALF, 1) * sb)
            o_ref[:, c * HEAD_DIM:(c + 1) * HEAD_DIM] = r.astype(o_ref.dtype)

    @pl.when(j >= n_rope_blocks)
    def _():
        o_ref[...] = acc.astype(o_ref.dtype)


def _project(xb, w, ct, sa, sb, n_roped):
    m, k = xb.shape
    n = w.shape[1]
    tm = min(1024, m)
    tn = next(t for t in (512, 256, 128) if n % t == 0 and n_roped % t == 0)
    kern = functools.partial(_proj_kernel, n_rope_blocks=n_roped // tn, tn=tn)
    return pl.pallas_call(
        kern,
        out_shape=jax.ShapeDtypeStruct((m, n), MXU_DTYPE),
        grid=(m // tm, n // tn),
        in_specs=[pl.BlockSpec((tm, k), lambda i, j: (i, 0)),
                  pl.BlockSpec((k, tn), lambda i, j: (0, j)),
                  pl.BlockSpec((tm, HEAD_DIM), lambda i, j: (i, 0)),
                  pl.BlockSpec((tm, HEAD_DIM), lambda i, j: (i, 0)),
                  pl.BlockSpec((tm, HEAD_DIM), lambda i, j: (i, 0))],
        out_specs=pl.BlockSpec((tm, tn), lambda i, j: (i, j)),
        compiler_params=_params("parallel", "arbitrary"),
    )(xb, w, ct, sa, sb)


def _gate_kernel(x_ref, w_ref, o_ref):
    o_ref[...] = _dot(x_ref[...], w_ref[...])


def _gate_logits(xb, wg):
    m, k = xb.shape
    n = wg.shape[1]
    tm = min(1024, m)
    return pl.pallas_call(
        _gate_kernel,
        out_shape=jax.ShapeDtypeStruct((m, n), jnp.float32),
        grid=(m // tm,),
        in_specs=[pl.BlockSpec((tm, k), lambda i: (i, 0)),
                  pl.BlockSpec((k, n), lambda i: (0, 0))],
        out_specs=pl.BlockSpec((tm, n), lambda i: (i, 0)),
        compiler_params=_params("parallel"),
    )(xb, wg)


EXP2_SCALE = ATTN_SCALE * math.log2(math.e)


def _mask_bias(mask):
    return jnp.where(mask, 0.0, -jnp.inf)


def _lanes(x, n):
    return x if n == LANES else jnp.concatenate([x] * (n // LANES), axis=1)


def _flash_step(s, v, m_ref, acc_ref):
    dv = v.shape[1]
    m_old = m_ref[...]
    m_new = jnp.maximum(m_old, jnp.max(s, axis=-1, keepdims=True))
    p = jnp.exp2((s - _lanes(m_new, s.shape[1])) * EXP2_SCALE)
    alpha = jnp.exp2((m_old - m_new) * EXP2_SCALE)
    acc_ref[:, :dv] = _lanes(alpha, dv) * acc_ref[:, :dv] + _dot(p.astype(v.dtype), v)
    acc_ref[:, dv:] = alpha * acc_ref[:, dv:] + jnp.sum(p, axis=-1, keepdims=True)
    m_ref[...] = m_new


def _flash_result(acc_ref, rows):
    dv = acc_ref.shape[1] - LANES
    denom = jnp.maximum(acc_ref[rows, dv:], 1e-30)
    return acc_ref[rows, :dv] / _lanes(denom, dv)


def _init_state(m_ref, acc_ref):
    m_ref[...] = jnp.full(m_ref.shape, NEG_INF, jnp.float32)
    acc_ref[...] = jnp.zeros(acc_ref.shape, jnp.float32)


def _key_rows(ki, tk):
    return pl.ds(pl.multiple_of(ki * tk, tk), tk)


def _for_tiles(lo, hi, body):
    def wrapped(ki, carry):
        body(ki)
        return carry
    lax.fori_loop(lo, hi, wrapped, 0)


def _diff_kernel(lam_ref, g_ref, q_ref, k_ref, v_ref, o_ref, m_ref, acc_ref, *, tq, lambda_init):
    qi = pl.program_id(2)
    tk = tq
    _init_state(m_ref, acc_ref)

    def tile(ki, masked):
        rows = _key_rows(ki, tk)
        maps = []
        for mp in range(2):
            q = q_ref[:, mp * HEAD_DIM:(mp + 1) * HEAD_DIM]
            maps.append(_dot_nt(q, k_ref[rows, mp * HEAD_DIM:(mp + 1) * HEAD_DIM]))
        if masked:
            bias = _mask_bias(_iota((tq, tk), 1) <= _iota((tq, tk), 0))
            maps = [s + bias for s in maps]
        _flash_step(jnp.concatenate(maps, axis=0), v_ref[rows, :], m_ref, acc_ref)

    _for_tiles(0, qi, lambda ki: tile(ki, False))
    tile(qi, True)

    lv = lam_ref[...]
    lam = (jnp.exp(jnp.sum(lv[0:1] * lv[1:2], axis=-1, keepdims=True))
           - jnp.exp(jnp.sum(lv[2:3] * lv[3:4], axis=-1, keepdims=True)) + lambda_init)
    o = _flash_result(acc_ref, pl.ds(0, tq)) - lam * _flash_result(acc_ref, pl.ds(tq, tq))
    o = o * lax.rsqrt(jnp.mean(jnp.square(o), axis=-1, keepdims=True) + LN_EPS) * g_ref[...]
    o_ref[...] = (o * (1.0 - lambda_init)).astype(o_ref.dtype)


def _diff_attention(hp, lam_vecs, subln_g, lay, b, s, lambda_init):
    hd = lay['diff_h']
    tq = min(512, s)
    nq = s // tq
    qb, kb, vb = (lay['off'][n] // DIFF_VDIM for n in ('dq', 'dk', 'dv'))
    assert all(lay['off'][n] % DIFF_VDIM == 0 for n in ('dq', 'dk', 'dv'))
    kern = functools.partial(_diff_kernel, tq=tq, lambda_init=lambda_init)
    return pl.pallas_call(
        kern, out_shape=jax.ShapeDtypeStruct((b * s, hd * DIFF_VDIM), MXU_DTYPE),
        grid=(b, hd, nq),
        in_specs=[pl.BlockSpec((4, HEAD_DIM), lambda bi, h, qi: (0, 0)),
                  pl.BlockSpec((1, DIFF_VDIM), lambda bi, h, qi: (0, 0)),
                  pl.BlockSpec((tq, DIFF_VDIM), lambda bi, h, qi: (bi * nq + qi, qb + h)),
                  pl.BlockSpec((s, DIFF_VDIM), lambda bi, h, qi: (bi, kb + h)),
                  pl.BlockSpec((s, DIFF_VDIM), lambda bi, h, qi: (bi, vb + h))],
        out_specs=pl.BlockSpec((tq, DIFF_VDIM), lambda bi, h, qi: (bi * nq + qi, h)),
        scratch_shapes=[pltpu.VMEM((2 * tq, LANES), jnp.float32),
                        pltpu.VMEM((2 * tq, DIFF_VDIM + LANES), jnp.float32)],
        compiler_params=_params("parallel", "parallel", "arbitrary"),
    )(lam_vecs.astype(jnp.float32), subln_g.reshape(1, DIFF_VDIM).astype(jnp.float32), hp, hp, hp)


def _kmean_kernel(k_ref, o_ref, *, n_blk):
    k = k_ref[...].astype(jnp.float32).reshape(n_blk, MOBA_BLOCK, HEAD_DIM)
    o_ref[0, 0] = jnp.mean(k, axis=1)


def _moba_kmean(hp, lay, b, s):
    h = lay['moba_h']
    n_blk = s // MOBA_BLOCK
    kb = lay['off']['mk'] // HEAD_DIM
    return pl.pallas_call(
        functools.partial(_kmean_kernel, n_blk=n_blk),
        out_shape=jax.ShapeDtypeStruct((b, h, n_blk, HEAD_DIM), jnp.float32),
        grid=(b, h),
        in_specs=[pl.BlockSpec((s, HEAD_DIM), lambda bi, hi: (bi, kb + hi))],
        out_specs=pl.BlockSpec((1, 1, n_blk, HEAD_DIM), lambda bi, hi: (bi, hi, 0, 0)),
        compiler_params=_params("parallel", "parallel"),
    )(hp)


def _block_expansion(s, block):
    return (np.arange(s)[None, :] // block == np.arange(LANES)[:, None]).astype(np.float32)


def _lane_column(x, n):
    return jnp.sum(jnp.where(_iota(x.shape, 1) == n, x, 0.0), axis=-1, keepdims=True)


def _moba_kernel(q_ref, k_ref, v_ref, km_ref, o_ref, bias_ref, m_ref, acc_ref, *, tq, tk):
    qi = pl.program_id(2)
    q = q_ref[...]
    _init_state(m_ref, acc_ref)
    score = _dot_nt(q, km_ref[0, 0].astype(q.dtype))
    blk = _lane_index((tq, LANES))
    own = ((qi * tq + _iota((tq, LANES), 0)) // MOBA_BLOCK).astype(jnp.float32)
    work = jnp.where(blk < own, score, -jnp.inf)
    sel = jnp.where(blk == own, 1.0, 0.0)
    for r in range(MOBA_TOPK):
        mx = jnp.max(work, axis=-1, keepdims=True)
        first = _first_lane(work == mx, blk)
        pick = blk == first
        sel = jnp.where(pick & (own > r), 1.0, sel)
        work = jnp.where(pick, -jnp.inf, work)
    bias_ref[...] = _mask_bias(sel > 0.5)

    def tile(ki, causal):
        rows = _key_rows(ki, tk)
        cols = []
        for c in range(tk // MOBA_BLOCK):
            col = _lane_column(bias_ref[...], ki * (tk // MOBA_BLOCK) + c)
            cols.append(jnp.broadcast_to(col, (tq, MOBA_BLOCK)))
        bias = cols[0] if len(cols) == 1 else jnp.concatenate(cols, axis=1)
        if causal:
            t = qi * tq + _iota((tq, tk), 0)
            u = ki * tk + _iota((tq, tk), 1)
            bias = jnp.where(u <= t, bias, -jnp.inf)
        _flash_step(_dot_nt(q, k_ref[rows, :]) + bias, v_ref[rows, :], m_ref, acc_ref)

    past = qi * (tq // tk)
    _for_tiles(0, past, lambda ki: tile(ki, False))
    for d in range(tq // tk):
        tile(past + d, True)
    o_ref[...] = _flash_result(acc_ref, pl.ds(0, tq)).astype(o_ref.dtype)


def _moba_attention(hp, lay, b, s):
    h = lay['moba_h']
    n_blk = s // MOBA_BLOCK
    assert s % MOBA_BLOCK == 0 and MOBA_TOPK <= n_blk <= LANES
    km = _moba_kmean(hp, lay, b, s)
    km = jnp.pad(km, ((0, 0), (0, 0), (0, LANES - n_blk), (0, 0)))
    tq = min(1024, s)
    tk = min(512, s)
    nq = s // tq
    assert tq % tk == 0 and tk % MOBA_BLOCK == 0
    qb, kb, vb = (lay['off'][n] // HEAD_DIM for n in ('mq', 'mk', 'mv'))
    return pl.pallas_call(
        functools.partial(_moba_kernel, tq=tq, tk=tk),
        out_shape=jax.ShapeDtypeStruct((b * s, h * HEAD_DIM), MXU_DTYPE),
        grid=(b, h, nq),
        in_specs=[pl.BlockSpec((tq, HEAD_DIM), lambda bi, hi, qi: (bi * nq + qi, qb + hi)),
                  pl.BlockSpec((s, HEAD_DIM), lambda bi, hi, qi: (bi, kb + hi)),
                  pl.BlockSpec((s, HEAD_DIM), lambda bi, hi, qi: (bi, vb + hi)),
                  pl.BlockSpec((1, 1, LANES, HEAD_DIM), lambda bi, hi, qi: (bi, hi, 0, 0))],
        out_specs=pl.BlockSpec((tq, HEAD_DIM), lambda bi, hi, qi: (bi * nq + qi, hi)),
        scratch_shapes=[pltpu.VMEM((tq, LANES), jnp.float32), pltpu.VMEM((tq, LANES), jnp.float32),
                        pltpu.VMEM((tq, HEAD_DIM + LANES), jnp.float32)],
        compiler_params=_params("parallel", "parallel", "arbitrary"),
    )(hp, hp, hp, km)


def _gelu_tanh(x):
    return 0.5 * x * (1.0 + jnp.tanh(math.sqrt(2.0 / math.pi) * (x + 0.044715 * (x * x * x))))


def _compress_kernel(seg_ref, plo_ref, phi_ref, w1lo_ref, w1hi_ref, w2_ref, o_ref, *, n_seg):
    seg = seg_ref[0, 0, 0].astype(jnp.float32)
    lo = _dot((seg + plo_ref[0]).astype(MXU_DTYPE), w1lo_ref[0])
    hi = _dot((seg + phi_ref[0]).astype(MXU_DTYPE), w1hi_ref[0])
    pre = lo + pltpu.roll(hi, n_seg - 1, 0)
    o_ref[0, 0, 0] = _dot(_gelu_tanh(pre).astype(MXU_DTYPE), w2_ref[0]).astype(o_ref.dtype)


def _nsa_compress(hp, cmp_pos, cmp_w1, cmp_w2, lay, b, s):
    kv = lay['kv']
    n_seg = s // CMP_STRIDE
    half = CMP_STRIDE * HEAD_DIM

    def segments(name):
        o = lay['off'][name]
        t = hp[:, o:o + kv * HEAD_DIM].reshape(b, n_seg, CMP_STRIDE, kv, HEAD_DIM)
        return t.transpose(0, 3, 1, 2, 4).reshape(b, kv, n_seg, half)

    seg = jnp.stack([segments('nkc'), segments('nvc')], axis=1)
    pos = cmp_pos.astype(jnp.float32).reshape(2, CMP_LEN * HEAD_DIM)
    plo = pos[:, :half].reshape(2, 1, half)
    phi = pos[:, half:].reshape(2, 1, half)
    w1 = cmp_w1.astype(MXU_DTYPE)
    w1lo, w1hi = w1[:, :half], w1[:, half:]
    w2 = cmp_w2.astype(MXU_DTYPE)
    return pl.pallas_call(
        functools.partial(_compress_kernel, n_seg=n_seg),
        out_shape=jax.ShapeDtypeStruct((b, 2, kv, n_seg, HEAD_DIM), MXU_DTYPE),
        grid=(b, 2, kv),
        in_specs=[pl.BlockSpec((1, 1, 1, n_seg, half), lambda bi, c, k: (bi, c, k, 0, 0)),
                  pl.BlockSpec((1, 1, half), lambda bi, c, k: (c, 0, 0)),
                  pl.BlockSpec((1, 1, half), lambda bi, c, k: (c, 0, 0)),
                  pl.BlockSpec((1, half, HEAD_DIM), lambda bi, c, k: (c, 0, 0)),
                  pl.BlockSpec((1, half, HEAD_DIM), lambda bi, c, k: (c, 0, 0)),
                  pl.BlockSpec((1, HEAD_DIM, HEAD_DIM), lambda bi, c, k: (c, 0, 0))],
        out_specs=pl.BlockSpec((1, 1, 1, n_seg, HEAD_DIM), lambda bi, c, k: (bi, c, k, 0, 0)),
        compiler_params=_params("parallel", "parallel", "parallel"),
    )(seg, plo, phi, w1lo, w1hi, w2)


def _nsa_cmp_kernel(q_ref, kc_ref, vc_ref, o_ref, sel_ref, *, tq, n_sel, n_cmp):
    qi = pl.program_id(2)
    width = CMP_PER_SEL * LANES
    t = qi * tq + _iota((tq, width), 0)
    pos = _iota((tq, width), 1)
    m_idx = pos % LANES
    n_idx = CMP_PER_SEL * m_idx + pos // LANES
    valid = (m_idx < n_sel) & (n_idx < n_cmp) & (n_idx * CMP_STRIDE + CMP_LEN - 1 <= t)
    kc = kc_ref[0, 0, 0]
    vc = vc_ref[0, 0, 0]
    p_grp = jnp.zeros((tq, width), jnp.float32)
    for g in range(NSA_GROUP):
        q = q_ref[:, g * HEAD_DIM:(g + 1) * HEAD_DIM]
        s = jnp.where(valid, _dot_nt(q, kc) * ATTN_SCALE, NEG_INF)
        p = jnp.where(valid, jnp.exp(s - jnp.max(s, axis=-1, keepdims=True)), 0.0)
        p = p / jnp.maximum(jnp.sum(p, axis=-1, keepdims=True), 1e-30)
        o_ref[:, g * HEAD_DIM:(g + 1) * HEAD_DIM] = _dot(p.astype(vc.dtype), vc)
        p_grp = p_grp + p

    slabs = [p_grp[:, r * LANES:(r + 1) * LANES] for r in range(CMP_PER_SEL)]
    blk = _lane_index((tq, LANES))
    prev = jnp.where(blk == 0, 0.0, pltpu.roll(slabs[CMP_PER_SEL - 1], 1, 1))
    imp = prev
    for r in range(CMP_PER_SEL):
        imp = imp + slabs[r]
    cur = ((qi * tq + _iota((tq, LANES), 0)) // SEL_LEN).astype(jnp.float32)
    forced = (blk == 0) | (blk == cur) | (blk == cur - 1)
    work = jnp.where(forced, jnp.inf, imp)
    work = jnp.where(blk <= cur, work, -jnp.inf)
    sel = jnp.zeros((tq, LANES), jnp.float32)
    for _ in range(SEL_TOPN):
        mx = jnp.max(work, axis=-1, keepdims=True)
        first = _first_lane(work == mx, blk)
        pick = blk == first
        sel = jnp.where(pick, 1.0, sel)
        work = jnp.where(pick, -jnp.inf, work)
    sel_ref[0, 0] = jnp.where(blk <= cur, sel, 0.0).astype(sel_ref.dtype)


def _nsa_cmp_select(hp, kcv, lay, b, s):
    kv = lay['kv']
    n_seg = s // CMP_STRIDE
    n_sel = s // SEL_LEN
    n_cmp = (s - CMP_LEN) // CMP_STRIDE + 1
    assert SEL_TOPN <= n_sel <= LANES
    width = CMP_PER_SEL * LANES
    kcp = kcv.reshape(b, 2, kv, n_sel, CMP_PER_SEL, HEAD_DIM).transpose(0, 1, 2, 4, 3, 5)
    kcp = jnp.pad(kcp, ((0, 0),) * 4 + ((0, LANES - n_sel), (0, 0))).reshape(b, 2, kv, width, HEAD_DIM)
    tq = min(512, s)
    nq = s // tq
    gw = NSA_GROUP * HEAD_DIM
    return pl.pallas_call(
        functools.partial(_nsa_cmp_kernel, tq=tq, n_sel=n_sel, n_cmp=n_cmp),
        out_shape=(jax.ShapeDtypeStruct((b * s, lay['nsa_h'] * HEAD_DIM), jnp.float32),
                   jax.ShapeDtypeStruct((b, kv, s, LANES), MXU_DTYPE)),
        grid=(b, kv, nq),
        in_specs=[pl.BlockSpec((tq, gw), lambda bi, k, qi: (bi * nq + qi, k)),
                  pl.BlockSpec((1, 1, 1, width, HEAD_DIM), lambda bi, k, qi: (bi, 0, k, 0, 0)),
                  pl.BlockSpec((1, 1, 1, width, HEAD_DIM), lambda bi, k, qi: (bi, 1, k, 0, 0))],
        out_specs=(pl.BlockSpec((tq, gw), lambda bi, k, qi: (bi * nq + qi, k)),
                   pl.BlockSpec((1, 1, tq, LANES), lambda bi, k, qi: (bi, k, qi, 0))),
        compiler_params=_params("parallel", "parallel", "parallel"),
    )(hp, kcp, kcp)


def _nsa_main_kernel(q_ref, ks_ref, vs_ref, kw_ref, vw_ref, sel_ref, e_ref, oc_ref, gl_ref, o_ref,
                     ms_ref, as_ref, mw_ref, aw_ref, *, tq, tk):
    qi = pl.program_id(2)
    last = (qi * tq) // tk
    first_win = jnp.maximum((qi * tq - WINDOW + 1) // tk, 0)
    _init_state(ms_ref, as_ref)
    _init_state(mw_ref, aw_ref)

    q_all = jnp.concatenate([q_ref[:, g * HEAD_DIM:(g + 1) * HEAD_DIM] for g in range(NSA_GROUP)], axis=0)

    def scores(k, bias):
        s = _dot_nt(q_all, k).reshape(NSA_GROUP, tq, tk) + bias[None]
        return s.reshape(NSA_GROUP * tq, tk)

    def positions(ki):
        return qi * tq + _iota((tq, tk), 0), ki * tk + _iota((tq, tk), 1)

    def selected(ki, causal):
        rows = _key_rows(ki, tk)
        ok = _dot(sel_ref[0, 0], e_ref[ki]) > 0.5
        if causal:
            t, u = positions(ki)
            ok = ok & (u <= t)
        _flash_step(scores(ks_ref[rows, :], _mask_bias(ok)), vs_ref[rows, :], ms_ref, as_ref)

    def window(ki):
        rows = _key_rows(ki, tk)
        t, u = positions(ki)
        b_win = _mask_bias((u <= t) & (u > t - WINDOW))
        _flash_step(scores(kw_ref[rows, :], b_win), vw_ref[rows, :], mw_ref, aw_ref)

    _for_tiles(0, last, lambda ki: selected(ki, False))
    selected(last, True)
    _for_tiles(first_win, last + 1, window)

    gate = 1.0 / (1.0 + jnp.exp(-gl_ref[...]))
    for g in range(NSA_GROUP):
        o_cmp = oc_ref[:, g * HEAD_DIM:(g + 1) * HEAD_DIM]
        o_slc = _flash_result(as_ref, pl.ds(g * tq, tq))
        o_win = _flash_result(aw_ref, pl.ds(g * tq, tq))
        out = (gate[:, 3 * g:3 * g + 1] * o_cmp + gate[:, 3 * g + 1:3 * g + 2] * o_slc
               + gate[:, 3 * g + 2:3 * g + 3] * o_win)
        o_ref[:, g * HEAD_DIM:(g + 1) * HEAD_DIM] = out.astype(o_ref.dtype)


def _nsa_main(hp, sel, o_cmp, gate_logits, lay, b, s):
    kv = lay['kv']
    tq = min(256, s)
    tk = min(512, s)
    nq, nk = s // tq, s // tk
    assert tk % tq == 0
    gw = NSA_GROUP * HEAD_DIM
    ksb, vsb, kwb, vwb = (lay['off'][n] // HEAD_DIM for n in ('nks', 'nvs', 'nkw', 'nvw'))
    expand = _block_expansion(s, SEL_LEN).reshape(LANES, nk, tk).transpose(1, 0, 2)
    expand = jnp.asarray(expand, MXU_DTYPE)
    row = lambda bi, k, qi: (bi * nq + qi, k)
    return pl.pallas_call(
        functools.partial(_nsa_main_kernel, tq=tq, tk=tk),
        out_shape=jax.ShapeDtypeStruct((b * s, lay['nsa_h'] * HEAD_DIM), MXU_DTYPE),
        grid=(b, kv, nq),
        in_specs=[pl.BlockSpec((tq, gw), row),
                  pl.BlockSpec((s, HEAD_DIM), lambda bi, k, qi: (bi, ksb + k)),
                  pl.BlockSpec((s, HEAD_DIM), lambda bi, k, qi: (bi, vsb + k)),
                  pl.BlockSpec((s, HEAD_DIM), lambda bi, k, qi: (bi, kwb + k)),
                  pl.BlockSpec((s, HEAD_DIM), lambda bi, k, qi: (bi, vwb + k)),
                  pl.BlockSpec((1, 1, tq, LANES), lambda bi, k, qi: (bi, k, qi, 0)),
                  pl.BlockSpec((nk, LANES, tk), lambda bi, k, qi: (0, 0, 0)),
                  pl.BlockSpec((tq, gw), row),
                  pl.BlockSpec((tq, LANES), row)],
        out_specs=pl.BlockSpec((tq, gw), row),
        scratch_shapes=[pltpu.VMEM((NSA_GROUP * tq, LANES), jnp.float32),
                        pltpu.VMEM((NSA_GROUP * tq, HEAD_DIM + LANES), jnp.float32)] * 2,
        compiler_params=_params("parallel", "parallel", "arbitrary"),
    )(hp, hp, hp, hp, hp, sel, expand, o_cmp, gate_logits)


def _layer_norm(y, g, b):
    mu = jnp.mean(y, axis=-1, keepdims=True)
    var = jnp.mean(jnp.square(y - mu), axis=-1, keepdims=True)
    return (y - mu) * lax.rsqrt(var + LN_EPS) * g + b


def _out_proj_kernel(mix_ref, w_ref, x_ref, g_ref, b_ref, wr_ref, br_ref, x1_ref, idx_ref, gate_ref,
                     acc_ref, *, alpha, n_experts):
    k = pl.program_id(1)

    @pl.when(k == 0)
    def _():
        acc_ref[...] = jnp.zeros(acc_ref.shape, jnp.float32)

    acc_ref[...] += _dot(mix_ref[...], w_ref[...])

    @pl.when(k == pl.num_programs(1) - 1)
    def _():
        x1 = _layer_norm(alpha * x_ref[...] + acc_ref[...], g_ref[...], b_ref[...])
        x1_ref[...] = x1
        logits = _dot(x1.astype(MXU_DTYPE), wr_ref[...]) + br_ref[...]
        lane = _lane_index(logits.shape)
        work = jnp.where(lane < n_experts, logits, -jnp.inf)
        idx_out = jnp.zeros(logits.shape, jnp.float32)
        val_out = jnp.zeros(logits.shape, jnp.float32)
        top = None
        for r in range(TOP_K):
            mx = jnp.max(work, axis=-1, keepdims=True)
            first = _first_lane(work == mx, lane)
            top = mx if top is None else top
            idx_out = jnp.where(lane == r, first, idx_out)
            val_out = jnp.where(lane == r, jnp.exp(mx - top), val_out)
            work = jnp.where(lane == first, -jnp.inf, work)
        idx_ref[...] = idx_out.astype(jnp.int32)
        gate_ref[...] = val_out / jnp.sum(val_out, axis=-1, keepdims=True)


def _out_proj_ln_router(mix, w_out, x, ln_g, ln_b, w_router, b_router, alpha):
    t, kdim = mix.shape
    d = w_out.shape[1]
    n_experts = w_router.shape[1]
    tm = min(256, t)
    tk = min(512, kdim)
    wr = jnp.pad(w_router, ((0, 0), (0, LANES - n_experts))).astype(MXU_DTYPE)
    br = jnp.pad(b_router.astype(jnp.float32), (0, LANES - n_experts)).reshape(1, LANES)
    row = lambda i, k: (i, 0)
    const = lambda i, k: (0, 0)
    return pl.pallas_call(
        functools.partial(_out_proj_kernel, alpha=alpha, n_experts=n_experts),
        out_shape=(jax.ShapeDtypeStruct((t, d), jnp.float32),
                   jax.ShapeDtypeStruct((t, LANES), jnp.int32),
                   jax.ShapeDtypeStruct((t, LANES), jnp.float32)),
        grid=(t // tm, kdim // tk),
        in_specs=[pl.BlockSpec((tm, tk), lambda i, k: (i, k)),
                  pl.BlockSpec((tk, d), lambda i, k: (k, 0)),
                  pl.BlockSpec((tm, d), row),
                  pl.BlockSpec((1, d), const), pl.BlockSpec((1, d), const),
                  pl.BlockSpec((d, LANES), const), pl.BlockSpec((1, LANES), const)],
        out_specs=(pl.BlockSpec((tm, d), row), pl.BlockSpec((tm, LANES), row),
                   pl.BlockSpec((tm, LANES), row)),
        scratch_shapes=[pltpu.VMEM((tm, d), jnp.float32)],
        compiler_params=_params("parallel", "arbitrary"),
    )(mix, w_out, x, ln_g.reshape(1, d), ln_b.reshape(1, d), wr, br)


MOE_TILE = 256
COMBINE_TILE = 128
GATHER_UNROLL = 8


GLU_GROUP = 2 * LANES


def _regroup_kernel(w_ref, p_ref, o_ref):
    w = w_ref[0].astype(MXU_DTYPE)
    for c in range(w.shape[1] // GLU_GROUP):
        cols = slice(c * GLU_GROUP, (c + 1) * GLU_GROUP)
        o_ref[0, :, cols] = _dot(w[:, cols], p_ref[...]).astype(o_ref.dtype)


def _regroup_gate_up(w_gate_up):
    e, d, f2 = w_gate_up.shape
    j = np.arange(GLU_GROUP)
    src = np.where(j < LANES, 2 * j, 2 * (j - LANES) + 1)
    perm = np.zeros((GLU_GROUP, GLU_GROUP), np.float32)
    perm[src, j] = 1.0
    tk = min(1024, d)
    return pl.pallas_call(
        _regroup_kernel,
        out_shape=jax.ShapeDtypeStruct((e, d, f2), MXU_DTYPE),
        grid=(e, d // tk),
        in_specs=[pl.BlockSpec((1, tk, f2), lambda ei, ki: (ei, ki, 0)),
                  pl.BlockSpec((GLU_GROUP, GLU_GROUP), lambda ei, ki: (0, 0))],
        out_specs=pl.BlockSpec((1, tk, f2), lambda ei, ki: (ei, ki, 0)),
        compiler_params=_params("parallel", "parallel"),
    )(w_gate_up, jnp.asarray(perm, MXU_DTYPE))


def _regroup_bias(b_gate_up):
    e, f2 = b_gate_up.shape
    b = b_gate_up.astype(jnp.float32).reshape(e, f2 // GLU_GROUP, LANES, 2)
    return b.transpose(0, 1, 3, 2).reshape(e, 1, f2)


def _moe_kernel(rt_ref, te_ref, nu_ref, x_hbm, wgu_ref, bgu_ref, wd_ref, bd_ref, o_ref, xbuf, sem):
    i = pl.program_id(0)
    n_used = nu_ref[0]

    def row_copy(tile, slot, r):
        tok = rt_ref[tile * MOE_TILE + r]
        return pltpu.make_async_copy(x_hbm.at[pl.ds(tok, 1)], xbuf.at[slot, pl.ds(r, 1)], sem.at[slot])

    def gather(tile, slot):
        def body(r, carry):
            row_copy(tile, slot, r).start()
            return carry
        lax.fori_loop(0, MOE_TILE, body, 0, unroll=GATHER_UNROLL)

    @pl.when(i == 0)
    def _():
        gather(0, 0)

    @pl.when(i + 1 < n_used)
    def _():
        gather(i + 1, (i + 1) % 2)

    @pl.when(i < n_used)
    def _():
        slot = i % 2
        pltpu.make_async_copy(x_hbm.at[pl.ds(0, MOE_TILE)], xbuf.at[slot], sem.at[slot]).wait()
        x = xbuf[slot].astype(MXU_DTYPE)
        hid = _dot(x, wgu_ref[0]) + bgu_ref[0]
        acts = []
        for c in range(hid.shape[1] // GLU_GROUP):
            h_glu = jnp.minimum(hid[:, c * GLU_GROUP:c * GLU_GROUP + LANES], SWIGLU_LIMIT)
            h_lin = jnp.clip(hid[:, c * GLU_GROUP + LANES:(c + 1) * GLU_GROUP], -SWIGLU_LIMIT, SWIGLU_LIMIT)
            acts.append(h_glu * (1.0 / (1.0 + jnp.exp(-SWIGLU_ALPHA * h_glu))) * (h_lin + 1.0))
        act = jnp.concatenate(acts, axis=1)
        o_ref[...] = _dot(act.astype(MXU_DTYPE), wd_ref[0]) + bd_ref[0]

    @pl.when(i >= n_used)
    def _():
        o_ref[...] = jnp.zeros(o_ref.shape, o_ref.dtype)


def _moe_experts(x1, row_token, tile_expert, n_used, wgu, bgu, wd, bd):
    t, d = x1.shape
    f2 = wgu.shape[2]
    n_tiles = tile_expert.shape[0]
    wmap = lambda i, rt, te, nu: (te[i], 0, 0)
    grid_spec = pltpu.PrefetchScalarGridSpec(
        num_scalar_prefetch=3,
        grid=(n_tiles,),
        in_specs=[pl.BlockSpec(memory_space=pl.ANY),
                  pl.BlockSpec((1, d, f2), wmap), pl.BlockSpec((1, 1, f2), wmap),
                  pl.BlockSpec((1, f2 // 2, d), wmap), pl.BlockSpec((1, 1, d), wmap)],
        out_specs=pl.BlockSpec((MOE_TILE, d), lambda i, rt, te, nu: (i, 0)),
        scratch_shapes=[pltpu.VMEM((2, MOE_TILE, d), jnp.float32), pltpu.SemaphoreType.DMA((2,))])
    return pl.pallas_call(
        _moe_kernel,
        out_shape=jax.ShapeDtypeStruct((n_tiles * MOE_TILE, d), jnp.float32),
        grid_spec=grid_spec, compiler_params=_params("arbitrary"),
    )(row_token, tile_expert, n_used, x1, wgu, bgu, wd, bd)


def _combine_kernel(pos_ref, y_hbm, x_ref, gate_ref, g_ref, b_ref, o_ref, ob_ref, ybuf, sem, *, alpha):
    i = pl.program_id(0)
    n_rows = COMBINE_TILE * TOP_K

    def gather(tile, slot):
        def body(r, carry):
            src = pos_ref[tile * n_rows + r]
            pltpu.make_async_copy(y_hbm.at[pl.ds(src, 1)], ybuf.at[slot, pl.ds(r, 1)], sem.at[slot]).start()
            return carry
        lax.fori_loop(0, n_rows, body, 0, unroll=GATHER_UNROLL)

    @pl.when(i == 0)
    def _():
        gather(0, 0)

    @pl.when(i + 1 < pl.num_programs(0))
    def _():
        gather(i + 1, (i + 1) % 2)

    slot = i % 2
    pltpu.make_async_copy(y_hbm.at[pl.ds(0, n_rows)], ybuf.at[slot], sem.at[slot]).wait()
    gate = gate_ref[...]
    ffn = jnp.zeros(x_ref.shape, jnp.float32)
    for k in range(TOP_K):
        ffn = ffn + gate[:, k:k + 1] * ybuf[slot, pl.ds(k * COMBINE_TILE, COMBINE_TILE)]
    x2 = _layer_norm(alpha * x_ref[...] + ffn, g_ref[...], b_ref[...])
    o_ref[...] = x2
    ob_ref[...] = x2.astype(ob_ref.dtype)


def _combine_ln(y, pos, x1, gate, ln_g, ln_b, alpha):
    t, d = x1.shape
    row = lambda i, p: (i, 0)
    const = lambda i, p: (0, 0)
    grid_spec = pltpu.PrefetchScalarGridSpec(
        num_scalar_prefetch=1,
        grid=(t // COMBINE_TILE,),
        in_specs=[pl.BlockSpec(memory_space=pl.ANY),
                  pl.BlockSpec((COMBINE_TILE, d), row), pl.BlockSpec((COMBINE_TILE, LANES), row),
                  pl.BlockSpec((1, d), const), pl.BlockSpec((1, d), const)],
        out_specs=(pl.BlockSpec((COMBINE_TILE, d), row), pl.BlockSpec((COMBINE_TILE, d), row)),
        scratch_shapes=[pltpu.VMEM((2, COMBINE_TILE * TOP_K, d), jnp.float32),
                        pltpu.SemaphoreType.DMA((2,))])
    return pl.pallas_call(
        functools.partial(_combine_kernel, alpha=alpha),
        out_shape=(jax.ShapeDtypeStruct((t, d), jnp.float32), jax.ShapeDtypeStruct((t, d), MXU_DTYPE)),
        grid_spec=grid_spec, compiler_params=_params("arbitrary"),
    )(pos, y, x1, gate, ln_g.reshape(1, d), ln_b.reshape(1, d))


def _route(top_idx, n_experts):
    t = top_idx.shape[0]
    n_assign = t * TOP_K
    n_tiles = n_assign // MOE_TILE + n_experts
    expert = top_idx.reshape(n_assign)
    onehot = (expert[:, None] == jnp.arange(n_experts, dtype=jnp.int32)[None, :]).astype(jnp.int32)
    running = jnp.cumsum(onehot, axis=0)
    counts = running[-1]
    rank = jnp.take_along_axis(running, expert[:, None], axis=1)[:, 0] - 1
    padded = (counts + MOE_TILE - 1) // MOE_TILE * MOE_TILE
    pad_end = jnp.cumsum(padded)
    dest = (pad_end - padded)[expert] + rank
    slot = dest.reshape(t // COMBINE_TILE, COMBINE_TILE, TOP_K).transpose(0, 2, 1).reshape(n_assign)
    token = jnp.arange(n_assign, dtype=jnp.int32) // TOP_K
    row_token = jnp.zeros((n_tiles * MOE_TILE,), jnp.int32).at[dest].set(token)
    tile_start = jnp.arange(n_tiles, dtype=jnp.int32) * MOE_TILE
    tile_expert = jnp.minimum(
        jnp.sum((pad_end[None, :] <= tile_start[:, None]).astype(jnp.int32), axis=1), n_experts - 1)
    n_used = (pad_end[-1:] // MOE_TILE).astype(jnp.int32)
    return row_token, tile_expert, n_used, slot.astype(jnp.int32)


def _rope_tables(positions):
    inv_freq = 1.0 / (ROPE_THETA ** (jnp.arange(0, ROT_DIM, 2, dtype=jnp.float32) / ROT_DIM))
    ang = positions.astype(jnp.float32).reshape(-1)[:, None] * inv_freq
    cos, sin = jnp.cos(ang), jnp.sin(ang)
    rest = HEAD_DIM - ROT_DIM
    n = ang.shape[0]
    ct = jnp.concatenate([cos, cos, jnp.ones((n, rest), jnp.float32)], axis=1)
    sa = jnp.concatenate([jnp.zeros_like(sin), sin, jnp.zeros((n, rest), jnp.float32)], axis=1)
    sb = jnp.concatenate([-sin, jnp.zeros_like(sin), jnp.zeros((n, rest), jnp.float32)], axis=1)
    return ct, sa, sb


def _gate_weight(w_in_l, lay):
    d = w_in_l.shape[0]
    kv = lay['kv']
    o = lay['ref_off']['ngate']
    wg = w_in_l[:, o:o + lay['size']['ngate']].reshape(d, kv, NSA_GROUP * 3)
    wg = jnp.pad(wg, ((0, 0), (0, 0), (0, LANES - NSA_GROUP * 3)))
    return wg.reshape(d, kv * LANES).astype(MXU_DTYPE)


def kernel(x, positions, w_in, nsa_cmp_pos, nsa_cmp_w1, nsa_cmp_w2, diff_lambda, diff_subln_g, w_out,
           ln1_g, ln1_b, w_router, b_router, w_gate_up, b_gate_up, w_down, b_down, ln2_g, ln2_b):
    b, s, d = x.shape
    depth = w_in.shape[0]
    n_experts = w_router.shape[2]
    lay = _layout(d)
    alpha = (2 * depth) ** 0.25
    ct, sa, sb = _rope_tables(positions)
    xf = x.reshape(b * s, d).astype(jnp.float32)
    xb = xf.astype(MXU_DTYPE)
    for layer in range(depth):
        w_l = w_in[layer]
        w_perm = jnp.concatenate(
            [w_l[:, lay['ref_off'][n]:lay['ref_off'][n] + lay['size'][n]] for n in ROPED + PLAIN],
            axis=1).astype(MXU_DTYPE)
        hp = _project(xb, w_perm, ct, sa, sb, lay['n_roped'])
        gate_logits = _gate_logits(xb, _gate_weight(w_l, lay))

        kcv = _nsa_compress(hp, nsa_cmp_pos[layer], nsa_cmp_w1[layer], nsa_cmp_w2[layer], lay, b, s)
        o_cmp, sel = _nsa_cmp_select(hp, kcv, lay, b, s)
        y_nsa = _nsa_main(hp, sel, o_cmp, gate_logits, lay, b, s)
        lambda_init = 0.8 - 0.6 * math.exp(-0.3 * layer)
        y_diff = _diff_attention(hp, diff_lambda[layer], diff_subln_g[layer], lay, b, s, lambda_init)
        y_moba = _moba_attention(hp, lay, b, s)
        mix = jnp.concatenate([y_nsa, y_diff, y_moba], axis=1)

        x1, top_idx, top_gate = _out_proj_ln_router(
            mix, w_out[layer].astype(MXU_DTYPE), xf, ln1_g[layer], ln1_b[layer],
            w_router[layer], b_router[layer], alpha)

        row_token, tile_expert, n_used, slot = _route(top_idx[:, :TOP_K], n_experts)
        y = _moe_experts(
            x1, row_token, tile_expert, n_used,
            _regroup_gate_up(w_gate_up[layer]), _regroup_bias(b_gate_up[layer]),
            w_down[layer].astype(MXU_DTYPE), b_down[layer].astype(jnp.float32).reshape(n_experts, 1, d))
        xf, xb = _combine_ln(y, slot, x1, top_gate, ln2_g[layer], ln2_b[layer], alpha)
    return xf.reshape(b, s, d).astype(x.dtype)
```

```python
import functools
import math

import numpy as np
import jax
import jax.numpy as jnp
from jax import lax
from jax.experimental import pallas as pl
from jax.experimental.pallas import tpu as pltpu

HEAD_DIM = 128
ROT_DIM = HEAD_DIM // 4
ROT_HALF = ROT_DIM // 2
ROPE_THETA = 500000.0
ATTN_SCALE = HEAD_DIM ** -0.5
NEG_INF = -1e30
LN_EPS = 1e-5

NSA_GROUP = 4
CMP_LEN = 32
CMP_STRIDE = 16
SEL_LEN = 64
SEL_TOPN = 16
SEL_FORCED = 3
WINDOW = 512
CMP_PER_SEL = SEL_LEN // CMP_STRIDE

DIFF_VDIM = 2 * HEAD_DIM
MOBA_BLOCK = 256
MOBA_TOPK = 3

TOP_K = 4
SWIGLU_LIMIT = 7.0
SWIGLU_ALPHA = 1.702

LANES = 128
MXU_DTYPE = jnp.bfloat16
VMEM_LIMIT = 56 * 1024 * 1024

ROPED = ('nq', 'nkc', 'nks', 'nkw', 'dq', 'dk', 'mq', 'mk')
PLAIN = ('nvc', 'nvs', 'nvw', 'dv', 'mv')
REF_ORDER = ('nq', 'nkc', 'nvc', 'nks', 'nvs', 'nkw', 'nvw', 'ngate', 'dq', 'dk', 'dv', 'mq', 'mk', 'mv')


def _layout(d):
    nsa_h = d // 256
    kv = nsa_h // NSA_GROUP
    diff_h = d // 1024
    moba_h = d // 512
    size = dict(nq=nsa_h * HEAD_DIM, nkc=kv * HEAD_DIM, nvc=kv * HEAD_DIM, nks=kv * HEAD_DIM,
                nvs=kv * HEAD_DIM, nkw=kv * HEAD_DIM, nvw=kv * HEAD_DIM, ngate=3 * nsa_h,
                dq=2 * diff_h * HEAD_DIM, dk=2 * diff_h * HEAD_DIM, dv=diff_h * DIFF_VDIM,
                mq=moba_h * HEAD_DIM, mk=moba_h * HEAD_DIM, mv=moba_h * HEAD_DIM)
    ref_off, o = {}, 0
    for n in REF_ORDER:
        ref_off[n] = o
        o += size[n]
    off, o = {}, 0
    for n in ROPED + PLAIN:
        off[n] = o
        o += size[n]
    n_roped = sum(size[n] for n in ROPED)
    return dict(nsa_h=nsa_h, kv=kv, diff_h=diff_h, moba_h=moba_h, size=size, ref_off=ref_off,
                off=off, n_roped=n_roped, n_cols=o)


def _params(*sem):
    return pltpu.CompilerParams(dimension_semantics=sem, vmem_limit_bytes=VMEM_LIMIT)


def _dot(a, b):
    return jnp.dot(a, b, preferred_element_type=jnp.float32)


def _dot_nt(a, b):
    return lax.dot_general(a, b, (((1,), (1,)), ((), ())), preferred_element_type=jnp.float32)


def _iota(shape, dim):
    return lax.broadcasted_iota(jnp.int32, shape, dim)


def _lane_index(shape):
    return _iota(shape, 1).astype(jnp.float32)


def _first_lane(hit, lane):
    return jnp.min(jnp.where(hit, lane, float(LANES)), axis=-1, keepdims=True)


def _proj_kernel(x_ref, w_ref, ct_ref, sa_ref, sb_ref, o_ref, *, n_rope_blocks, tn):
    j = pl.program_id(1)
    acc = _dot(x_ref[...], w_ref[...])

    @pl.when(j < n_rope_blocks)
    def _():
        ct, sa, sb = ct_ref[...], sa_ref[...], sb_ref[...]
        for c in range(tn // HEAD_DIM):
            a = acc[:, c * HEAD_DIM:(c + 1) * HEAD_DIM]
            r = (a * ct + pltpu.roll(a, ROT_HALF, 1) * sa
                 + pltpu.roll(a, HEAD_DIM - ROT_HALF, 1) * sb)
            o_ref[:, c * HEAD_DIM:(c + 1) * HEAD_DIM] = r.astype(o_ref.dtype)

    @pl.when(j >= n_rope_blocks)
    def _():
        o_ref[...] = acc.astype(o_ref.dtype)


def _project(xb, w, ct, sa, sb, n_roped):
    m, k = xb.shape
    n = w.shape[1]
    tm = min(1024, m)
    tn = next(t for t in (512, 256, 128) if n % t == 0 and n_roped % t == 0)
    kern = functools.partial(_proj_kernel, n_rope_blocks=n_roped // tn, tn=tn)
    return pl.pallas_call(
        kern,
        out_shape=jax.ShapeDtypeStruct((m, n), MXU_DTYPE),
        grid=(m // tm, n // tn),
        in_specs=[pl.BlockSpec((tm, k), lambda i, j: (i, 0)),
                  pl.BlockSpec((k, tn), lambda i, j: (0, j)),
                  pl.BlockSpec((tm, HEAD_DIM), lambda i, j: (i, 0)),
                  pl.BlockSpec((tm, HEAD_DIM), lambda i, j: (i, 0)),
                  pl.BlockSpec((tm, HEAD_DIM), lambda i, j: (i, 0))],
        out_specs=pl.BlockSpec((tm, tn), lambda i, j: (i, j)),
        compiler_params=_params("parallel", "arbitrary"),
    )(xb, w, ct, sa, sb)


def _gate_kernel(x_ref, w_ref, o_ref):
    o_ref[...] = _dot(x_ref[...], w_ref[...])


def _gate_logits(xb, wg):
    m, k = xb.shape
    n = wg.shape[1]
    tm = min(1024, m)
    return pl.pallas_call(
        _gate_kernel,
        out_shape=jax.ShapeDtypeStruct((m, n), jnp.float32),
        grid=(m // tm,),
        in_specs=[pl.BlockSpec((tm, k), lambda i: (i, 0)),
                  pl.BlockSpec((k, n), lambda i: (0, 0))],
        out_specs=pl.BlockSpec((tm, n), lambda i: (i, 0)),
        compiler_params=_params("parallel"),
    )(xb, wg)


EXP2_SCALE = ATTN_SCALE * math.log2(math.e)


def _mask_bias(mask):
    return jnp.where(mask, 0.0, -jnp.inf)


def _lanes(x, n):
    return x if n == LANES else jnp.concatenate([x] * (n // LANES), axis=1)


def _flash_step(s, v, m_ref, acc_ref):
    dv = v.shape[1]
    m_old = m_ref[...]
    m_new = jnp.maximum(m_old, jnp.max(s, axis=-1, keepdims=True))
    p = jnp.exp2((s - _lanes(m_new, s.shape[1])) * EXP2_SCALE)
    alpha = jnp.exp2((m_old - m_new) * EXP2_SCALE)
    v_ones = jnp.concatenate([v, jnp.ones((v.shape[0], LANES), v.dtype)], axis=1)
    acc_ref[...] = _lanes(alpha, dv + LANES) * acc_ref[...] + _dot(p.astype(v.dtype), v_ones)
    m_ref[...] = m_new


def _flash_result(acc_ref, rows):
    dv = acc_ref.shape[1] - LANES
    denom = jnp.maximum(acc_ref[rows, dv:], 1e-30)
    return acc_ref[rows, :dv] / _lanes(denom, dv)


def _init_state(m_ref, acc_ref):
    m_ref[...] = jnp.full(m_ref.shape, NEG_INF, jnp.float32)
    acc_ref[...] = jnp.zeros(acc_ref.shape, jnp.float32)


def _key_rows(ki, tk):
    return pl.ds(pl.multiple_of(ki * tk, tk), tk)


def _for_tiles(lo, hi, body):
    def wrapped(ki, carry):
        body(ki)
        return carry
    lax.fori_loop(lo, hi, wrapped, 0)


def _diff_kernel(lam_ref, g_ref, q_ref, k_ref, v_ref, o_ref, m_ref, acc_ref, *, tq, lambda_init):
    qi = pl.program_id(2)
    tk = tq
    _init_state(m_ref, acc_ref)

    def tile(ki, masked):
        rows = _key_rows(ki, tk)
        maps = []
        for mp in range(2):
            q = q_ref[:, mp * HEAD_DIM:(mp + 1) * HEAD_DIM]
            maps.append(_dot_nt(q, k_ref[rows, mp * HEAD_DIM:(mp + 1) * HEAD_DIM]))
        if masked:
            bias = _mask_bias(_iota((tq, tk), 1) <= _iota((tq, tk), 0))
            maps = [s + bias for s in maps]
        _flash_step(jnp.concatenate(maps, axis=0), v_ref[rows, :], m_ref, acc_ref)

    _for_tiles(0, qi, lambda ki: tile(ki, False))
    tile(qi, True)

    lv = lam_ref[...]
    lam = (jnp.exp(jnp.sum(lv[0:1] * lv[1:2], axis=-1, keepdims=True))
           - jnp.exp(jnp.sum(lv[2:3] * lv[3:4], axis=-1, keepdims=True)) + lambda_init)
    o = _flash_result(acc_ref, pl.ds(0, tq)) - lam * _flash_result(acc_ref, pl.ds(tq, tq))
    o = o * lax.rsqrt(jnp.mean(jnp.square(o), axis=-1, keepdims=True) + LN_EPS) * g_ref[...]
    o_ref[...] = (o * (1.0 - lambda_init)).astype(o_ref.dtype)


def _diff_attention(hp, lam_vecs, subln_g, lay, b, s, lambda_init):
    hd = lay['diff_h']
    tq = min(512, s)
    nq = s // tq
    qb, kb, vb = (lay['off'][n] // DIFF_VDIM for n in ('dq', 'dk', 'dv'))
    assert all(lay['off'][n] % DIFF_VDIM == 0 for n in ('dq', 'dk', 'dv'))
    kern = functools.partial(_diff_kernel, tq=tq, lambda_init=lambda_init)
    return pl.pallas_call(
        kern, out_shape=jax.ShapeDtypeStruct((b * s, hd * DIFF_VDIM), MXU_DTYPE),
        grid=(b, hd, nq),
        in_specs=[pl.BlockSpec((4, HEAD_DIM), lambda bi, h, qi: (0, 0)),
                  pl.BlockSpec((1, DIFF_VDIM), lambda bi, h, qi: (0, 0)),
                  pl.BlockSpec((tq, DIFF_VDIM), lambda bi, h, qi: (bi * nq + qi, qb + h)),
                  pl.BlockSpec((s, DIFF_VDIM), lambda bi, h, qi: (bi, kb + h)),
                  pl.BlockSpec((s, DIFF_VDIM), lambda bi, h, qi: (bi, vb + h))],
        out_specs=pl.BlockSpec((tq, DIFF_VDIM), lambda bi, h, qi: (bi * nq + qi, h)),
        scratch_shapes=[pltpu.VMEM((2 * tq, LANES), jnp.float32),
                        pltpu.VMEM((2 * tq, DIFF_VDIM + LANES), jnp.float32)],
        compiler_params=_params("parallel", "parallel", "arbitrary"),
    )(lam_vecs.astype(jnp.float32), subln_g.reshape(1, DIFF_VDIM).astype(jnp.float32), hp, hp, hp)


def _kmean_kernel(k_ref, o_ref, *, n_blk):
    k = k_ref[...].astype(jnp.float32).reshape(n_blk, MOBA_BLOCK, HEAD_DIM)
    o_ref[0, 0] = jnp.mean(k, axis=1)


def _moba_kmean(hp, lay, b, s):
    h = lay['moba_h']
    n_blk = s // MOBA_BLOCK
    kb = lay['off']['mk'] // HEAD_DIM
    return pl.pallas_call(
        functools.partial(_kmean_kernel, n_blk=n_blk),
        out_shape=jax.ShapeDtypeStruct((b, h, n_blk, HEAD_DIM), jnp.float32),
        grid=(b, h),
        in_specs=[pl.BlockSpec((s, HEAD_DIM), lambda bi, hi: (bi, kb + hi))],
        out_specs=pl.BlockSpec((1, 1, n_blk, HEAD_DIM), lambda bi, hi: (bi, hi, 0, 0)),
        compiler_params=_params("parallel", "parallel"),
    )(hp)


def _block_expansion(s, block):
    return (np.arange(s)[None, :] // block == np.arange(LANES)[:, None]).astype(np.float32)


def _moba_kernel(q_ref, k_ref, v_ref, km_ref, e_ref, o_ref, sel_ref, m_ref, acc_ref, *, tq, tk):
    qi = pl.program_id(2)
    q = q_ref[...]
    _init_state(m_ref, acc_ref)
    score = _dot_nt(q, km_ref[0, 0].astype(q.dtype))
    blk = _lane_index((tq, LANES))
    own = ((qi * tq + _iota((tq, LANES), 0)) // MOBA_BLOCK).astype(jnp.float32)
    work = jnp.where(blk < own, score, -jnp.inf)
    sel = jnp.where(blk == own, 1.0, 0.0)
    for r in range(MOBA_TOPK):
        mx = jnp.max(work, axis=-1, keepdims=True)
        first = _first_lane(work == mx, blk)
        pick = blk == first
        sel = jnp.where(pick & (own > r), 1.0, sel)
        work = jnp.where(pick, -jnp.inf, work)
    sel_ref[...] = sel.astype(sel_ref.dtype)

    def tile(ki, causal):
        rows = _key_rows(ki, tk)
        ok = _dot(sel_ref[...], e_ref[ki]) > 0.5
        if causal:
            t = qi * tq + _iota((tq, tk), 0)
            u = ki * tk + _iota((tq, tk), 1)
            ok = ok & (u <= t)
        _flash_step(_dot_nt(q, k_ref[rows, :]) + _mask_bias(ok), v_ref[rows, :], m_ref, acc_ref)

    past = qi * (tq // tk)
    _for_tiles(0, past, lambda ki: tile(ki, False))
    for d in range(tq // tk):
        tile(past + d, True)
    o_ref[...] = _flash_result(acc_ref, pl.ds(0, tq)).astype(o_ref.dtype)


def _moba_attention(hp, lay, b, s):
    h = lay['moba_h']
    n_blk = s // MOBA_BLOCK
    assert s % MOBA_BLOCK == 0 and MOBA_TOPK <= n_blk <= LANES
    km = _moba_kmean(hp, lay, b, s)
    km = jnp.pad(km, ((0, 0), (0, 0), (0, LANES - n_blk), (0, 0)))
    tq = min(1024, s)
    tk = min(512, s)
    nq = s // tq
    assert tq % tk == 0 and tk % MOBA_BLOCK == 0
    qb, kb, vb = (lay['off'][n] // HEAD_DIM for n in ('mq', 'mk', 'mv'))
    expand = _block_expansion(s, MOBA_BLOCK).reshape(LANES, s // tk, tk).transpose(1, 0, 2)
    expand = jnp.asarray(expand, MXU_DTYPE)
    return pl.pallas_call(
        functools.partial(_moba_kernel, tq=tq, tk=tk),
        out_shape=jax.ShapeDtypeStruct((b * s, h * HEAD_DIM), MXU_DTYPE),
        grid=(b, h, nq),
        in_specs=[pl.BlockSpec((tq, HEAD_DIM), lambda bi, hi, qi: (bi * nq + qi, qb + hi)),
                  pl.BlockSpec((s, HEAD_DIM), lambda bi, hi, qi: (bi, kb + hi)),
                  pl.BlockSpec((s, HEAD_DIM), lambda bi, hi, qi: (bi, vb + hi)),
                  pl.BlockSpec((1, 1, LANES, HEAD_DIM), lambda bi, hi, qi: (bi, hi, 0, 0)),
                  pl.BlockSpec((s // tk, LANES, tk), lambda bi, hi, qi: (0, 0, 0))],
        out_specs=pl.BlockSpec((tq, HEAD_DIM), lambda bi, hi, qi: (bi * nq + qi, hi)),
        scratch_shapes=[pltpu.VMEM((tq, LANES), MXU_DTYPE), pltpu.VMEM((tq, LANES), jnp.float32),
                        pltpu.VMEM((tq, HEAD_DIM + LANES), jnp.float32)],
        compiler_params=_params("parallel", "parallel", "arbitrary"),
    )(hp, hp, hp, km, expand)


def _gelu_tanh(x):
    return 0.5 * x * (1.0 + jnp.tanh(math.sqrt(2.0 / math.pi) * (x + 0.044715 * (x * x * x))))


def _compress_kernel(seg_ref, plo_ref, phi_ref, w1lo_ref, w1hi_ref, w2_ref, o_ref, *, n_seg):
    seg = seg_ref[0, 0, 0].astype(jnp.float32)
    lo = _dot((seg + plo_ref[0]).astype(MXU_DTYPE), w1lo_ref[0])
    hi = _dot((seg + phi_ref[0]).astype(MXU_DTYPE), w1hi_ref[0])
    pre = lo + pltpu.roll(hi, n_seg - 1, 0)
    o_ref[0, 0, 0] = _dot(_gelu_tanh(pre).astype(MXU_DTYPE), w2_ref[0]).astype(o_ref.dtype)


def _nsa_compress(hp, cmp_pos, cmp_w1, cmp_w2, lay, b, s):
    kv = lay['kv']
    n_seg = s // CMP_STRIDE
    half = CMP_STRIDE * HEAD_DIM

    def segments(name):
        o = lay['off'][name]
        t = hp[:, o:o + kv * HEAD_DIM].reshape(b, n_seg, CMP_STRIDE, kv, HEAD_DIM)
        return t.transpose(0, 3, 1, 2, 4).reshape(b, kv, n_seg, half)

    seg = jnp.stack([segments('nkc'), segments('nvc')], axis=1)
    pos = cmp_pos.astype(jnp.float32).reshape(2, CMP_LEN * HEAD_DIM)
    plo = pos[:, :half].reshape(2, 1, half)
    phi = pos[:, half:].reshape(2, 1, half)
    w1 = cmp_w1.astype(MXU_DTYPE)
    w1lo, w1hi = w1[:, :half], w1[:, half:]
    w2 = cmp_w2.astype(MXU_DTYPE)
    return pl.pallas_call(
        functools.partial(_compress_kernel, n_seg=n_seg),
        out_shape=jax.ShapeDtypeStruct((b, 2, kv, n_seg, HEAD_DIM), MXU_DTYPE),
        grid=(b, 2, kv),
        in_specs=[pl.BlockSpec((1, 1, 1, n_seg, half), lambda bi, c, k: (bi, c, k, 0, 0)),
                  pl.BlockSpec((1, 1, half), lambda bi, c, k: (c, 0, 0)),
                  pl.BlockSpec((1, 1, half), lambda bi, c, k: (c, 0, 0)),
                  pl.BlockSpec((1, half, HEAD_DIM), lambda bi, c, k: (c, 0, 0)),
                  pl.BlockSpec((1, half, HEAD_DIM), lambda bi, c, k: (c, 0, 0)),
                  pl.BlockSpec((1, HEAD_DIM, HEAD_DIM), lambda bi, c, k: (c, 0, 0))],
        out_specs=pl.BlockSpec((1, 1, 1, n_seg, HEAD_DIM), lambda bi, c, k: (bi, c, k, 0, 0)),
        compiler_params=_params("parallel", "parallel", "parallel"),
    )(seg, plo, phi, w1lo, w1hi, w2)


def _nsa_cmp_kernel(q_ref, kc_ref, vc_ref, o_ref, sel_ref, *, tq, n_sel, n_cmp):
    qi = pl.program_id(2)
    width = CMP_PER_SEL * LANES
    t = qi * tq + _iota((tq, width), 0)
    pos = _iota((tq, width), 1)
    m_idx = pos % LANES
    n_idx = CMP_PER_SEL * m_idx + pos // LANES
    valid = (m_idx < n_sel) & (n_idx < n_cmp) & (n_idx * CMP_STRIDE + CMP_LEN - 1 <= t)
    kc = kc_ref[0, 0, 0]
    vc = vc_ref[0, 0, 0]
    p_grp = jnp.zeros((tq, width), jnp.float32)
    for g in range(NSA_GROUP):
        q = q_ref[:, g * HEAD_DIM:(g + 1) * HEAD_DIM]
        s = jnp.where(valid, _dot_nt(q, kc) * ATTN_SCALE, NEG_INF)
        p = jnp.where(valid, jnp.exp(s - jnp.max(s, axis=-1, keepdims=True)), 0.0)
        p = p / jnp.maximum(jnp.sum(p, axis=-1, keepdims=True), 1e-30)
        o_ref[:, g * HEAD_DIM:(g + 1) * HEAD_DIM] = _dot(p.astype(vc.dtype), vc)
        p_grp = p_grp + p

    slabs = [p_grp[:, r * LANES:(r + 1) * LANES] for r in range(CMP_PER_SEL)]
    blk = _lane_index((tq, LANES))
    prev = jnp.where(blk == 0, 0.0, pltpu.roll(slabs[CMP_PER_SEL - 1], 1, 1))
    imp = prev
    for r in range(CMP_PER_SEL):
        imp = imp + slabs[r]
    cur = ((qi * tq + _iota((tq, LANES), 0)) // SEL_LEN).astype(jnp.float32)
    forced = (blk == 0) | (blk == cur) | (blk == cur - 1)
    work = jnp.where(forced | (blk > cur), -jnp.inf, imp)
    sel = jnp.where(forced, 1.0, 0.0)
    for _ in range(SEL_TOPN - SEL_FORCED):
        mx = jnp.max(work, axis=-1, keepdims=True)
        first = _first_lane(work == mx, blk)
        pick = blk == first
        sel = jnp.where(pick, 1.0, sel)
        work = jnp.where(pick, -jnp.inf, work)
    sel_ref[0, 0] = jnp.where(blk <= cur, sel, 0.0).astype(sel_ref.dtype)


def _nsa_cmp_select(hp, kcv, lay, b, s):
    kv = lay['kv']
    n_seg = s // CMP_STRIDE
    n_sel = s // SEL_LEN
    n_cmp = (s - CMP_LEN) // CMP_STRIDE + 1
    assert SEL_TOPN <= n_sel <= LANES
    width = CMP_PER_SEL * LANES
    kcp = kcv.reshape(b, 2, kv, n_sel, CMP_PER_SEL, HEAD_DIM).transpose(0, 1, 2, 4, 3, 5)
    kcp = jnp.pad(kcp, ((0, 0),) * 4 + ((0, LANES - n_sel), (0, 0))).reshape(b, 2, kv, width, HEAD_DIM)
    tq = min(512, s)
    nq = s // tq
    gw = NSA_GROUP * HEAD_DIM
    return pl.pallas_call(
        functools.partial(_nsa_cmp_kernel, tq=tq, n_sel=n_sel, n_cmp=n_cmp),
        out_shape=(jax.ShapeDtypeStruct((b * s, lay['nsa_h'] * HEAD_DIM), jnp.float32),
                   jax.ShapeDtypeStruct((b, kv, s, LANES), MXU_DTYPE)),
        grid=(b, kv, nq),
        in_specs=[pl.BlockSpec((tq, gw), lambda bi, k, qi: (bi * nq + qi, k)),
                  pl.BlockSpec((1, 1, 1, width, HEAD_DIM), lambda bi, k, qi: (bi, 0, k, 0, 0)),
                  pl.BlockSpec((1, 1, 1, width, HEAD_DIM), lambda bi, k, qi: (bi, 1, k, 0, 0))],
        out_specs=(pl.BlockSpec((tq, gw), lambda bi, k, qi: (bi * nq + qi, k)),
                   pl.BlockSpec((1, 1, tq, LANES), lambda bi, k, qi: (bi, k, qi, 0))),
        compiler_params=_params("parallel", "parallel", "parallel"),
    )(hp, kcp, kcp)


def _nsa_main_kernel(q_ref, ks_ref, vs_ref, kw_ref, vw_ref, sel_ref, e_ref, oc_ref, gl_ref, o_ref,
                     ms_ref, as_ref, mw_ref, aw_ref, *, tq, tk):
    qi = pl.program_id(2)
    last = (qi * tq) // tk
    first_win = jnp.maximum((qi * tq - WINDOW + 1) // tk, 0)
    _init_state(ms_ref, as_ref)
    _init_state(mw_ref, aw_ref)

    q_all = jnp.concatenate([q_ref[:, g * HEAD_DIM:(g + 1) * HEAD_DIM] for g in range(NSA_GROUP)], axis=0)

    def scores(k, bias):
        s = _dot_nt(q_all, k).reshape(NSA_GROUP, tq, tk) + bias[None]
        return s.reshape(NSA_GROUP * tq, tk)

    def positions(ki):
        return qi * tq + _iota((tq, tk), 0), ki * tk + _iota((tq, tk), 1)

    def selected(ki, causal):
        rows = _key_rows(ki, tk)
        ok = _dot(sel_ref[0, 0], e_ref[ki]) > 0.5
        if causal:
            t, u = positions(ki)
            ok = ok & (u <= t)
        _flash_step(scores(ks_ref[rows, :], _mask_bias(ok)), vs_ref[rows, :], ms_ref, as_ref)

    def window(ki):
        rows = _key_rows(ki, tk)
        t, u = positions(ki)
        b_win = _mask_bias((u <= t) & (u > t - WINDOW))
        _flash_step(scores(kw_ref[rows, :], b_win), vw_ref[rows, :], mw_ref, aw_ref)

    _for_tiles(0, last, lambda ki: selected(ki, False))
    selected(last, True)
    _for_tiles(first_win, last + 1, window)

    gate = 1.0 / (1.0 + jnp.exp(-gl_ref[...]))
    for g in range(NSA_GROUP):
        o_cmp = oc_ref[:, g * HEAD_DIM:(g + 1) * HEAD_DIM]
        o_slc = _flash_result(as_ref, pl.ds(g * tq, tq))
        o_win = _flash_result(aw_ref, pl.ds(g * tq, tq))
        out = (gate[:, 3 * g:3 * g + 1] * o_cmp + gate[:, 3 * g + 1:3 * g + 2] * o_slc
               + gate[:, 3 * g + 2:3 * g + 3] * o_win)
        o_ref[:, g * HEAD_DIM:(g + 1) * HEAD_DIM] = out.astype(o_ref.dtype)


def _nsa_main(hp, sel, o_cmp, gate_logits, lay, b, s):
    kv = lay['kv']
    tq = min(256, s)
    tk = min(512, s)
    nq, nk = s // tq, s // tk
    assert tk % tq == 0
    gw = NSA_GROUP * HEAD_DIM
    ksb, vsb, kwb, vwb = (lay['off'][n] // HEAD_DIM for n in ('nks', 'nvs', 'nkw', 'nvw'))
    expand = _block_expansion(s, SEL_LEN).reshape(LANES, nk, tk).transpose(1, 0, 2)
    expand = jnp.asarray(expand, MXU_DTYPE)
    row = lambda bi, k, qi: (bi * nq + qi, k)
    return pl.pallas_call(
        functools.partial(_nsa_main_kernel, tq=tq, tk=tk),
        out_shape=jax.ShapeDtypeStruct((b * s, lay['nsa_h'] * HEAD_DIM), MXU_DTYPE),
        grid=(b, kv, nq),
        in_specs=[pl.BlockSpec((tq, gw), row),
                  pl.BlockSpec((s, HEAD_DIM), lambda bi, k, qi: (bi, ksb + k)),
                  pl.BlockSpec((s, HEAD_DIM), lambda bi, k, qi: (bi, vsb + k)),
                  pl.BlockSpec((s, HEAD_DIM), lambda bi, k, qi: (bi, kwb + k)),
                  pl.BlockSpec((s, HEAD_DIM), lambda bi, k, qi: (bi, vwb + k)),
                  pl.BlockSpec((1, 1, tq, LANES), lambda bi, k, qi: (bi, k, qi, 0)),
                  pl.BlockSpec((nk, LANES, tk), lambda bi, k, qi: (0, 0, 0)),
                  pl.BlockSpec((tq, gw), row),
                  pl.BlockSpec((tq, LANES), row)],
        out_specs=pl.BlockSpec((tq, gw), row),
        scratch_shapes=[pltpu.VMEM((NSA_GROUP * tq, LANES), jnp.float32),
                        pltpu.VMEM((NSA_GROUP * tq, HEAD_DIM + LANES), jnp.float32)] * 2,
        compiler_params=_params("parallel", "parallel", "arbitrary"),
    )(hp, hp, hp, hp, hp, sel, expand, o_cmp, gate_logits)


def _layer_norm(y, g, b):
    mu = jnp.mean(y, axis=-1, keepdims=True)
    var = jnp.mean(jnp.square(y - mu), axis=-1, keepdims=True)
    return (y - mu) * lax.rsqrt(var + LN_EPS) * g + b


def _out_proj_kernel(mix_ref, w_ref, x_ref, g_ref, b_ref, wr_ref, br_ref, x1_ref, idx_ref, gate_ref,
                     acc_ref, *, alpha, n_experts):
    k = pl.program_id(1)

    @pl.when(k == 0)
    def _():
        acc_ref[...] = jnp.zeros(acc_ref.shape, jnp.float32)

    acc_ref[...] += _dot(mix_ref[...], w_ref[...])

    @pl.when(k == pl.num_programs(1) - 1)
    def _():
        x1 = _layer_norm(alpha * x_ref[...] + acc_ref[...], g_ref[...], b_ref[...])
        x1_ref[...] = x1
        logits = _dot(x1.astype(MXU_DTYPE), wr_ref[...]) + br_ref[...]
        lane = _lane_index(logits.shape)
        work = jnp.where(lane < n_experts, logits, -jnp.inf)
        idx_out = jnp.zeros(logits.shape, jnp.float32)
        val_out = jnp.zeros(logits.shape, jnp.float32)
        top = None
        for r in range(TOP_K):
            mx = jnp.max(work, axis=-1, keepdims=True)
            first = _first_lane(work == mx, lane)
            top = mx if top is None else top
            idx_out = jnp.where(lane == r, first, idx_out)
            val_out = jnp.where(lane == r, jnp.exp(mx - top), val_out)
            work = jnp.where(lane == first, -jnp.inf, work)
        idx_ref[...] = idx_out.astype(jnp.int32)
        gate_ref[...] = val_out / jnp.sum(val_out, axis=-1, keepdims=True)


def _out_proj_ln_router(mix, w_out, x, ln_g, ln_b, w_router, b_router, alpha):
    t, kdim = mix.shape
    d = w_out.shape[1]
    n_experts = w_router.shape[1]
    tm = min(256, t)
    tk = min(512, kdim)
    wr = jnp.pad(w_router, ((0, 0), (0, LANES - n_experts))).astype(MXU_DTYPE)
    br = jnp.pad(b_router.astype(jnp.float32), (0, LANES - n_experts)).reshape(1, LANES)
    row = lambda i, k: (i, 0)
    const = lambda i, k: (0, 0)
    return pl.pallas_call(
        functools.partial(_out_proj_kernel, alpha=alpha, n_experts=n_experts),
        out_shape=(jax.ShapeDtypeStruct((t, d), jnp.float32),
                   jax.ShapeDtypeStruct((t, LANES), jnp.int32),
                   jax.ShapeDtypeStruct((t, LANES), jnp.float32)),
        grid=(t // tm, kdim // tk),
        in_specs=[pl.BlockSpec((tm, tk), lambda i, k: (i, k)),
                  pl.BlockSpec((tk, d), lambda i, k: (k, 0)),
                  pl.BlockSpec((tm, d), row),
                  pl.BlockSpec((1, d), const), pl.BlockSpec((1, d), const),
                  pl.BlockSpec((d, LANES), const), pl.BlockSpec((1, LANES), const)],
        out_specs=(pl.BlockSpec((tm, d), row), pl.BlockSpec((tm, LANES), row),
                   pl.BlockSpec((tm, LANES), row)),
        scratch_shapes=[pltpu.VMEM((tm, d), jnp.float32)],
        compiler_params=_params("parallel", "arbitrary"),
    )(mix, w_out, x, ln_g.reshape(1, d), ln_b.reshape(1, d), wr, br)


MOE_TILE = 256
COMBINE_TILE = 128
GATHER_UNROLL = 8


GLU_GROUP = 2 * LANES


def _regroup_kernel(w_ref, p_ref, o_ref):
    w = w_ref[0].astype(MXU_DTYPE)
    for c in range(w.shape[1] // GLU_GROUP):
        cols = slice(c * GLU_GROUP, (c + 1) * GLU_GROUP)
        o_ref[0, :, cols] = _dot(w[:, cols], p_ref[...]).astype(o_ref.dtype)


def _regroup_gate_up(w_gate_up):
    e, d, f2 = w_gate_up.shape
    j = np.arange(GLU_GROUP)
    src = np.where(j < LANES, 2 * j, 2 * (j - LANES) + 1)
    perm = np.zeros((GLU_GROUP, GLU_GROUP), np.float32)
    perm[src, j] = 1.0
    tk = min(1024, d)
    return pl.pallas_call(
        _regroup_kernel,
        out_shape=jax.ShapeDtypeStruct((e, d, f2), MXU_DTYPE),
        grid=(e, d // tk),
        in_specs=[pl.BlockSpec((1, tk, f2), lambda ei, ki: (ei, ki, 0)),
                  pl.BlockSpec((GLU_GROUP, GLU_GROUP), lambda ei, ki: (0, 0))],
        out_specs=pl.BlockSpec((1, tk, f2), lambda ei, ki: (ei, ki, 0)),
        compiler_params=_params("parallel", "parallel"),
    )(w_gate_up, jnp.asarray(perm, MXU_DTYPE))


def _regroup_bias(b_gate_up):
    e, f2 = b_gate_up.shape
    b = b_gate_up.astype(jnp.float32).reshape(e, f2 // GLU_GROUP, LANES, 2)
    return b.transpose(0, 1, 3, 2).reshape(e, 1, f2)


def _moe_kernel(rt_ref, te_ref, nu_ref, x_hbm, wgu_ref, bgu_ref, wd_ref, bd_ref, o_ref, xbuf, sem):
    i = pl.program_id(0)
    n_used = nu_ref[0]

    def row_copy(tile, slot, r):
        tok = rt_ref[tile * MOE_TILE + r]
        return pltpu.make_async_copy(x_hbm.at[pl.ds(tok, 1)], xbuf.at[slot, pl.ds(r, 1)], sem.at[slot])

    def gather(tile, slot):
        def body(r, carry):
            row_copy(tile, slot, r).start()
            return carry
        lax.fori_loop(0, MOE_TILE, body, 0, unroll=GATHER_UNROLL)

    @pl.when(i == 0)
    def _():
        gather(0, 0)

    @pl.when(i + 1 < n_used)
    def _():
        gather(i + 1, (i + 1) % 2)

    @pl.when(i < n_used)
    def _():
        slot = i % 2
        pltpu.make_async_copy(x_hbm.at[pl.ds(0, MOE_TILE)], xbuf.at[slot], sem.at[slot]).wait()
        x = xbuf[slot].astype(MXU_DTYPE)
        hid = _dot(x, wgu_ref[0]) + bgu_ref[0]
        acts = []
        for c in range(hid.shape[1] // GLU_GROUP):
            h_glu = jnp.minimum(hid[:, c * GLU_GROUP:c * GLU_GROUP + LANES], SWIGLU_LIMIT)
            h_lin = jnp.clip(hid[:, c * GLU_GROUP + LANES:(c + 1) * GLU_GROUP], -SWIGLU_LIMIT, SWIGLU_LIMIT)
            acts.append(h_glu * (1.0 / (1.0 + jnp.exp(-SWIGLU_ALPHA * h_glu))) * (h_lin + 1.0))
        act = jnp.concatenate(acts, axis=1)
        o_ref[...] = _dot(act.astype(MXU_DTYPE), wd_ref[0]) + bd_ref[0]

    @pl.when(i >= n_used)
    def _():
        o_ref[...] = jnp.zeros(o_ref.shape, o_ref.dtype)


def _moe_experts(x1, row_token, tile_expert, n_used, wgu, bgu, wd, bd):
    t, d = x1.shape
    f2 = wgu.shape[2]
    n_tiles = tile_expert.shape[0]
    wmap = lambda i, rt, te, nu: (te[i], 0, 0)
    grid_spec = pltpu.PrefetchScalarGridSpec(
        num_scalar_prefetch=3,
        grid=(n_tiles,),
        in_specs=[pl.BlockSpec(memory_space=pl.ANY),
                  pl.BlockSpec((1, d, f2), wmap), pl.BlockSpec((1, 1, f2), wmap),
                  pl.BlockSpec((1, f2 // 2, d), wmap), pl.BlockSpec((1, 1, d), wmap)],
        out_specs=pl.BlockSpec((MOE_TILE, d), lambda i, rt, te, nu: (i, 0)),
        scratch_shapes=[pltpu.VMEM((2, MOE_TILE, d), jnp.float32), pltpu.SemaphoreType.DMA((2,))])
    return pl.pallas_call(
        _moe_kernel,
        out_shape=jax.ShapeDtypeStruct((n_tiles * MOE_TILE, d), jnp.float32),
        grid_spec=grid_spec, compiler_params=_params("arbitrary"),
    )(row_token, tile_expert, n_used, x1, wgu, bgu, wd, bd)


def _combine_kernel(pos_ref, y_hbm, x_ref, gate_ref, g_ref, b_ref, o_ref, ob_ref, ybuf, sem, *, alpha):
    i = pl.program_id(0)
    n_rows = COMBINE_TILE * TOP_K

    def gather(tile, slot):
        def body(r, carry):
            src = pos_ref[tile * n_rows + r]
            pltpu.make_async_copy(y_hbm.at[pl.ds(src, 1)], ybuf.at[slot, pl.ds(r, 1)], sem.at[slot]).start()
            return carry
        lax.fori_loop(0, n_rows, body, 0, unroll=GATHER_UNROLL)

    @pl.when(i == 0)
    def _():
        gather(0, 0)

    @pl.when(i + 1 < pl.num_programs(0))
    def _():
        gather(i + 1, (i + 1) % 2)

    slot = i % 2
    pltpu.make_async_copy(y_hbm.at[pl.ds(0, n_rows)], ybuf.at[slot], sem.at[slot]).wait()
    gate = gate_ref[...]
    ffn = jnp.zeros(x_ref.shape, jnp.float32)
    for k in range(TOP_K):
        ffn = ffn + gate[:, k:k + 1] * ybuf[slot, pl.ds(k * COMBINE_TILE, COMBINE_TILE)]
    x2 = _layer_norm(alpha * x_ref[...] + ffn, g_ref[...], b_ref[...])
    o_ref[...] = x2
    ob_ref[...] = x2.astype(ob_ref.dtype)


def _combine_ln(y, pos, x1, gate, ln_g, ln_b, alpha):
    t, d = x1.shape
    row = lambda i, p: (i, 0)
    const = lambda i, p: (0, 0)
    grid_spec = pltpu.PrefetchScalarGridSpec(
        num_scalar_prefetch=1,
        grid=(t // COMBINE_TILE,),
        in_specs=[pl.BlockSpec(memory_space=pl.ANY),
                  pl.BlockSpec((COMBINE_TILE, d), row), pl.BlockSpec((COMBINE_TILE, LANES), row),
                  pl.BlockSpec((1, d), const), pl.BlockSpec((1, d), const)],
        out_specs=(pl.BlockSpec((COMBINE_TILE, d), row), pl.BlockSpec((COMBINE_TILE, d), row)),
        scratch_shapes=[pltpu.VMEM((2, COMBINE_TILE * TOP_K, d), jnp.float32),
                        pltpu.SemaphoreType.DMA((2,))])
    return pl.pallas_call(
        functools.partial(_combine_kernel, alpha=alpha),
        out_shape=(jax.ShapeDtypeStruct((t, d), jnp.float32), jax.ShapeDtypeStruct((t, d), MXU_DTYPE)),
        grid_spec=grid_spec, compiler_params=_params("arbitrary"),
    )(pos, y, x1, gate, ln_g.reshape(1, d), ln_b.reshape(1, d))


def _route(top_idx, n_experts):
    t = top_idx.shape[0]
    n_assign = t * TOP_K
    n_tiles = n_assign // MOE_TILE + n_experts
    expert = top_idx.reshape(n_assign)
    onehot = (expert[:, None] == jnp.arange(n_experts, dtype=jnp.int32)[None, :]).astype(jnp.int32)
    running = jnp.cumsum(onehot, axis=0)
    counts = running[-1]
    rank = jnp.take_along_axis(running, expert[:, None], axis=1)[:, 0] - 1
    padded = (counts + MOE_TILE - 1) // MOE_TILE * MOE_TILE
    pad_end = jnp.cumsum(padded)
    dest = (pad_end - padded)[expert] + rank
    slot = dest.reshape(t // COMBINE_TILE, COMBINE_TILE, TOP_K).transpose(0, 2, 1).reshape(n_assign)
    token = jnp.arange(n_assign, dtype=jnp.int32) // TOP_K
    row_token = jnp.zeros((n_tiles * MOE_TILE,), jnp.int32).at[dest].set(token)
    tile_start = jnp.arange(n_tiles, dtype=jnp.int32) * MOE_TILE
    tile_expert = jnp.minimum(
        jnp.sum((pad_end[None, :] <= tile_start[:, None]).astype(jnp.int32), axis=1), n_experts - 1)
    n_used = (pad_end[-1:] // MOE_TILE).astype(jnp.int32)
    return row_token, tile_expert, n_used, slot.astype(jnp.int32)


def _rope_tables(positions):
    inv_freq = 1.0 / (ROPE_THETA ** (jnp.arange(0, ROT_DIM, 2, dtype=jnp.float32) / ROT_DIM))
    ang = positions.astype(jnp.float32).reshape(-1)[:, None] * inv_freq
    cos, sin = jnp.cos(ang), jnp.sin(ang)
    rest = HEAD_DIM - ROT_DIM
    n = ang.shape[0]
    ct = jnp.concatenate([cos, cos, jnp.ones((n, rest), jnp.float32)], axis=1)
    sa = jnp.concatenate([jnp.zeros_like(sin), sin, jnp.zeros((n, rest), jnp.float32)], axis=1)
    sb = jnp.concatenate([-sin, jnp.zeros_like(sin), jnp.zeros((n, rest), jnp.float32)], axis=1)
    return ct, sa, sb


def _gate_weight(w_in_l, lay):
    d = w_in_l.shape[0]
    kv = lay['kv']
    o = lay['ref_off']['ngate']
    wg = w_in_l[:, o:o + lay['size']['ngate']].reshape(d, kv, NSA_GROUP * 3)
    wg = jnp.pad(wg, ((0, 0), (0, 0), (0, LANES - NSA_GROUP * 3)))
    return wg.reshape(d, kv * LANES).astype(MXU_DTYPE)


def kernel(x, positions, w_in, nsa_cmp_pos, nsa_cmp_w1, nsa_cmp_w2, diff_lambda, diff_subln_g, w_out,
           ln1_g, ln1_b, w_router, b_router, w_gate_up, b_gate_up, w_down, b_down, ln2_g, ln2_b):
    b, s, d = x.shape
    depth = w_in.shape[0]
    n_experts = w_router.shape[2]
    lay = _layout(d)
    alpha = (2 * depth) ** 0.25
    ct, sa, sb = _rope_tables(positions)
    xf = x.reshape(b * s, d).astype(jnp.float32)
    xb = xf.astype(MXU_DTYPE)
    for layer in range(depth):
        w_l = w_in[layer]
        w_perm = jnp.concatenate(
            [w_l[:, lay['ref_off'][n]:lay['ref_off'][n] + lay['size'][n]] for n in ROPED + PLAIN],
            axis=1).astype(MXU_DTYPE)
        hp = _project(xb, w_perm, ct, sa, sb, lay['n_roped'])
        gate_logits = _gate_logits(xb, _gate_weight(w_l, lay))

        kcv = _nsa_compress(hp, nsa_cmp_pos[layer], nsa_cmp_w1[layer], nsa_cmp_w2[layer], lay, b, s)
        o_cmp, sel = _nsa_cmp_select(hp, kcv, lay, b, s)
        y_nsa = _nsa_main(hp, sel, o_cmp, gate_logits, lay, b, s)
        lambda_init = 0.8 - 0.6 * math.exp(-0.3 * layer)
        y_diff = _diff_attention(hp, diff_lambda[layer], diff_subln_g[layer], lay, b, s, lambda_init)
        y_moba = _moba_attention(hp, lay, b, s)
        mix = jnp.concatenate([y_nsa, y_diff, y_moba], axis=1)

        x1, top_idx, top_gate = _out_proj_ln_router(
            mix, w_out[layer].astype(MXU_DTYPE), xf, ln1_g[layer], ln1_b[layer],
            w_router[layer], b_router[layer], alpha)

        row_token, tile_expert, n_used, slot = _route(top_idx[:, :TOP_K], n_experts)
        y = _moe_experts(
            x1, row_token, tile_expert, n_used,
            _regroup_gate_up(w_gate_up[layer]), _regroup_bias(b_gate_up[layer]),
            w_down[layer].astype(MXU_DTYPE), b_down[layer].astype(jnp.float32).reshape(n_experts, 1, d))
        xf, xb = _combine_ln(y, slot, x1, top_gate, ln2_g[layer], ln2_b[layer], alpha)
    return xf.reshape(b, s, d).astype(x.dtype)
```

```python
import functools
import math

import numpy as np
import jax
import jax.numpy as jnp
from jax import lax
from jax.experimental import pallas as pl
from jax.experimental.pallas import tpu as pltpu

HEAD_DIM = 128
ROT_DIM = HEAD_DIM // 4
ROT_HALF = ROT_DIM // 2
ROPE_THETA = 500000.0
ATTN_SCALE = HEAD_DIM ** -0.5
NEG_INF = -1e30
LN_EPS = 1e-5

NSA_GROUP = 4
CMP_LEN = 32
CMP_STRIDE = 16
SEL_LEN = 64
SEL_TOPN = 16
SEL_FORCED = 3
WINDOW = 512
CMP_PER_SEL = SEL_LEN // CMP_STRIDE

DIFF_VDIM = 2 * HEAD_DIM
MOBA_BLOCK = 256
MOBA_TOPK = 3

TOP_K = 4
SWIGLU_LIMIT = 7.0
SWIGLU_ALPHA = 1.702

LANES = 128
MXU_DTYPE = jnp.bfloat16
VMEM_LIMIT = 56 * 1024 * 1024

ROPED = ('nq', 'nkc', 'nks', 'nkw', 'dq', 'dk', 'mq', 'mk')
PLAIN = ('nvc', 'nvs', 'nvw', 'dv', 'mv')
REF_ORDER = ('nq', 'nkc', 'nvc', 'nks', 'nvs', 'nkw', 'nvw', 'ngate', 'dq', 'dk', 'dv', 'mq', 'mk', 'mv')


def _layout(d):
    nsa_h = d // 256
    kv = nsa_h // NSA_GROUP
    diff_h = d // 1024
    moba_h = d // 512
    size = dict(nq=nsa_h * HEAD_DIM, nkc=kv * HEAD_DIM, nvc=kv * HEAD_DIM, nks=kv * HEAD_DIM,
                nvs=kv * HEAD_DIM, nkw=kv * HEAD_DIM, nvw=kv * HEAD_DIM, ngate=3 * nsa_h,
                dq=2 * diff_h * HEAD_DIM, dk=2 * diff_h * HEAD_DIM, dv=diff_h * DIFF_VDIM,
                mq=moba_h * HEAD_DIM, mk=moba_h * HEAD_DIM, mv=moba_h * HEAD_DIM)
    ref_off, o = {}, 0
    for n in REF_ORDER:
        ref_off[n] = o
        o += size[n]
    off, o = {}, 0
    for n in ROPED + PLAIN:
        off[n] = o
        o += size[n]
    n_roped = sum(size[n] for n in ROPED)
    return dict(nsa_h=nsa_h, kv=kv, diff_h=diff_h, moba_h=moba_h, size=size, ref_off=ref_off,
                off=off, n_roped=n_roped, n_cols=o)


def _params(*sem):
    return pltpu.CompilerParams(dimension_semantics=sem, vmem_limit_bytes=VMEM_LIMIT)


def _dot(a, b):
    return jnp.dot(a, b, preferred_element_type=jnp.float32)


def _dot_nt(a, b):
    return lax.dot_general(a, b, (((1,), (1,)), ((), ())), preferred_element_type=jnp.float32)


def _iota(shape, dim):
    return lax.broadcasted_iota(jnp.int32, shape, dim)


def _lane_index(shape):
    return _iota(shape, 1).astype(jnp.float32)


def _first_lane(hit, lane):
    return jnp.min(jnp.where(hit, lane, float(LANES)), axis=-1, keepdims=True)


def _proj_kernel(x_ref, w_ref, ct_ref, sa_ref, sb_ref, o_ref, *, n_rope_blocks, tn):
    j = pl.program_id(1)
    acc = _dot(x_ref[...], w_ref[...])

    @pl.when(j < n_rope_blocks)
    def _():
        ct, sa, sb = ct_ref[...], sa_ref[...], sb_ref[...]
        for c in range(tn // HEAD_DIM):
            a = acc[:, c * HEAD_DIM:(c + 1) * HEAD_DIM]
            r = (a * ct + pltpu.roll(a, ROT_HALF, 1) * sa
                 + pltpu.roll(a, HEAD_DIM - ROT_HALF, 1) * sb)
            o_ref[:, c * HEAD_DIM:(c + 1) * HEAD_DIM] = r.astype(o_ref.dtype)

    @pl.when(j >= n_rope_blocks)
    def _():
        o_ref[...] = acc.astype(o_ref.dtype)


def _project(xb, w, ct, sa, sb, n_roped):
    m, k = xb.shape
    n = w.shape[1]
    tm = min(1024, m)
    tn = next(t for t in (512, 256, 128) if n % t == 0 and n_roped % t == 0)
    kern = functools.partial(_proj_kernel, n_rope_blocks=n_roped // tn, tn=tn)
    return pl.pallas_call(
        kern,
        out_shape=jax.ShapeDtypeStruct((m, n), MXU_DTYPE),
        grid=(m // tm, n // tn),
        in_specs=[pl.BlockSpec((tm, k), lambda i, j: (i, 0)),
                  pl.BlockSpec((k, tn), lambda i, j: (0, j)),
                  pl.BlockSpec((tm, HEAD_DIM), lambda i, j: (i, 0)),
                  pl.BlockSpec((tm, HEAD_DIM), lambda i, j: (i, 0)),
                  pl.BlockSpec((tm, HEAD_DIM), lambda i, j: (i, 0))],
        out_specs=pl.BlockSpec((tm, tn), lambda i, j: (i, j)),
        compiler_params=_params("parallel", "arbitrary"),
    )(xb, w, ct, sa, sb)


def _gate_kernel(x_ref, w_ref, o_ref):
    o_ref[...] = _dot(x_ref[...], w_ref[...])


def _gate_logits(xb, wg):
    m, k = xb.shape
    n = wg.shape[1]
    tm = min(1024, m)
    return pl.pallas_call(
        _gate_kernel,
        out_shape=jax.ShapeDtypeStruct((m, n), jnp.float32),
        grid=(m // tm,),
        in_specs=[pl.BlockSpec((tm, k), lambda i: (i, 0)),
                  pl.BlockSpec((k, n), lambda i: (0, 0))],
        out_specs=pl.BlockSpec((tm, n), lambda i: (i, 0)),
        compiler_params=_params("parallel"),
    )(xb, wg)


EXP2_SCALE = ATTN_SCALE * math.log2(math.e)


def _mask_bias(mask):
    return jnp.where(mask, 0.0, -jnp.inf)


def _lanes(x, n):
    return x if n == LANES else jnp.concatenate([x] * (n // LANES), axis=1)


def _flash_step(s, v, m_ref, acc_ref):
    dv = v.shape[1]
    m_old = m_ref[...]
    m_new = jnp.maximum(m_old, jnp.max(s, axis=-1, keepdims=True))
    p = jnp.exp2((s - _lanes(m_new, s.shape[1])) * EXP2_SCALE)
    alpha = jnp.exp2((m_old - m_new) * EXP2_SCALE)
    acc_ref[:, :dv] = _lanes(alpha, dv) * acc_ref[:, :dv] + _dot(p.astype(v.dtype), v)
    acc_ref[:, dv:] = alpha * acc_ref[:, dv:] + jnp.sum(p, axis=-1, keepdims=True)
    m_ref[...] = m_new


def _flash_result(acc_ref, rows):
    dv = acc_ref.shape[1] - LANES
    denom = jnp.maximum(acc_ref[rows, dv:], 1e-30)
    return acc_ref[rows, :dv] / _lanes(denom, dv)


def _init_state(m_ref, acc_ref):
    m_ref[...] = jnp.full(m_ref.shape, NEG_INF, jnp.float32)
    acc_ref[...] = jnp.zeros(acc_ref.shape, jnp.float32)


def _key_rows(ki, tk):
    return pl.ds(pl.multiple_of(ki * tk, tk), tk)


def _for_tiles(lo, hi, body):
    def wrapped(ki, carry):
        body(ki)
        return carry
    lax.fori_loop(lo, hi, wrapped, 0)


def _diff_kernel(lam_ref, g_ref, q_ref, k_ref, v_ref, o_ref, m_ref, acc_ref, *, tq, lambda_init):
    qi = pl.program_id(2)
    tk = tq
    _init_state(m_ref, acc_ref)

    def tile(ki, masked):
        rows = _key_rows(ki, tk)
        maps = []
        for mp in range(2):
            q = q_ref[:, mp * HEAD_DIM:(mp + 1) * HEAD_DIM]
            maps.append(_dot_nt(q, k_ref[rows, mp * HEAD_DIM:(mp + 1) * HEAD_DIM]))
        if masked:
            bias = _mask_bias(_iota((tq, tk), 1) <= _iota((tq, tk), 0))
            maps = [s + bias for s in maps]
        _flash_step(jnp.concatenate(maps, axis=0), v_ref[rows, :], m_ref, acc_ref)

    _for_tiles(0, qi, lambda ki: tile(ki, False))
    tile(qi, True)

    lv = lam_ref[...]
    lam = (jnp.exp(jnp.sum(lv[0:1] * lv[1:2], axis=-1, keepdims=True))
           - jnp.exp(jnp.sum(lv[2:3] * lv[3:4], axis=-1, keepdims=True)) + lambda_init)
    o = _flash_result(acc_ref, pl.ds(0, tq)) - lam * _flash_result(acc_ref, pl.ds(tq, tq))
    o = o * lax.rsqrt(jnp.mean(jnp.square(o), axis=-1, keepdims=True) + LN_EPS) * g_ref[...]
    o_ref[...] = (o * (1.0 - lambda_init)).astype(o_ref.dtype)


def _diff_attention(hp, lam_vecs, subln_g, lay, b, s, lambda_init):
    hd = lay['diff_h']
    tq = min(512, s)
    nq = s // tq
    qb, kb, vb = (lay['off'][n] // DIFF_VDIM for n in ('dq', 'dk', 'dv'))
    assert all(lay['off'][n] % DIFF_VDIM == 0 for n in ('dq', 'dk', 'dv'))
    kern = functools.partial(_diff_kernel, tq=tq, lambda_init=lambda_init)
    return pl.pallas_call(
        kern, out_shape=jax.ShapeDtypeStruct((b * s, hd * DIFF_VDIM), MXU_DTYPE),
        grid=(b, hd, nq),
        in_specs=[pl.BlockSpec((4, HEAD_DIM), lambda bi, h, qi: (0, 0)),
                  pl.BlockSpec((1, DIFF_VDIM), lambda bi, h, qi: (0, 0)),
                  pl.BlockSpec((tq, DIFF_VDIM), lambda bi, h, qi: (bi * nq + qi, qb + h)),
                  pl.BlockSpec((s, DIFF_VDIM), lambda bi, h, qi: (bi, kb + h)),
                  pl.BlockSpec((s, DIFF_VDIM), lambda bi, h, qi: (bi, vb + h))],
        out_specs=pl.BlockSpec((tq, DIFF_VDIM), lambda bi, h, qi: (bi * nq + qi, h)),
        scratch_shapes=[pltpu.VMEM((2 * tq, LANES), jnp.float32),
                        pltpu.VMEM((2 * tq, DIFF_VDIM + LANES), jnp.float32)],
        compiler_params=_params("parallel", "parallel", "arbitrary"),
    )(lam_vecs.astype(jnp.float32), subln_g.reshape(1, DIFF_VDIM).astype(jnp.float32), hp, hp, hp)


def _kmean_kernel(k_ref, o_ref, *, n_blk):
    k = k_ref[...].astype(jnp.float32).reshape(n_blk, MOBA_BLOCK, HEAD_DIM)
    o_ref[0, 0] = jnp.mean(k, axis=1)


def _moba_kmean(hp, lay, b, s):
    h = lay['moba_h']
    n_blk = s // MOBA_BLOCK
    kb = lay['off']['mk'] // HEAD_DIM
    return pl.pallas_call(
        functools.partial(_kmean_kernel, n_blk=n_blk),
        out_shape=jax.ShapeDtypeStruct((b, h, n_blk, HEAD_DIM), jnp.float32),
        grid=(b, h),
        in_specs=[pl.BlockSpec((s, HEAD_DIM), lambda bi, hi: (bi, kb + hi))],
        out_specs=pl.BlockSpec((1, 1, n_blk, HEAD_DIM), lambda bi, hi: (bi, hi, 0, 0)),
        compiler_params=_params("parallel", "parallel"),
    )(hp)


def _block_expansion(s, block):
    return (np.arange(s)[None, :] // block == np.arange(LANES)[:, None]).astype(np.float32)


def _lane_column(x, n):
    return jnp.sum(jnp.where(_iota(x.shape, 1) == n, x, 0.0), axis=-1, keepdims=True)


def _moba_kernel(q_ref, k_ref, v_ref, km_ref, o_ref, bias_ref, m_ref, acc_ref, *, tq, tk):
    qi = pl.program_id(2)
    q = q_ref[...]
    _init_state(m_ref, acc_ref)
    score = _dot_nt(q, km_ref[0, 0].astype(q.dtype))
    blk = _lane_index((tq, LANES))
    own = ((qi * tq + _iota((tq, LANES), 0)) // MOBA_BLOCK).astype(jnp.float32)
    work = jnp.where(blk < own, score, -jnp.inf)
    sel = jnp.where(blk == own, 1.0, 0.0)
    for r in range(MOBA_TOPK):
        mx = jnp.max(work, axis=-1, keepdims=True)
        first = _first_lane(work == mx, blk)
        pick = blk == first
        sel = jnp.where(pick & (own > r), 1.0, sel)
        work = jnp.where(pick, -jnp.inf, work)
    bias_ref[...] = _mask_bias(sel > 0.5)

    def tile(ki, causal):
        rows = _key_rows(ki, tk)
        cols = []
        for c in range(tk // MOBA_BLOCK):
            col = _lane_column(bias_ref[...], ki * (tk // MOBA_BLOCK) + c)
            cols.append(jnp.broadcast_to(col, (tq, MOBA_BLOCK)))
        bias = cols[0] if len(cols) == 1 else jnp.concatenate(cols, axis=1)
        if causal:
            t = qi * tq + _iota((tq, tk), 0)
            u = ki * tk + _iota((tq, tk), 1)
            bias = jnp.where(u <= t, bias, -jnp.inf)
        _flash_step(_dot_nt(q, k_ref[rows, :]) + bias, v_ref[rows, :], m_ref, acc_ref)

    past = qi * (tq // tk)
    _for_tiles(0, past, lambda ki: tile(ki, False))
    for d in range(tq // tk):
        tile(past + d, True)
    o_ref[...] = _flash_result(acc_ref, pl.ds(0, tq)).astype(o_ref.dtype)


def _moba_attention(hp, lay, b, s):
    h = lay['moba_h']
    n_blk = s // MOBA_BLOCK
    assert s % MOBA_BLOCK == 0 and MOBA_TOPK <= n_blk <= LANES
    km = _moba_kmean(hp, lay, b, s)
    km = jnp.pad(km, ((0, 0), (0, 0), (0, LANES - n_blk), (0, 0)))
    tq = min(1024, s)
    tk = min(512, s)
    nq = s // tq
    assert tq % tk == 0 and tk % MOBA_BLOCK == 0
    qb, kb, vb = (lay['off'][n] // HEAD_DIM for n in ('mq', 'mk', 'mv'))
    return pl.pallas_call(
        functools.partial(_moba_kernel, tq=tq, tk=tk),
        out_shape=jax.ShapeDtypeStruct((b * s, h * HEAD_DIM), MXU_DTYPE),
        grid=(b, h, nq),
        in_specs=[pl.BlockSpec((tq, HEAD_DIM), lambda bi, hi, qi: (bi * nq + qi, qb + hi)),
                  pl.BlockSpec((s, HEAD_DIM), lambda bi, hi, qi: (bi, kb + hi)),
                  pl.BlockSpec((s, HEAD_DIM), lambda bi, hi, qi: (bi, vb + hi)),
                  pl.BlockSpec((1, 1, LANES, HEAD_DIM), lambda bi, hi, qi: (bi, hi, 0, 0))],
        out_specs=pl.BlockSpec((tq, HEAD_DIM), lambda bi, hi, qi: (bi * nq + qi, hi)),
        scratch_shapes=[pltpu.VMEM((tq, LANES), jnp.float32), pltpu.VMEM((tq, LANES), jnp.float32),
                        pltpu.VMEM((tq, HEAD_DIM + LANES), jnp.float32)],
        compiler_params=_params("parallel", "parallel", "arbitrary"),
    )(hp, hp, hp, km)


def _gelu_tanh(x):
    return 0.5 * x * (1.0 + jnp.tanh(math.sqrt(2.0 / math.pi) * (x + 0.044715 * (x * x * x))))


def _compress_kernel(seg_ref, plo_ref, phi_ref, w1lo_ref, w1hi_ref, w2_ref, o_ref, *, n_seg):
    seg = seg_ref[0, 0, 0].astype(jnp.float32)
    lo = _dot((seg + plo_ref[0]).astype(MXU_DTYPE), w1lo_ref[0])
    hi = _dot((seg + phi_ref[0]).astype(MXU_DTYPE), w1hi_ref[0])
    pre = lo + pltpu.roll(hi, n_seg - 1, 0)
    o_ref[0, 0, 0] = _dot(_gelu_tanh(pre).astype(MXU_DTYPE), w2_ref[0]).astype(o_ref.dtype)


def _nsa_compress(hp, cmp_pos, cmp_w1, cmp_w2, lay, b, s):
    kv = lay['kv']
    n_seg = s // CMP_STRIDE
    half = CMP_STRIDE * HEAD_DIM

    def segments(name):
        o = lay['off'][name]
        t = hp[:, o:o + kv * HEAD_DIM].reshape(b, n_seg, CMP_STRIDE, kv, HEAD_DIM)
        return t.transpose(0, 3, 1, 2, 4).reshape(b, kv, n_seg, half)

    seg = jnp.stack([segments('nkc'), segments('nvc')], axis=1)
    pos = cmp_pos.astype(jnp.float32).reshape(2, CMP_LEN * HEAD_DIM)
    plo = pos[:, :half].reshape(2, 1, half)
    phi = pos[:, half:].reshape(2, 1, half)
    w1 = cmp_w1.astype(MXU_DTYPE)
    w1lo, w1hi = w1[:, :half], w1[:, half:]
    w2 = cmp_w2.astype(MXU_DTYPE)
    return pl.pallas_call(
        functools.partial(_compress_kernel, n_seg=n_seg),
        out_shape=jax.ShapeDtypeStruct((b, 2, kv, n_seg, HEAD_DIM), MXU_DTYPE),
        grid=(b, 2, kv),
        in_specs=[pl.BlockSpec((1, 1, 1, n_seg, half), lambda bi, c, k: (bi, c, k, 0, 0)),
                  pl.BlockSpec((1, 1, half), lambda bi, c, k: (c, 0, 0)),
                  pl.BlockSpec((1, 1, half), lambda bi, c, k: (c, 0, 0)),
                  pl.BlockSpec((1, half, HEAD_DIM), lambda bi, c, k: (c, 0, 0)),
                  pl.BlockSpec((1, half, HEAD_DIM), lambda bi, c, k: (c, 0, 0)),
                  pl.BlockSpec((1, HEAD_DIM, HEAD_DIM), lambda bi, c, k: (c, 0, 0))],
        out_specs=pl.BlockSpec((1, 1, 1, n_seg, HEAD_DIM), lambda bi, c, k: (bi, c, k, 0, 0)),
        compiler_params=_params("parallel", "parallel", "parallel"),
    )(seg, plo, phi, w1lo, w1hi, w2)


def _nsa_cmp_kernel(q_ref, kc_ref, vc_ref, o_ref, sel_ref, *, tq, n_sel, n_cmp):
    qi = pl.program_id(2)
    width = CMP_PER_SEL * LANES
    t = qi * tq + _iota((tq, width), 0)
    pos = _iota((tq, width), 1)
    m_idx = pos % LANES
    n_idx = CMP_PER_SEL * m_idx + pos // LANES
    valid = (m_idx < n_sel) & (n_idx < n_cmp) & (n_idx * CMP_STRIDE + CMP_LEN - 1 <= t)
    kc = kc_ref[0, 0, 0]
    vc = vc_ref[0, 0, 0]
    p_grp = jnp.zeros((tq, width), jnp.float32)
    for g in range(NSA_GROUP):
        q = q_ref[:, g * HEAD_DIM:(g + 1) * HEAD_DIM]
        s = jnp.where(valid, _dot_nt(q, kc) * ATTN_SCALE, NEG_INF)
        p = jnp.where(valid, jnp.exp(s - jnp.max(s, axis=-1, keepdims=True)), 0.0)
        p = p / jnp.maximum(jnp.sum(p, axis=-1, keepdims=True), 1e-30)
        o_ref[:, g * HEAD_DIM:(g + 1) * HEAD_DIM] = _dot(p.astype(vc.dtype), vc)
        p_grp = p_grp + p

    slabs = [p_grp[:, r * LANES:(r + 1) * LANES] for r in range(CMP_PER_SEL)]
    blk = _lane_index((tq, LANES))
    prev = jnp.where(blk == 0, 0.0, pltpu.roll(slabs[CMP_PER_SEL - 1], 1, 1))
    imp = prev
    for r in range(CMP_PER_SEL):
        imp = imp + slabs[r]
    cur = ((qi * tq + _iota((tq, LANES), 0)) // SEL_LEN).astype(jnp.float32)
    forced = (blk == 0) | (blk == cur) | (blk == cur - 1)
    work = jnp.where(forced | (blk > cur), -jnp.inf, imp)
    sel = jnp.where(forced, 1.0, 0.0)
    for _ in range(SEL_TOPN - SEL_FORCED):
        mx = jnp.max(work, axis=-1, keepdims=True)
        first = _first_lane(work == mx, blk)
        pick = blk == first
        sel = jnp.where(pick, 1.0, sel)
        work = jnp.where(pick, -jnp.inf, work)
    sel_ref[0, 0] = jnp.where(blk <= cur, sel, 0.0).astype(sel_ref.dtype)


def _nsa_cmp_select(hp, kcv, lay, b, s):
    kv = lay['kv']
    n_seg = s // CMP_STRIDE
    n_sel = s // SEL_LEN
    n_cmp = (s - CMP_LEN) // CMP_STRIDE + 1
    assert SEL_TOPN <= n_sel <= LANES
    width = CMP_PER_SEL * LANES
    kcp = kcv.reshape(b, 2, kv, n_sel, CMP_PER_SEL, HEAD_DIM).transpose(0, 1, 2, 4, 3, 5)
    kcp = jnp.pad(kcp, ((0, 0),) * 4 + ((0, LANES - n_sel), (0, 0))).reshape(b, 2, kv, width, HEAD_DIM)
    tq = min(512, s)
    nq = s // tq
    gw = NSA_GROUP * HEAD_DIM
    return pl.pallas_call(
        functools.partial(_nsa_cmp_kernel, tq=tq, n_sel=n_sel, n_cmp=n_cmp),
        out_shape=(jax.ShapeDtypeStruct((b * s, lay['nsa_h'] * HEAD_DIM), jnp.float32),
                   jax.ShapeDtypeStruct((b, kv, s, LANES), MXU_DTYPE)),
        grid=(b, kv, nq),
        in_specs=[pl.BlockSpec((tq, gw), lambda bi, k, qi: (bi * nq + qi, k)),
                  pl.BlockSpec((1, 1, 1, width, HEAD_DIM), lambda bi, k, qi: (bi, 0, k, 0, 0)),
                  pl.BlockSpec((1, 1, 1, width, HEAD_DIM), lambda bi, k, qi: (bi, 1, k, 0, 0))],
        out_specs=(pl.BlockSpec((tq, gw), lambda bi, k, qi: (bi * nq + qi, k)),
                   pl.BlockSpec((1, 1, tq, LANES), lambda bi, k, qi: (bi, k, qi, 0))),
        compiler_params=_params("parallel", "parallel", "parallel"),
    )(hp, kcp, kcp)


def _nsa_main_kernel(q_ref, ks_ref, vs_ref, kw_ref, vw_ref, sel_ref, e_ref, oc_ref, gl_ref, o_ref,
                     ms_ref, as_ref, mw_ref, aw_ref, *, tq, tk):
    qi = pl.program_id(2)
    last = (qi * tq) // tk
    first_win = jnp.maximum((qi * tq - WINDOW + 1) // tk, 0)
    _init_state(ms_ref, as_ref)
    _init_state(mw_ref, aw_ref)

    q_all = jnp.concatenate([q_ref[:, g * HEAD_DIM:(g + 1) * HEAD_DIM] for g in range(NSA_GROUP)], axis=0)

    def scores(k, bias):
        s = _dot_nt(q_all, k).reshape(NSA_GROUP, tq, tk) + bias[None]
        return s.reshape(NSA_GROUP * tq, tk)

    def positions(ki):
        return qi * tq + _iota((tq, tk), 0), ki * tk + _iota((tq, tk), 1)

    def selected(ki, causal):
        rows = _key_rows(ki, tk)
        ok = _dot(sel_ref[0, 0], e_ref[ki]) > 0.5
        if causal:
            t, u = positions(ki)
            ok = ok & (u <= t)
        _flash_step(scores(ks_ref[rows, :], _mask_bias(ok)), vs_ref[rows, :], ms_ref, as_ref)

    def window(ki):
        rows = _key_rows(ki, tk)
        t, u = positions(ki)
        b_win = _mask_bias((u <= t) & (u > t - WINDOW))
        _flash_step(scores(kw_ref[rows, :], b_win), vw_ref[rows, :], mw_ref, aw_ref)

    _for_tiles(0, last, lambda ki: selected(ki, False))
    selected(last, True)
    _for_tiles(first_win, last + 1, window)

    gate = 1.0 / (1.0 + jnp.exp(-gl_ref[...]))
    for g in range(NSA_GROUP):
        o_cmp = oc_ref[:, g * HEAD_DIM:(g + 1) * HEAD_DIM]
        o_slc = _flash_result(as_ref, pl.ds(g * tq, tq))
        o_win = _flash_result(aw_ref, pl.ds(g * tq, tq))
        out = (gate[:, 3 * g:3 * g + 1] * o_cmp + gate[:, 3 * g + 1:3 * g + 2] * o_slc
               + gate[:, 3 * g + 2:3 * g + 3] * o_win)
        o_ref[:, g * HEAD_DIM:(g + 1) * HEAD_DIM] = out.astype(o_ref.dtype)


def _nsa_main(hp, sel, o_cmp, gate_logits, lay, b, s):
    kv = lay['kv']
    tq = min(256, s)
    tk = min(512, s)
    nq, nk = s // tq, s // tk
    assert tk % tq == 0
    gw = NSA_GROUP * HEAD_DIM
    ksb, vsb, kwb, vwb = (lay['off'][n] // HEAD_DIM for n in ('nks', 'nvs', 'nkw', 'nvw'))
    expand = _block_expansion(s, SEL_LEN).reshape(LANES, nk, tk).transpose(1, 0, 2)
    expand = jnp.asarray(expand, MXU_DTYPE)
    row = lambda bi, k, qi: (bi * nq + qi, k)
    return pl.pallas_call(
        functools.partial(_nsa_main_kernel, tq=tq, tk=tk),
        out_shape=jax.ShapeDtypeStruct((b * s, lay['nsa_h'] * HEAD_DIM), MXU_DTYPE),
        grid=(b, kv, nq),
        in_specs=[pl.BlockSpec((tq, gw), row),
                  pl.BlockSpec((s, HEAD_DIM), lambda bi, k, qi: (bi, ksb + k)),
                  pl.BlockSpec((s, HEAD_DIM), lambda bi, k, qi: (bi, vsb + k)),
                  pl.BlockSpec((s, HEAD_DIM), lambda bi, k, qi: (bi, kwb + k)),
                  pl.BlockSpec((s, HEAD_DIM), lambda bi, k, qi: (bi, vwb + k)),
                  pl.BlockSpec((1, 1, tq, LANES), lambda bi, k, qi: (bi, k, qi, 0)),
                  pl.BlockSpec((nk, LANES, tk), lambda bi, k, qi: (0, 0, 0)),
                  pl.BlockSpec((tq, gw), row),
                  pl.BlockSpec((tq, LANES), row)],
        out_specs=pl.BlockSpec((tq, gw), row),
        scratch_shapes=[pltpu.VMEM((NSA_GROUP * tq, LANES), jnp.float32),
                        pltpu.VMEM((NSA_GROUP * tq, HEAD_DIM + LANES), jnp.float32)] * 2,
        compiler_params=_params("parallel", "parallel", "arbitrary"),
    )(hp, hp, hp, hp, hp, sel, expand, o_cmp, gate_logits)


def _layer_norm(y, g, b):
    mu = jnp.mean(y, axis=-1, keepdims=True)
    var = jnp.mean(jnp.square(y - mu), axis=-1, keepdims=True)
    return (y - mu) * lax.rsqrt(var + LN_EPS) * g + b


def _out_proj_kernel(mix_ref, w_ref, x_ref, g_ref, b_ref, wr_ref, br_ref, x1_ref, idx_ref, gate_ref,
                     acc_ref, *, alpha, n_experts):
    k = pl.program_id(1)

    @pl.when(k == 0)
    def _():
        acc_ref[...] = jnp.zeros(acc_ref.shape, jnp.float32)

    acc_ref[...] += _dot(mix_ref[...], w_ref[0])

    @pl.when(k == pl.num_programs(1) - 1)
    def _():
        x1 = _layer_norm(alpha * x_ref[...] + acc_ref[...], g_ref[...], b_ref[...])
        x1_ref[...] = x1
        logits = _dot(x1.astype(MXU_DTYPE), wr_ref[...]) + br_ref[...]
        lane = _lane_index(logits.shape)
        work = jnp.where(lane < n_experts, logits, -jnp.inf)
        idx_out = jnp.zeros(logits.shape, jnp.float32)
        val_out = jnp.zeros(logits.shape, jnp.float32)
        top = None
        for r in range(TOP_K):
            mx = jnp.max(work, axis=-1, keepdims=True)
            first = _first_lane(work == mx, lane)
            top = mx if top is None else top
            idx_out = jnp.where(lane == r, first, idx_out)
            val_out = jnp.where(lane == r, jnp.exp(mx - top), val_out)
            work = jnp.where(lane == first, -jnp.inf, work)
        idx_ref[...] = idx_out.astype(jnp.int32)
        gate_ref[...] = val_out / jnp.sum(val_out, axis=-1, keepdims=True)


def _out_proj_ln_router(mix, w_out_stack, layer, x, ln_g, ln_b, w_router, b_router, alpha):
    t, kdim = mix.shape
    d = w_out_stack.shape[2]
    n_experts = w_router.shape[1]
    tm = min(256, t)
    tk = min(512, kdim)
    wr = jnp.pad(w_router, ((0, 0), (0, LANES - n_experts))).astype(MXU_DTYPE)
    br = jnp.pad(b_router.astype(jnp.float32), (0, LANES - n_experts)).reshape(1, LANES)
    row = lambda i, k: (i, 0)
    const = lambda i, k: (0, 0)
    return pl.pallas_call(
        functools.partial(_out_proj_kernel, alpha=alpha, n_experts=n_experts),
        out_shape=(jax.ShapeDtypeStruct((t, d), jnp.float32),
                   jax.ShapeDtypeStruct((t, LANES), jnp.int32),
                   jax.ShapeDtypeStruct((t, LANES), jnp.float32)),
        grid=(t // tm, kdim // tk),
        in_specs=[pl.BlockSpec((tm, tk), lambda i, k: (i, k)),
                  pl.BlockSpec((1, tk, d), lambda i, k: (layer, k, 0)),
                  pl.BlockSpec((tm, d), row),
                  pl.BlockSpec((1, d), const), pl.BlockSpec((1, d), const),
                  pl.BlockSpec((d, LANES), const), pl.BlockSpec((1, LANES), const)],
        out_specs=(pl.BlockSpec((tm, d), row), pl.BlockSpec((tm, LANES), row),
                   pl.BlockSpec((tm, LANES), row)),
        scratch_shapes=[pltpu.VMEM((tm, d), jnp.float32)],
        compiler_params=_params("parallel", "arbitrary"),
    )(mix, w_out_stack, x, ln_g.reshape(1, d), ln_b.reshape(1, d), wr, br)


MOE_TILE = 256
COMBINE_TILE = 128
GATHER_UNROLL = 8


GLU_GROUP = 2 * LANES


def _regroup_kernel(w_ref, p_ref, o_ref):
    w = w_ref[0, 0].astype(MXU_DTYPE)
    for c in range(w.shape[1] // GLU_GROUP):
        cols = slice(c * GLU_GROUP, (c + 1) * GLU_GROUP)
        o_ref[0, :, cols] = _dot(w[:, cols], p_ref[...]).astype(o_ref.dtype)


def _regroup_gate_up(w_gate_up, layer):
    _, e, d, f2 = w_gate_up.shape
    j = np.arange(GLU_GROUP)
    src = np.where(j < LANES, 2 * j, 2 * (j - LANES) + 1)
    perm = np.zeros((GLU_GROUP, GLU_GROUP), np.float32)
    perm[src, j] = 1.0
    tk = min(1024, d)
    return pl.pallas_call(
        _regroup_kernel,
        out_shape=jax.ShapeDtypeStruct((e, d, f2), MXU_DTYPE),
        grid=(e, d // tk),
        in_specs=[pl.BlockSpec((1, 1, tk, f2), lambda ei, ki: (layer, ei, ki, 0)),
                  pl.BlockSpec((GLU_GROUP, GLU_GROUP), lambda ei, ki: (0, 0))],
        out_specs=pl.BlockSpec((1, tk, f2), lambda ei, ki: (ei, ki, 0)),
        compiler_params=_params("parallel", "parallel"),
    )(w_gate_up, jnp.asarray(perm, MXU_DTYPE))


def _regroup_bias(b_gate_up):
    e, f2 = b_gate_up.shape
    b = b_gate_up.astype(jnp.float32).reshape(e, f2 // GLU_GROUP, LANES, 2)
    return b.transpose(0, 1, 3, 2).reshape(e, 1, f2)


def _moe_kernel(rt_ref, te_ref, nu_ref, x_hbm, wgu_ref, bgu_ref, wd_ref, bd_ref, o_ref, xbuf, sem):
    i = pl.program_id(0)
    n_used = nu_ref[0]

    def row_copy(tile, slot, r):
        tok = rt_ref[tile * MOE_TILE + r]
        return pltpu.make_async_copy(x_hbm.at[pl.ds(tok, 1)], xbuf.at[slot, pl.ds(r, 1)], sem.at[slot])

    @pl.when(i == 0)
    def _():
        def body(r, carry):
            row_copy(0, 0, r).start()
            return carry
        lax.fori_loop(0, MOE_TILE, body, 0, unroll=GATHER_UNROLL)

    def expert_tile(prefetch):
        slot = i % 2
        pltpu.make_async_copy(x_hbm.at[pl.ds(0, MOE_TILE)], xbuf.at[slot], sem.at[slot]).wait()
        if prefetch:
            for r in range(MOE_TILE):
                row_copy(i + 1, 1 - slot, r).start()
        x = xbuf[slot].astype(MXU_DTYPE)
        hid = _dot(x, wgu_ref[0]) + bgu_ref[0]
        acts = []
        for c in range(hid.shape[1] // GLU_GROUP):
            h_glu = jnp.minimum(hid[:, c * GLU_GROUP:c * GLU_GROUP + LANES], SWIGLU_LIMIT)
            h_lin = jnp.clip(hid[:, c * GLU_GROUP + LANES:(c + 1) * GLU_GROUP], -SWIGLU_LIMIT, SWIGLU_LIMIT)
            acts.append(h_glu * (1.0 / (1.0 + jnp.exp(-SWIGLU_ALPHA * h_glu))) * (h_lin + 1.0))
        act = jnp.concatenate(acts, axis=1)
        o_ref[...] = _dot(act.astype(MXU_DTYPE), wd_ref[0, 0]) + bd_ref[0]

    @pl.when(i + 1 < n_used)
    def _():
        expert_tile(True)

    @pl.when(i + 1 == n_used)
    def _():
        expert_tile(False)

    @pl.when(i >= n_used)
    def _():
        o_ref[...] = jnp.zeros(o_ref.shape, o_ref.dtype)


def _moe_experts(x1, row_token, tile_expert, n_used, wgu, bgu, wd_stack, layer, bd):
    t, d = x1.shape
    f2 = wgu.shape[2]
    n_tiles = tile_expert.shape[0]
    wmap = lambda i, rt, te, nu: (te[i], 0, 0)
    grid_spec = pltpu.PrefetchScalarGridSpec(
        num_scalar_prefetch=3,
        grid=(n_tiles,),
        in_specs=[pl.BlockSpec(memory_space=pl.ANY),
                  pl.BlockSpec((1, d, f2), wmap), pl.BlockSpec((1, 1, f2), wmap),
                  pl.BlockSpec((1, 1, f2 // 2, d), lambda i, rt, te, nu: (layer, te[i], 0, 0)),
                  pl.BlockSpec((1, 1, d), wmap)],
        out_specs=pl.BlockSpec((MOE_TILE, d), lambda i, rt, te, nu: (i, 0)),
        scratch_shapes=[pltpu.VMEM((2, MOE_TILE, d), jnp.float32), pltpu.SemaphoreType.DMA((2,))])
    return pl.pallas_call(
        _moe_kernel,
        out_shape=jax.ShapeDtypeStruct((n_tiles * MOE_TILE, d), jnp.float32),
        grid_spec=grid_spec, compiler_params=_params("arbitrary"),
    )(row_token, tile_expert, n_used, x1, wgu, bgu, wd_stack, bd)


def _combine_kernel(pos_ref, y_hbm, x_ref, gate_ref, g_ref, b_ref, o_ref, ob_ref, ybuf, sem, *, alpha):
    i = pl.program_id(0)
    n_rows = COMBINE_TILE * TOP_K

    def row_copy(tile, slot, r):
        src = pos_ref[tile * n_rows + r]
        return pltpu.make_async_copy(y_hbm.at[pl.ds(src, 1)], ybuf.at[slot, pl.ds(r, 1)], sem.at[slot])

    @pl.when(i == 0)
    def _():
        def body(r, carry):
            row_copy(0, 0, r).start()
            return carry
        lax.fori_loop(0, n_rows, body, 0, unroll=GATHER_UNROLL)

    def token_tile(prefetch):
        slot = i % 2
        pltpu.make_async_copy(y_hbm.at[pl.ds(0, n_rows)], ybuf.at[slot], sem.at[slot]).wait()
        if prefetch:
            for r in range(n_rows):
                row_copy(i + 1, 1 - slot, r).start()
        gate = gate_ref[...]
        ffn = jnp.zeros(x_ref.shape, jnp.float32)
        for k in range(TOP_K):
            ffn = ffn + gate[:, k:k + 1] * ybuf[slot, pl.ds(k * COMBINE_TILE, COMBINE_TILE)]
        x2 = _layer_norm(alpha * x_ref[...] + ffn, g_ref[...], b_ref[...])
        o_ref[...] = x2
        ob_ref[...] = x2.astype(ob_ref.dtype)

    last = pl.num_programs(0) - 1

    @pl.when(i < last)
    def _():
        token_tile(True)

    @pl.when(i == last)
    def _():
        token_tile(False)


def _combine_ln(y, pos, x1, gate, ln_g, ln_b, alpha):
    t, d = x1.shape
    row = lambda i, p: (i, 0)
    const = lambda i, p: (0, 0)
    grid_spec = pltpu.PrefetchScalarGridSpec(
        num_scalar_prefetch=1,
        grid=(t // COMBINE_TILE,),
        in_specs=[pl.BlockSpec(memory_space=pl.ANY),
                  pl.BlockSpec((COMBINE_TILE, d), row), pl.BlockSpec((COMBINE_TILE, LANES), row),
                  pl.BlockSpec((1, d), const), pl.BlockSpec((1, d), const)],
        out_specs=(pl.BlockSpec((COMBINE_TILE, d), row), pl.BlockSpec((COMBINE_TILE, d), row)),
        scratch_shapes=[pltpu.VMEM((2, COMBINE_TILE * TOP_K, d), jnp.float32),
                        pltpu.SemaphoreType.DMA((2,))])
    return pl.pallas_call(
        functools.partial(_combine_kernel, alpha=alpha),
        out_shape=(jax.ShapeDtypeStruct((t, d), jnp.float32), jax.ShapeDtypeStruct((t, d), MXU_DTYPE)),
        grid_spec=grid_spec, compiler_params=_params("arbitrary"),
    )(pos, y, x1, gate, ln_g.reshape(1, d), ln_b.reshape(1, d))


def _route(top_idx, n_experts):
    t = top_idx.shape[0]
    n_assign = t * TOP_K
    n_tiles = n_assign // MOE_TILE + n_experts
    expert = top_idx.reshape(n_assign)
    onehot = (expert[:, None] == jnp.arange(n_experts, dtype=jnp.int32)[None, :]).astype(jnp.int32)
    running = jnp.cumsum(onehot, axis=0)
    counts = running[-1]
    rank = jnp.take_along_axis(running, expert[:, None], axis=1)[:, 0] - 1
    padded = (counts + MOE_TILE - 1) // MOE_TILE * MOE_TILE
    pad_end = jnp.cumsum(padded)
    dest = (pad_end - padded)[expert] + rank
    slot = dest.reshape(t // COMBINE_TILE, COMBINE_TILE, TOP_K).transpose(0, 2, 1).reshape(n_assign)
    token = jnp.arange(n_assign, dtype=jnp.int32) // TOP_K
    row_token = jnp.zeros((n_tiles * MOE_TILE,), jnp.int32).at[dest].set(token)
    tile_start = jnp.arange(n_tiles, dtype=jnp.int32) * MOE_TILE
    tile_expert = jnp.minimum(
        jnp.sum((pad_end[None, :] <= tile_start[:, None]).astype(jnp.int32), axis=1), n_experts - 1)
    n_used = (pad_end[-1:] // MOE_TILE).astype(jnp.int32)
    return row_token, tile_expert, n_used, slot.astype(jnp.int32)


def _rope_tables(positions):
    inv_freq = 1.0 / (ROPE_THETA ** (jnp.arange(0, ROT_DIM, 2, dtype=jnp.float32) / ROT_DIM))
    ang = positions.astype(jnp.float32).reshape(-1)[:, None] * inv_freq
    cos, sin = jnp.cos(ang), jnp.sin(ang)
    rest = HEAD_DIM - ROT_DIM
    n = ang.shape[0]
    ct = jnp.concatenate([cos, cos, jnp.ones((n, rest), jnp.float32)], axis=1)
    sa = jnp.concatenate([jnp.zeros_like(sin), sin, jnp.zeros((n, rest), jnp.float32)], axis=1)
    sb = jnp.concatenate([-sin, jnp.zeros_like(sin), jnp.zeros((n, rest), jnp.float32)], axis=1)
    return ct, sa, sb


def _gate_weight(w_in_l, lay):
    d = w_in_l.shape[0]
    kv = lay['kv']
    o = lay['ref_off']['ngate']
    wg = w_in_l[:, o:o + lay['size']['ngate']].reshape(d, kv, NSA_GROUP * 3)
    wg = jnp.pad(wg, ((0, 0), (0, 0), (0, LANES - NSA_GROUP * 3)))
    return wg.reshape(d, kv * LANES).astype(MXU_DTYPE)


def kernel(x, positions, w_in, nsa_cmp_pos, nsa_cmp_w1, nsa_cmp_w2, diff_lambda, diff_subln_g, w_out,
           ln1_g, ln1_b, w_router, b_router, w_gate_up, b_gate_up, w_down, b_down, ln2_g, ln2_b):
    b, s, d = x.shape
    depth = w_in.shape[0]
    n_experts = w_router.shape[2]
    lay = _layout(d)
    alpha = (2 * depth) ** 0.25
    ct, sa, sb = _rope_tables(positions)
    xf = x.reshape(b * s, d).astype(jnp.float32)
    xb = xf.astype(MXU_DTYPE)
    w_out_b = w_out.astype(MXU_DTYPE)
    w_down_b = w_down.astype(MXU_DTYPE)
    for layer in range(depth):
        w_l = w_in[layer]
        w_perm = jnp.concatenate(
            [w_l[:, lay['ref_off'][n]:lay['ref_off'][n] + lay['size'][n]] for n in ROPED + PLAIN],
            axis=1).astype(MXU_DTYPE)
        hp = _project(xb, w_perm, ct, sa, sb, lay['n_roped'])
        gate_logits = _gate_logits(xb, _gate_weight(w_l, lay))

        kcv = _nsa_compress(hp, nsa_cmp_pos[layer], nsa_cmp_w1[layer], nsa_cmp_w2[layer], lay, b, s)
        o_cmp, sel = _nsa_cmp_select(hp, kcv, lay, b, s)
        y_nsa = _nsa_main(hp, sel, o_cmp, gate_logits, lay, b, s)
        lambda_init = 0.8 - 0.6 * math.exp(-0.3 * layer)
        y_diff = _diff_attention(hp, diff_lambda[layer], diff_subln_g[layer], lay, b, s, lambda_init)
        y_moba = _moba_attention(hp, lay, b, s)
        mix = jnp.concatenate([y_nsa, y_diff, y_moba], axis=1)

        x1, top_idx, top_gate = _out_proj_ln_router(
            mix, w_out_b, layer, xf, ln1_g[layer], ln1_b[layer],
            w_router[layer], b_router[layer], alpha)

        row_token, tile_expert, n_used, slot = _route(top_idx[:, :TOP_K], n_experts)
        y = _moe_experts(
            x1, row_token, tile_expert, n_used,
            _regroup_gate_up(w_gate_up, layer), _regroup_bias(b_gate_up[layer]),
            w_down_b, layer, b_down[layer].astype(jnp.float32).reshape(n_experts, 1, d))
        xf, xb = _combine_ln(y, slot, x1, top_gate, ln2_g[layer], ln2_b[layer], alpha)
    return xf.reshape(b, s, d).astype(x.dtype)
```

```python
import functools
import math

import numpy as np
import jax
import jax.numpy as jnp
from jax import lax
from jax.experimental import pallas as pl
from jax.experimental.pallas import tpu as pltpu

HEAD_DIM = 128
ROT_DIM = HEAD_DIM // 4
ROT_HALF = ROT_DIM // 2
ROPE_THETA = 500000.0
ATTN_SCALE = HEAD_DIM ** -0.5
NEG_INF = -1e30
LN_EPS = 1e-5

NSA_GROUP = 4
CMP_LEN = 32
CMP_STRIDE = 16
SEL_LEN = 64
SEL_TOPN = 16
SEL_FORCED = 3
WINDOW = 512
CMP_PER_SEL = SEL_LEN // CMP_STRIDE

DIFF_VDIM = 2 * HEAD_DIM
MOBA_BLOCK = 256
MOBA_TOPK = 3

TOP_K = 4
SWIGLU_LIMIT = 7.0
SWIGLU_ALPHA = 1.702

LANES = 128
MXU_DTYPE = jnp.bfloat16
VMEM_LIMIT = 56 * 1024 * 1024

ROPED = ('nq', 'nkc', 'nks', 'nkw', 'dq', 'dk', 'mq', 'mk')
PLAIN = ('nvc', 'nvs', 'nvw', 'dv', 'mv')
REF_ORDER = ('nq', 'nkc', 'nvc', 'nks', 'nvs', 'nkw', 'nvw', 'ngate', 'dq', 'dk', 'dv', 'mq', 'mk', 'mv')


def _layout(d):
    nsa_h = d // 256
    kv = nsa_h // NSA_GROUP
    diff_h = d // 1024
    moba_h = d // 512
    size = dict(nq=nsa_h * HEAD_DIM, nkc=kv * HEAD_DIM, nvc=kv * HEAD_DIM, nks=kv * HEAD_DIM,
                nvs=kv * HEAD_DIM, nkw=kv * HEAD_DIM, nvw=kv * HEAD_DIM, ngate=3 * nsa_h,
                dq=2 * diff_h * HEAD_DIM, dk=2 * diff_h * HEAD_DIM, dv=diff_h * DIFF_VDIM,
                mq=moba_h * HEAD_DIM, mk=moba_h * HEAD_DIM, mv=moba_h * HEAD_DIM)
    ref_off, o = {}, 0
    for n in REF_ORDER:
        ref_off[n] = o
        o += size[n]
    off, o = {}, 0
    for n in ROPED + PLAIN:
        off[n] = o
        o += size[n]
    n_roped = sum(size[n] for n in ROPED)
    return dict(nsa_h=nsa_h, kv=kv, diff_h=diff_h, moba_h=moba_h, size=size, ref_off=ref_off,
                off=off, n_roped=n_roped, n_cols=o)


def _params(*sem):
    return pltpu.CompilerParams(dimension_semantics=sem, vmem_limit_bytes=VMEM_LIMIT)


def _dot(a, b):
    return jnp.dot(a, b, preferred_element_type=jnp.float32)


def _dot_nt(a, b):
    return lax.dot_general(a, b, (((1,), (1,)), ((), ())), preferred_element_type=jnp.float32)


def _iota(shape, dim):
    return lax.broadcasted_iota(jnp.int32, shape, dim)


def _lane_index(shape):
    return _iota(shape, 1).astype(jnp.float32)


def _first_lane(hit, lane):
    return jnp.min(jnp.where(hit, lane, float(LANES)), axis=-1, keepdims=True)


def _proj_kernel(x_ref, w_ref, ct_ref, sa_ref, sb_ref, o_ref, *, n_rope_blocks, tn):
    j = pl.program_id(1)
    acc = _dot(x_ref[...], w_ref[...])

    @pl.when(j < n_rope_blocks)
    def _():
        ct, sa, sb = ct_ref[...], sa_ref[...], sb_ref[...]
        for c in range(tn // HEAD_DIM):
            a = acc[:, c * HEAD_DIM:(c + 1) * HEAD_DIM]
            r = (a * ct + pltpu.roll(a, ROT_HALF, 1) * sa
                 + pltpu.roll(a, HEAD_DIM - ROT_HALF, 1) * sb)
            o_ref[:, c * HEAD_DIM:(c + 1) * HEAD_DIM] = r.astype(o_ref.dtype)

    @pl.when(j >= n_rope_blocks)
    def _():
        o_ref[...] = acc.astype(o_ref.dtype)


def _project(xb, w, ct, sa, sb, n_roped):
    m, k = xb.shape
    n = w.shape[1]
    tm = min(1024, m)
    tn = next(t for t in (512, 256, 128) if n % t == 0 and n_roped % t == 0)
    kern = functools.partial(_proj_kernel, n_rope_blocks=n_roped // tn, tn=tn)
    return pl.pallas_call(
        kern,
        out_shape=jax.ShapeDtypeStruct((m, n), MXU_DTYPE),
        grid=(m // tm, n // tn),
        in_specs=[pl.BlockSpec((tm, k), lambda i, j: (i, 0)),
                  pl.BlockSpec((k, tn), lambda i, j: (0, j)),
                  pl.BlockSpec((tm, HEAD_DIM), lambda i, j: (i, 0)),
                  pl.BlockSpec((tm, HEAD_DIM), lambda i, j: (i, 0)),
                  pl.BlockSpec((tm, HEAD_DIM), lambda i, j: (i, 0))],
        out_specs=pl.BlockSpec((tm, tn), lambda i, j: (i, j)),
        compiler_params=_params("parallel", "arbitrary"),
    )(xb, w, ct, sa, sb)


def _gate_kernel(x_ref, w_ref, o_ref):
    o_ref[...] = _dot(x_ref[...], w_ref[...])


def _gate_logits(xb, wg):
    m, k = xb.shape
    n = wg.shape[1]
    tm = min(1024, m)
    return pl.pallas_call(
        _gate_kernel,
        out_shape=jax.ShapeDtypeStruct((m, n), jnp.float32),
        grid=(m // tm,),
        in_specs=[pl.BlockSpec((tm, k), lambda i: (i, 0)),
                  pl.BlockSpec((k, n), lambda i: (0, 0))],
        out_specs=pl.BlockSpec((tm, n), lambda i: (i, 0)),
        compiler_params=_params("parallel"),
    )(xb, wg)


EXP2_SCALE = ATTN_SCALE * math.log2(math.e)


def _mask_bias(mask):
    return jnp.where(mask, 0.0, -jnp.inf)


def _lanes(x, n):
    return x if n == LANES else jnp.concatenate([x] * (n // LANES), axis=1)


def _flash_step(s, v, m_ref, acc_ref):
    dv = v.shape[1]
    m_old = m_ref[...]
    m_new = jnp.maximum(m_old, jnp.max(s, axis=-1, keepdims=True))
    p = jnp.exp2((s - _lanes(m_new, s.shape[1])) * EXP2_SCALE)
    alpha = jnp.exp2((m_old - m_new) * EXP2_SCALE)
    acc_ref[:, :dv] = _lanes(alpha, dv) * acc_ref[:, :dv] + _dot(p.astype(v.dtype), v)
    acc_ref[:, dv:] = alpha * acc_ref[:, dv:] + jnp.sum(p, axis=-1, keepdims=True)
    m_ref[...] = m_new


def _flash_result(acc_ref, rows):
    dv = acc_ref.shape[1] - LANES
    denom = jnp.maximum(acc_ref[rows, dv:], 1e-30)
    return acc_ref[rows, :dv] / _lanes(denom, dv)


def _init_state(m_ref, acc_ref):
    m_ref[...] = jnp.full(m_ref.shape, NEG_INF, jnp.float32)
    acc_ref[...] = jnp.zeros(acc_ref.shape, jnp.float32)


def _key_rows(ki, tk):
    return pl.ds(pl.multiple_of(ki * tk, tk), tk)


def _for_tiles(lo, hi, body):
    def wrapped(ki, carry):
        body(ki)
        return carry
    lax.fori_loop(lo, hi, wrapped, 0)


def _diff_kernel(lam_ref, g_ref, q_ref, k_ref, v_ref, o_ref, m_ref, acc_ref, *, tq, lambda_init):
    qi = pl.program_id(2)
    tk = tq
    _init_state(m_ref, acc_ref)

    def tile(ki, masked):
        rows = _key_rows(ki, tk)
        maps = []
        for mp in range(2):
            q = q_ref[:, mp * HEAD_DIM:(mp + 1) * HEAD_DIM]
            maps.append(_dot_nt(q, k_ref[rows, mp * HEAD_DIM:(mp + 1) * HEAD_DIM]))
        if masked:
            bias = _mask_bias(_iota((tq, tk), 1) <= _iota((tq, tk), 0))
            maps = [s + bias for s in maps]
        _flash_step(jnp.concatenate(maps, axis=0), v_ref[rows, :], m_ref, acc_ref)

    _for_tiles(0, qi, lambda ki: tile(ki, False))
    tile(qi, True)

    lv = lam_ref[...]
    lam = (jnp.exp(jnp.sum(lv[0:1] * lv[1:2], axis=-1, keepdims=True))
           - jnp.exp(jnp.sum(lv[2:3] * lv[3:4], axis=-1, keepdims=True)) + lambda_init)
    o = _flash_result(acc_ref, pl.ds(0, tq)) - lam * _flash_result(acc_ref, pl.ds(tq, tq))
    o = o * lax.rsqrt(jnp.mean(jnp.square(o), axis=-1, keepdims=True) + LN_EPS) * g_ref[...]
    o_ref[...] = (o * (1.0 - lambda_init)).astype(o_ref.dtype)


def _diff_attention(hp, lam_vecs, subln_g, lay, b, s, lambda_init):
    hd = lay['diff_h']
    tq = min(512, s)
    nq = s // tq
    qb, kb, vb = (lay['off'][n] // DIFF_VDIM for n in ('dq', 'dk', 'dv'))
    assert all(lay['off'][n] % DIFF_VDIM == 0 for n in ('dq', 'dk', 'dv'))
    kern = functools.partial(_diff_kernel, tq=tq, lambda_init=lambda_init)
    return pl.pallas_call(
        kern, out_shape=jax.ShapeDtypeStruct((b * s, hd * DIFF_VDIM), MXU_DTYPE),
        grid=(b, hd, nq),
        in_specs=[pl.BlockSpec((4, HEAD_DIM), lambda bi, h, qi: (0, 0)),
                  pl.BlockSpec((1, DIFF_VDIM), lambda bi, h, qi: (0, 0)),
                  pl.BlockSpec((tq, DIFF_VDIM), lambda bi, h, qi: (bi * nq + qi, qb + h)),
                  pl.BlockSpec((s, DIFF_VDIM), lambda bi, h, qi: (bi, kb + h)),
                  pl.BlockSpec((s, DIFF_VDIM), lambda bi, h, qi: (bi, vb + h))],
        out_specs=pl.BlockSpec((tq, DIFF_VDIM), lambda bi, h, qi: (bi * nq + qi, h)),
        scratch_shapes=[pltpu.VMEM((2 * tq, LANES), jnp.float32),
                        pltpu.VMEM((2 * tq, DIFF_VDIM + LANES), jnp.float32)],
        compiler_params=_params("parallel", "parallel", "arbitrary"),
    )(lam_vecs.astype(jnp.float32), subln_g.reshape(1, DIFF_VDIM).astype(jnp.float32), hp, hp, hp)


def _kmean_kernel(k_ref, o_ref, *, n_blk):
    k = k_ref[...].astype(jnp.float32).reshape(n_blk, MOBA_BLOCK, HEAD_DIM)
    o_ref[0, 0] = jnp.mean(k, axis=1)


def _moba_kmean(hp, lay, b, s):
    h = lay['moba_h']
    n_blk = s // MOBA_BLOCK
    kb = lay['off']['mk'] // HEAD_DIM
    return pl.pallas_call(
        functools.partial(_kmean_kernel, n_blk=n_blk),
        out_shape=jax.ShapeDtypeStruct((b, h, n_blk, HEAD_DIM), jnp.float32),
        grid=(b, h),
        in_specs=[pl.BlockSpec((s, HEAD_DIM), lambda bi, hi: (bi, kb + hi))],
        out_specs=pl.BlockSpec((1, 1, n_blk, HEAD_DIM), lambda bi, hi: (bi, hi, 0, 0)),
        compiler_params=_params("parallel", "parallel"),
    )(hp)


def _block_expansion(s, block):
    return (np.arange(s)[None, :] // block == np.arange(LANES)[:, None]).astype(np.float32)


def _lane_column(x, n):
    return jnp.sum(jnp.where(_iota(x.shape, 1) == n, x, 0.0), axis=-1, keepdims=True)


def _moba_kernel(q_ref, k_ref, v_ref, km_ref, o_ref, bias_ref, m_ref, acc_ref, *, tq, tk):
    qi = pl.program_id(2)
    q = q_ref[...]
    _init_state(m_ref, acc_ref)
    score = _dot_nt(q, km_ref[0, 0].astype(q.dtype))
    blk = _lane_index((tq, LANES))
    own = ((qi * tq + _iota((tq, LANES), 0)) // MOBA_BLOCK).astype(jnp.float32)
    work = jnp.where(blk < own, score, -jnp.inf)
    sel = jnp.where(blk == own, 1.0, 0.0)
    for r in range(MOBA_TOPK):
        mx = jnp.max(work, axis=-1, keepdims=True)
        first = _first_lane(work == mx, blk)
        pick = blk == first
        sel = jnp.where(pick & (own > r), 1.0, sel)
        work = jnp.where(pick, -jnp.inf, work)
    bias_ref[...] = _mask_bias(sel > 0.5)

    def tile(ki, causal):
        rows = _key_rows(ki, tk)
        cols = []
        for c in range(tk // MOBA_BLOCK):
            col = _lane_column(bias_ref[...], ki * (tk // MOBA_BLOCK) + c)
            cols.append(jnp.broadcast_to(col, (tq, MOBA_BLOCK)))
        bias = cols[0] if len(cols) == 1 else jnp.concatenate(cols, axis=1)
        if causal:
            t = qi * tq + _iota((tq, tk), 0)
            u = ki * tk + _iota((tq, tk), 1)
            bias = jnp.where(u <= t, bias, -jnp.inf)
        _flash_step(_dot_nt(q, k_ref[rows, :]) + bias, v_ref[rows, :], m_ref, acc_ref)

    past = qi * (tq // tk)
    _for_tiles(0, past, lambda ki: tile(ki, False))
    for d in range(tq // tk):
        tile(past + d, True)
    o_ref[...] = _flash_result(acc_ref, pl.ds(0, tq)).astype(o_ref.dtype)


def _moba_attention(hp, lay, b, s):
    h = lay['moba_h']
    n_blk = s // MOBA_BLOCK
    assert s % MOBA_BLOCK == 0 and MOBA_TOPK <= n_blk <= LANES
    km = _moba_kmean(hp, lay, b, s)
    km = jnp.pad(km, ((0, 0), (0, 0), (0, LANES - n_blk), (0, 0)))
    tq = min(1024, s)
    tk = min(512, s)
    nq = s // tq
    assert tq % tk == 0 and tk % MOBA_BLOCK == 0
    qb, kb, vb = (lay['off'][n] // HEAD_DIM for n in ('mq', 'mk', 'mv'))
    return pl.pallas_call(
        functools.partial(_moba_kernel, tq=tq, tk=tk),
        out_shape=jax.ShapeDtypeStruct((b * s, h * HEAD_DIM), MXU_DTYPE),
        grid=(b, h, nq),
        in_specs=[pl.BlockSpec((tq, HEAD_DIM), lambda bi, hi, qi: (bi * nq + qi, qb + hi)),
                  pl.BlockSpec((s, HEAD_DIM), lambda bi, hi, qi: (bi, kb + hi)),
                  pl.BlockSpec((s, HEAD_DIM), lambda bi, hi, qi: (bi, vb + hi)),
                  pl.BlockSpec((1, 1, LANES, HEAD_DIM), lambda bi, hi, qi: (bi, hi, 0, 0))],
        out_specs=pl.BlockSpec((tq, HEAD_DIM), lambda bi, hi, qi: (bi * nq + qi, hi)),
        scratch_shapes=[pltpu.VMEM((tq, LANES), jnp.float32), pltpu.VMEM((tq, LANES), jnp.float32),
                        pltpu.VMEM((tq, HEAD_DIM + LANES), jnp.float32)],
        compiler_params=_params("parallel", "parallel", "arbitrary"),
    )(hp, hp, hp, km)


def _gelu_tanh(x):
    return 0.5 * x * (1.0 + jnp.tanh(math.sqrt(2.0 / math.pi) * (x + 0.044715 * (x * x * x))))


def _compress_kernel(seg_ref, plo_ref, phi_ref, w1lo_ref, w1hi_ref, w2_ref, o_ref, *, n_seg):
    seg = seg_ref[0, 0, 0].astype(jnp.float32)
    lo = _dot((seg + plo_ref[0]).astype(MXU_DTYPE), w1lo_ref[0])
    hi = _dot((seg + phi_ref[0]).astype(MXU_DTYPE), w1hi_ref[0])
    pre = lo + pltpu.roll(hi, n_seg - 1, 0)
    o_ref[0, 0, 0] = _dot(_gelu_tanh(pre).astype(MXU_DTYPE), w2_ref[0]).astype(o_ref.dtype)


def _nsa_compress(hp, cmp_pos, cmp_w1, cmp_w2, lay, b, s):
    kv = lay['kv']
    n_seg = s // CMP_STRIDE
    half = CMP_STRIDE * HEAD_DIM

    def segments(name):
        o = lay['off'][name]
        t = hp[:, o:o + kv * HEAD_DIM].reshape(b, n_seg, CMP_STRIDE, kv, HEAD_DIM)
        return t.transpose(0, 3, 1, 2, 4).reshape(b, kv, n_seg, half)

    seg = jnp.stack([segments('nkc'), segments('nvc')], axis=1)
    pos = cmp_pos.astype(jnp.float32).reshape(2, CMP_LEN * HEAD_DIM)
    plo = pos[:, :half].reshape(2, 1, half)
    phi = pos[:, half:].reshape(2, 1, half)
    w1 = cmp_w1.astype(MXU_DTYPE)
    w1lo, w1hi = w1[:, :half], w1[:, half:]
    w2 = cmp_w2.astype(MXU_DTYPE)
    return pl.pallas_call(
        functools.partial(_compress_kernel, n_seg=n_seg),
        out_shape=jax.ShapeDtypeStruct((b, 2, kv, n_seg, HEAD_DIM), MXU_DTYPE),
        grid=(b, 2, kv),
        in_specs=[pl.BlockSpec((1, 1, 1, n_seg, half), lambda bi, c, k: (bi, c, k, 0, 0)),
                  pl.BlockSpec((1, 1, half), lambda bi, c, k: (c, 0, 0)),
                  pl.BlockSpec((1, 1, half), lambda bi, c, k: (c, 0, 0)),
                  pl.BlockSpec((1, half, HEAD_DIM), lambda bi, c, k: (c, 0, 0)),
                  pl.BlockSpec((1, half, HEAD_DIM), lambda bi, c, k: (c, 0, 0)),
                  pl.BlockSpec((1, HEAD_DIM, HEAD_DIM), lambda bi, c, k: (c, 0, 0))],
        out_specs=pl.BlockSpec((1, 1, 1, n_seg, HEAD_DIM), lambda bi, c, k: (bi, c, k, 0, 0)),
        compiler_params=_params("parallel", "parallel", "parallel"),
    )(seg, plo, phi, w1lo, w1hi, w2)


def _nsa_cmp_kernel(q_ref, kc_ref, vc_ref, o_ref, sel_ref, *, tq, n_sel, n_cmp):
    qi = pl.program_id(2)
    width = CMP_PER_SEL * LANES
    t = qi * tq + _iota((tq, width), 0)
    pos = _iota((tq, width), 1)
    m_idx = pos % LANES
    n_idx = CMP_PER_SEL * m_idx + pos // LANES
    valid = (m_idx < n_sel) & (n_idx < n_cmp) & (n_idx * CMP_STRIDE + CMP_LEN - 1 <= t)
    kc = kc_ref[0, 0, 0]
    vc = vc_ref[0, 0, 0]
    p_grp = jnp.zeros((tq, width), jnp.float32)
    for g in range(NSA_GROUP):
        q = q_ref[:, g * HEAD_DIM:(g + 1) * HEAD_DIM]
        s = jnp.where(valid, _dot_nt(q, kc) * ATTN_SCALE, NEG_INF)
        p = jnp.where(valid, jnp.exp(s - jnp.max(s, axis=-1, keepdims=True)), 0.0)
        p = p / jnp.maximum(jnp.sum(p, axis=-1, keepdims=True), 1e-30)
        o_ref[:, g * HEAD_DIM:(g + 1) * HEAD_DIM] = _dot(p.astype(vc.dtype), vc)
        p_grp = p_grp + p

    slabs = [p_grp[:, r * LANES:(r + 1) * LANES] for r in range(CMP_PER_SEL)]
    blk = _lane_index((tq, LANES))
    prev = jnp.where(blk == 0, 0.0, pltpu.roll(slabs[CMP_PER_SEL - 1], 1, 1))
    imp = prev
    for r in range(CMP_PER_SEL):
        imp = imp + slabs[r]
    cur = ((qi * tq + _iota((tq, LANES), 0)) // SEL_LEN).astype(jnp.float32)
    forced = (blk == 0) | (blk == cur) | (blk == cur - 1)
    work = jnp.where(forced | (blk > cur), -jnp.inf, imp)
    sel = jnp.where(forced, 1.0, 0.0)
    for _ in range(SEL_TOPN - SEL_FORCED):
        mx = jnp.max(work, axis=-1, keepdims=True)
        first = _first_lane(work == mx, blk)
        pick = blk == first
        sel = jnp.where(pick, 1.0, sel)
        work = jnp.where(pick, -jnp.inf, work)
    sel_ref[0, 0] = jnp.where(blk <= cur, sel, 0.0).astype(sel_ref.dtype)


def _nsa_cmp_select(hp, kcv, lay, b, s):
    kv = lay['kv']
    n_seg = s // CMP_STRIDE
    n_sel = s // SEL_LEN
    n_cmp = (s - CMP_LEN) // CMP_STRIDE + 1
    assert SEL_TOPN <= n_sel <= LANES
    width = CMP_PER_SEL * LANES
    kcp = kcv.reshape(b, 2, kv, n_sel, CMP_PER_SEL, HEAD_DIM).transpose(0, 1, 2, 4, 3, 5)
    kcp = jnp.pad(kcp, ((0, 0),) * 4 + ((0, LANES - n_sel), (0, 0))).reshape(b, 2, kv, width, HEAD_DIM)
    tq = min(512, s)
    nq = s // tq
    gw = NSA_GROUP * HEAD_DIM
    return pl.pallas_call(
        functools.partial(_nsa_cmp_kernel, tq=tq, n_sel=n_sel, n_cmp=n_cmp),
        out_shape=(jax.ShapeDtypeStruct((b * s, lay['nsa_h'] * HEAD_DIM), jnp.float32),
                   jax.ShapeDtypeStruct((b, kv, s, LANES), MXU_DTYPE)),
        grid=(b, kv, nq),
        in_specs=[pl.BlockSpec((tq, gw), lambda bi, k, qi: (bi * nq + qi, k)),
                  pl.BlockSpec((1, 1, 1, width, HEAD_DIM), lambda bi, k, qi: (bi, 0, k, 0, 0)),
                  pl.BlockSpec((1, 1, 1, width, HEAD_DIM), lambda bi, k, qi: (bi, 1, k, 0, 0))],
        out_specs=(pl.BlockSpec((tq, gw), lambda bi, k, qi: (bi * nq + qi, k)),
                   pl.BlockSpec((1, 1, tq, LANES), lambda bi, k, qi: (bi, k, qi, 0))),
        compiler_params=_params("parallel", "parallel", "parallel"),
    )(hp, kcp, kcp)


def _nsa_main_kernel(q_ref, ks_ref, vs_ref, kw_ref, vw_ref, sel_ref, e_ref, oc_ref, gl_ref, o_ref,
                     ms_ref, as_ref, mw_ref, aw_ref, *, tq, tk):
    qi = pl.program_id(2)
    last = (qi * tq) // tk
    _init_state(ms_ref, as_ref)
    _init_state(mw_ref, aw_ref)

    q_all = jnp.concatenate([q_ref[:, g * HEAD_DIM:(g + 1) * HEAD_DIM] for g in range(NSA_GROUP)], axis=0)

    def scores(k, bias):
        keys = k.shape[0]
        s = _dot_nt(q_all, k).reshape(NSA_GROUP, tq, keys) + bias[None]
        return s.reshape(NSA_GROUP * tq, keys)

    def selected(ki, causal):
        rows = _key_rows(ki, tk)
        ok = _dot(sel_ref[0, 0], e_ref[ki]) > 0.5
        if causal:
            ok = ok & (ki * tk + _iota((tq, tk), 1) <= qi * tq + _iota((tq, tk), 0))
        _flash_step(scores(ks_ref[rows, :], _mask_bias(ok)), vs_ref[rows, :], ms_ref, as_ref)

    _for_tiles(0, last, lambda ki: selected(ki, False))
    selected(last, True)

    span = WINDOW + tq
    start = pl.multiple_of(jnp.maximum(qi * tq - WINDOW, 0), tq)
    t = qi * tq + _iota((tq, span), 0)
    u = start + _iota((tq, span), 1)
    b_win = _mask_bias((u <= t) & (u > t - WINDOW))
    slab = pl.ds(start, span)
    _flash_step(scores(kw_ref[slab, :], b_win), vw_ref[slab, :], mw_ref, aw_ref)

    gate = 1.0 / (1.0 + jnp.exp(-gl_ref[...]))
    for g in range(NSA_GROUP):
        o_cmp = oc_ref[:, g * HEAD_DIM:(g + 1) * HEAD_DIM]
        o_slc = _flash_result(as_ref, pl.ds(g * tq, tq))
        o_win = _flash_result(aw_ref, pl.ds(g * tq, tq))
        out = (gate[:, 3 * g:3 * g + 1] * o_cmp + gate[:, 3 * g + 1:3 * g + 2] * o_slc
               + gate[:, 3 * g + 2:3 * g + 3] * o_win)
        o_ref[:, g * HEAD_DIM:(g + 1) * HEAD_DIM] = out.astype(o_ref.dtype)


def _nsa_main(hp, sel, o_cmp, gate_logits, lay, b, s):
    kv = lay['kv']
    tq = min(256, s)
    tk = min(512, s)
    nq, nk = s // tq, s // tk
    assert tk % tq == 0 and WINDOW % tq == 0 and s >= WINDOW + tq
    gw = NSA_GROUP * HEAD_DIM
    ksb, vsb, kwb, vwb = (lay['off'][n] // HEAD_DIM for n in ('nks', 'nvs', 'nkw', 'nvw'))
    expand = _block_expansion(s, SEL_LEN).reshape(LANES, nk, tk).transpose(1, 0, 2)
    expand = jnp.asarray(expand, MXU_DTYPE)
    row = lambda bi, k, qi: (bi * nq + qi, k)
    return pl.pallas_call(
        functools.partial(_nsa_main_kernel, tq=tq, tk=tk),
        out_shape=jax.ShapeDtypeStruct((b * s, lay['nsa_h'] * HEAD_DIM), MXU_DTYPE),
        grid=(b, kv, nq),
        in_specs=[pl.BlockSpec((tq, gw), row),
                  pl.BlockSpec((s, HEAD_DIM), lambda bi, k, qi: (bi, ksb + k)),
                  pl.BlockSpec((s, HEAD_DIM), lambda bi, k, qi: (bi, vsb + k)),
                  pl.BlockSpec((s, HEAD_DIM), lambda bi, k, qi: (bi, kwb + k)),
                  pl.BlockSpec((s, HEAD_DIM), lambda bi, k, qi: (bi, vwb + k)),
                  pl.BlockSpec((1, 1, tq, LANES), lambda bi, k, qi: (bi, k, qi, 0)),
                  pl.BlockSpec((nk, LANES, tk), lambda bi, k, qi: (0, 0, 0)),
                  pl.BlockSpec((tq, gw), row),
                  pl.BlockSpec((tq, LANES), row)],
        out_specs=pl.BlockSpec((tq, gw), row),
        scratch_shapes=[pltpu.VMEM((NSA_GROUP * tq, LANES), jnp.float32),
                        pltpu.VMEM((NSA_GROUP * tq, HEAD_DIM + LANES), jnp.float32)] * 2,
        compiler_params=_params("parallel", "parallel", "arbitrary"),
    )(hp, hp, hp, hp, hp, sel, expand, o_cmp, gate_logits)


def _layer_norm(y, g, b):
    mu = jnp.mean(y, axis=-1, keepdims=True)
    var = jnp.mean(jnp.square(y - mu), axis=-1, keepdims=True)
    return (y - mu) * lax.rsqrt(var + LN_EPS) * g + b


def _out_proj_kernel(mix_ref, w_ref, x_ref, g_ref, b_ref, wr_ref, br_ref, x1_ref, idx_ref, gate_ref,
                     acc_ref, *, alpha, n_experts):
    k = pl.program_id(1)

    @pl.when(k == 0)
    def _():
        acc_ref[...] = jnp.zeros(acc_ref.shape, jnp.float32)

    acc_ref[...] += _dot(mix_ref[...], w_ref[0])

    @pl.when(k == pl.num_programs(1) - 1)
    def _():
        x1 = _layer_norm(alpha * x_ref[...] + acc_ref[...], g_ref[...], b_ref[...])
        x1_ref[...] = x1
        logits = _dot(x1.astype(MXU_DTYPE), wr_ref[...]) + br_ref[...]
        lane = _lane_index(logits.shape)
        work = jnp.where(lane < n_experts, logits, -jnp.inf)
        idx_out = jnp.zeros(logits.shape, jnp.float32)
        val_out = jnp.zeros(logits.shape, jnp.float32)
        top = None
        for r in range(TOP_K):
            mx = jnp.max(work, axis=-1, keepdims=True)
            first = _first_lane(work == mx, lane)
            top = mx if top is None else top
            idx_out = jnp.where(lane == r, first, idx_out)
            val_out = jnp.where(lane == r, jnp.exp(mx - top), val_out)
            work = jnp.where(lane == first, -jnp.inf, work)
        idx_ref[...] = idx_out.astype(jnp.int32)
        gate_ref[...] = val_out / jnp.sum(val_out, axis=-1, keepdims=True)


def _out_proj_ln_router(mix, w_out_stack, layer, x, ln_g, ln_b, w_router, b_router, alpha):
    t, kdim = mix.shape
    d = w_out_stack.shape[2]
    n_experts = w_router.shape[1]
    tm = min(256, t)
    tk = min(512, kdim)
    wr = jnp.pad(w_router, ((0, 0), (0, LANES - n_experts))).astype(MXU_DTYPE)
    br = jnp.pad(b_router.astype(jnp.float32), (0, LANES - n_experts)).reshape(1, LANES)
    row = lambda i, k: (i, 0)
    const = lambda i, k: (0, 0)
    return pl.pallas_call(
        functools.partial(_out_proj_kernel, alpha=alpha, n_experts=n_experts),
        out_shape=(jax.ShapeDtypeStruct((t, d), jnp.float32),
                   jax.ShapeDtypeStruct((t, LANES), jnp.int32),
                   jax.ShapeDtypeStruct((t, LANES), jnp.float32)),
        grid=(t // tm, kdim // tk),
        in_specs=[pl.BlockSpec((tm, tk), lambda i, k: (i, k)),
                  pl.BlockSpec((1, tk, d), lambda i, k: (layer, k, 0)),
                  pl.BlockSpec((tm, d), row),
                  pl.BlockSpec((1, d), const), pl.BlockSpec((1, d), const),
                  pl.BlockSpec((d, LANES), const), pl.BlockSpec((1, LANES), const)],
        out_specs=(pl.BlockSpec((tm, d), row), pl.BlockSpec((tm, LANES), row),
                   pl.BlockSpec((tm, LANES), row)),
        scratch_shapes=[pltpu.VMEM((tm, d), jnp.float32)],
        compiler_params=_params("parallel", "arbitrary"),
    )(mix, w_out_stack, x, ln_g.reshape(1, d), ln_b.reshape(1, d), wr, br)


MOE_TILE = 256
COMBINE_TILE = 128
GATHER_UNROLL = 8


GLU_GROUP = 2 * LANES


def _regroup_kernel(w_ref, p_ref, o_ref):
    w = w_ref[0, 0].astype(MXU_DTYPE)
    for c in range(w.shape[1] // GLU_GROUP):
        cols = slice(c * GLU_GROUP, (c + 1) * GLU_GROUP)
        o_ref[0, :, cols] = _dot(w[:, cols], p_ref[...]).astype(o_ref.dtype)


def _regroup_gate_up(w_gate_up, layer):
    _, e, d, f2 = w_gate_up.shape
    j = np.arange(GLU_GROUP)
    src = np.where(j < LANES, 2 * j, 2 * (j - LANES) + 1)
    perm = np.zeros((GLU_GROUP, GLU_GROUP), np.float32)
    perm[src, j] = 1.0
    tk = min(1024, d)
    return pl.pallas_call(
        _regroup_kernel,
        out_shape=jax.ShapeDtypeStruct((e, d, f2), MXU_DTYPE),
        grid=(e, d // tk),
        in_specs=[pl.BlockSpec((1, 1, tk, f2), lambda ei, ki: (layer, ei, ki, 0)),
                  pl.BlockSpec((GLU_GROUP, GLU_GROUP), lambda ei, ki: (0, 0))],
        out_specs=pl.BlockSpec((1, tk, f2), lambda ei, ki: (ei, ki, 0)),
        compiler_params=_params("parallel", "parallel"),
    )(w_gate_up, jnp.asarray(perm, MXU_DTYPE))


def _regroup_bias(b_gate_up):
    e, f2 = b_gate_up.shape
    b = b_gate_up.astype(jnp.float32).reshape(e, f2 // GLU_GROUP, LANES, 2)
    return b.transpose(0, 1, 3, 2).reshape(e, 1, f2)


def _moe_kernel(rt_ref, te_ref, nu_ref, x_hbm, wgu_ref, bgu_ref, wd_ref, bd_ref, o_ref, xbuf, sem):
    i = pl.program_id(0)
    n_used = nu_ref[0]

    def row_copy(tile, slot, r):
        tok = rt_ref[tile * MOE_TILE + r]
        return pltpu.make_async_copy(x_hbm.at[pl.ds(tok, 1)], xbuf.at[slot, pl.ds(r, 1)], sem.at[slot])

    @pl.when(i == 0)
    def _():
        def body(r, carry):
            row_copy(0, 0, r).start()
            return carry
        lax.fori_loop(0, MOE_TILE, body, 0, unroll=GATHER_UNROLL)

    def expert_tile(prefetch):
        slot = i % 2
        pltpu.make_async_copy(x_hbm.at[pl.ds(0, MOE_TILE)], xbuf.at[slot], sem.at[slot]).wait()
        if prefetch:
            for r in range(MOE_TILE):
                row_copy(i + 1, 1 - slot, r).start()
        x = xbuf[slot].astype(MXU_DTYPE)
        hid = _dot(x, wgu_ref[0]) + bgu_ref[0]
        acts = []
        for c in range(hid.shape[1] // GLU_GROUP):
            h_glu = jnp.minimum(hid[:, c * GLU_GROUP:c * GLU_GROUP + LANES], SWIGLU_LIMIT)
            h_lin = jnp.clip(hid[:, c * GLU_GROUP + LANES:(c + 1) * GLU_GROUP], -SWIGLU_LIMIT, SWIGLU_LIMIT)
            acts.append(h_glu * (1.0 / (1.0 + jnp.exp(-SWIGLU_ALPHA * h_glu))) * (h_lin + 1.0))
        act = jnp.concatenate(acts, axis=1)
        o_ref[...] = _dot(act.astype(MXU_DTYPE), wd_ref[0, 0]) + bd_ref[0]

    @pl.when(i + 1 < n_used)
    def _():
        expert_tile(True)

    @pl.when(i + 1 == n_used)
    def _():
        expert_tile(False)

    @pl.when(i >= n_used)
    def _():
        o_ref[...] = jnp.zeros(o_ref.shape, o_ref.dtype)


def _moe_experts(x1, row_token, tile_expert, n_used, wgu, bgu, wd_stack, layer, bd):
    t, d = x1.shape
    f2 = wgu.shape[2]
    n_tiles = tile_expert.shape[0]
    wmap = lambda i, rt, te, nu: (te[i], 0, 0)
    grid_spec = pltpu.PrefetchScalarGridSpec(
        num_scalar_prefetch=3,
        grid=(n_tiles,),
        in_specs=[pl.BlockSpec(memory_space=pl.ANY),
                  pl.BlockSpec((1, d, f2), wmap), pl.BlockSpec((1, 1, f2), wmap),
                  pl.BlockSpec((1, 1, f2 // 2, d), lambda i, rt, te, nu: (layer, te[i], 0, 0)),
                  pl.BlockSpec((1, 1, d), wmap)],
        out_specs=pl.BlockSpec((MOE_TILE, d), lambda i, rt, te, nu: (i, 0)),
        scratch_shapes=[pltpu.VMEM((2, MOE_TILE, d), jnp.float32), pltpu.SemaphoreType.DMA((2,))])
    return pl.pallas_call(
        _moe_kernel,
        out_shape=jax.ShapeDtypeStruct((n_tiles * MOE_TILE, d), jnp.float32),
        grid_spec=grid_spec, compiler_params=_params("arbitrary"),
    )(row_token, tile_expert, n_used, x1, wgu, bgu, wd_stack, bd)


def _combine_kernel(pos_ref, y_hbm, x_ref, gate_ref, g_ref, b_ref, o_ref, ob_ref, ybuf, sem, *, alpha):
    i = pl.program_id(0)
    n_rows = COMBINE_TILE * TOP_K

    def row_copy(tile, slot, r):
        src = pos_ref[tile * n_rows + r]
        return pltpu.make_async_copy(y_hbm.at[pl.ds(src, 1)], ybuf.at[slot, pl.ds(r, 1)], sem.at[slot])

    @pl.when(i == 0)
    def _():
        def body(r, carry):
            row_copy(0, 0, r).start()
            return carry
        lax.fori_loop(0, n_rows, body, 0, unroll=GATHER_UNROLL)

    def token_tile(prefetch):
        slot = i % 2
        pltpu.make_async_copy(y_hbm.at[pl.ds(0, n_rows)], ybuf.at[slot], sem.at[slot]).wait()
        if prefetch:
            for r in range(n_rows):
                row_copy(i + 1, 1 - slot, r).start()
        gate = gate_ref[...]
        ffn = jnp.zeros(x_ref.shape, jnp.float32)
        for k in range(TOP_K):
            ffn = ffn + gate[:, k:k + 1] * ybuf[slot, pl.ds(k * COMBINE_TILE, COMBINE_TILE)]
        x2 = _layer_norm(alpha * x_ref[...] + ffn, g_ref[...], b_ref[...])
        o_ref[...] = x2
        ob_ref[...] = x2.astype(ob_ref.dtype)

    last = pl.num_programs(0) - 1

    @pl.when(i < last)
    def _():
        token_tile(True)

    @pl.when(i == last)
    def _():
        token_tile(False)


def _combine_ln(y, pos, x1, gate, ln_g, ln_b, alpha):
    t, d = x1.shape
    row = lambda i, p: (i, 0)
    const = lambda i, p: (0, 0)
    grid_spec = pltpu.PrefetchScalarGridSpec(
        num_scalar_prefetch=1,
        grid=(t // COMBINE_TILE,),
        in_specs=[pl.BlockSpec(memory_space=pl.ANY),
                  pl.BlockSpec((COMBINE_TILE, d), row), pl.BlockSpec((COMBINE_TILE, LANES), row),
                  pl.BlockSpec((1, d), const), pl.BlockSpec((1, d), const)],
        out_specs=(pl.BlockSpec((COMBINE_TILE, d), row), pl.BlockSpec((COMBINE_TILE, d), row)),
        scratch_shapes=[pltpu.VMEM((2, COMBINE_TILE * TOP_K, d), jnp.float32),
                        pltpu.SemaphoreType.DMA((2,))])
    return pl.pallas_call(
        functools.partial(_combine_kernel, alpha=alpha),
        out_shape=(jax.ShapeDtypeStruct((t, d), jnp.float32), jax.ShapeDtypeStruct((t, d), MXU_DTYPE)),
        grid_spec=grid_spec, compiler_params=_params("arbitrary"),
    )(pos, y, x1, gate, ln_g.reshape(1, d), ln_b.reshape(1, d))


def _route(top_idx, n_experts):
    t = top_idx.shape[0]
    n_assign = t * TOP_K
    n_tiles = n_assign // MOE_TILE + n_experts
    expert = top_idx.reshape(n_assign)
    onehot = (expert[:, None] == jnp.arange(n_experts, dtype=jnp.int32)[None, :]).astype(jnp.int32)
    running = jnp.cumsum(onehot, axis=0)
    counts = running[-1]
    rank = jnp.take_along_axis(running, expert[:, None], axis=1)[:, 0] - 1
    padded = (counts + MOE_TILE - 1) // MOE_TILE * MOE_TILE
    pad_end = jnp.cumsum(padded)
    dest = (pad_end - padded)[expert] + rank
    slot = dest.reshape(t // COMBINE_TILE, COMBINE_TILE, TOP_K).transpose(0, 2, 1).reshape(n_assign)
    token = jnp.arange(n_assign, dtype=jnp.int32) // TOP_K
    row_token = jnp.zeros((n_tiles * MOE_TILE,), jnp.int32).at[dest].set(token)
    tile_start = jnp.arange(n_tiles, dtype=jnp.int32) * MOE_TILE
    tile_expert = jnp.minimum(
        jnp.sum((pad_end[None, :] <= tile_start[:, None]).astype(jnp.int32), axis=1), n_experts - 1)
    n_used = (pad_end[-1:] // MOE_TILE).astype(jnp.int32)
    return row_token, tile_expert, n_used, slot.astype(jnp.int32)


def _rope_tables(positions):
    inv_freq = 1.0 / (ROPE_THETA ** (jnp.arange(0, ROT_DIM, 2, dtype=jnp.float32) / ROT_DIM))
    ang = positions.astype(jnp.float32).reshape(-1)[:, None] * inv_freq
    cos, sin = jnp.cos(ang), jnp.sin(ang)
    rest = HEAD_DIM - ROT_DIM
    n = ang.shape[0]
    ct = jnp.concatenate([cos, cos, jnp.ones((n, rest), jnp.float32)], axis=1)
    sa = jnp.concatenate([jnp.zeros_like(sin), sin, jnp.zeros((n, rest), jnp.float32)], axis=1)
    sb = jnp.concatenate([-sin, jnp.zeros_like(sin), jnp.zeros((n, rest), jnp.float32)], axis=1)
    return ct, sa, sb


def _gate_weight(w_in_l, lay):
    d = w_in_l.shape[0]
    kv = lay['kv']
    o = lay['ref_off']['ngate']
    wg = w_in_l[:, o:o + lay['size']['ngate']].reshape(d, kv, NSA_GROUP * 3)
    wg = jnp.pad(wg, ((0, 0), (0, 0), (0, LANES - NSA_GROUP * 3)))
    return wg.reshape(d, kv * LANES).astype(MXU_DTYPE)


def kernel(x, positions, w_in, nsa_cmp_pos, nsa_cmp_w1, nsa_cmp_w2, diff_lambda, diff_subln_g, w_out,
           ln1_g, ln1_b, w_router, b_router, w_gate_up, b_gate_up, w_down, b_down, ln2_g, ln2_b):
    b, s, d = x.shape
    depth = w_in.shape[0]
    n_experts = w_router.shape[2]
    lay = _layout(d)
    alpha = (2 * depth) ** 0.25
    ct, sa, sb = _rope_tables(positions)
    xf = x.reshape(b * s, d).astype(jnp.float32)
    xb = xf.astype(MXU_DTYPE)
    w_out_b = w_out.astype(MXU_DTYPE)
    w_down_b = w_down.astype(MXU_DTYPE)
    for layer in range(depth):
        w_l = w_in[layer]
        w_perm = jnp.concatenate(
            [w_l[:, lay['ref_off'][n]:lay['ref_off'][n] + lay['size'][n]] for n in ROPED + PLAIN],
            axis=1).astype(MXU_DTYPE)
        hp = _project(xb, w_perm, ct, sa, sb, lay['n_roped'])
        gate_logits = _gate_logits(xb, _gate_weight(w_l, lay))

        kcv = _nsa_compress(hp, nsa_cmp_pos[layer], nsa_cmp_w1[layer], nsa_cmp_w2[layer], lay, b, s)
        o_cmp, sel = _nsa_cmp_select(hp, kcv, lay, b, s)
        y_nsa = _nsa_main(hp, sel, o_cmp, gate_logits, lay, b, s)
        lambda_init = 0.8 - 0.6 * math.exp(-0.3 * layer)
        y_diff = _diff_attention(hp, diff_lambda[layer], diff_subln_g[layer], lay, b, s, lambda_init)
        y_moba = _moba_attention(hp, lay, b, s)
        mix = jnp.concatenate([y_nsa, y_diff, y_moba], axis=1)

        x1, top_idx, top_gate = _out_proj_ln_router(
            mix, w_out_b, layer, xf, ln1_g[layer], ln1_b[layer],
            w_router[layer], b_router[layer], alpha)

        row_token, tile_expert, n_used, slot = _route(top_idx[:, :TOP_K], n_experts)
        y = _moe_experts(
            x1, row_token, tile_expert, n_used,
            _regroup_gate_up(w_gate_up, layer), _regroup_bias(b_gate_up[layer]),
            w_down_b, layer, b_down[layer].astype(jnp.float32).reshape(n_experts, 1, d))
        xf, xb = _combine_ln(y, slot, x1, top_gate, ln2_g[layer], ln2_b[layer], alpha)
    return xf.reshape(b, s, d).astype(x.dtype)
```

```python
import functools
import math

import numpy as np
import jax
import jax.numpy as jnp
from jax import lax
from jax.experimental import pallas as pl
from jax.experimental.pallas import tpu as pltpu

HEAD_DIM = 128
ROT_DIM = HEAD_DIM // 4
ROT_HALF = ROT_DIM // 2
ROPE_THETA = 500000.0
ATTN_SCALE = HEAD_DIM ** -0.5
NEG_INF = -1e30
LN_EPS = 1e-5

NSA_GROUP = 4
CMP_LEN = 32
CMP_STRIDE = 16
SEL_LEN = 64
SEL_TOPN = 16
SEL_FORCED = 3
WINDOW = 512
CMP_PER_SEL = SEL_LEN // CMP_STRIDE

DIFF_VDIM = 2 * HEAD_DIM
MOBA_BLOCK = 256
MOBA_TOPK = 3

TOP_K = 4
SWIGLU_LIMIT = 7.0
SWIGLU_ALPHA = 1.702

LANES = 128
MXU_DTYPE = jnp.bfloat16
VMEM_LIMIT = 56 * 1024 * 1024

ROPED = ('nq', 'nkc', 'nks', 'nkw', 'dq', 'dk', 'mq', 'mk')
PLAIN = ('nvc', 'nvs', 'nvw', 'dv', 'mv')
REF_ORDER = ('nq', 'nkc', 'nvc', 'nks', 'nvs', 'nkw', 'nvw', 'ngate', 'dq', 'dk', 'dv', 'mq', 'mk', 'mv')


def _layout(d):
    nsa_h = d // 256
    kv = nsa_h // NSA_GROUP
    diff_h = d // 1024
    moba_h = d // 512
    size = dict(nq=nsa_h * HEAD_DIM, nkc=kv * HEAD_DIM, nvc=kv * HEAD_DIM, nks=kv * HEAD_DIM,
                nvs=kv * HEAD_DIM, nkw=kv * HEAD_DIM, nvw=kv * HEAD_DIM, ngate=3 * nsa_h,
                dq=2 * diff_h * HEAD_DIM, dk=2 * diff_h * HEAD_DIM, dv=diff_h * DIFF_VDIM,
                mq=moba_h * HEAD_DIM, mk=moba_h * HEAD_DIM, mv=moba_h * HEAD_DIM)
    ref_off, o = {}, 0
    for n in REF_ORDER:
        ref_off[n] = o
        o += size[n]
    off, o = {}, 0
    for n in ROPED + PLAIN:
        off[n] = o
        o += size[n]
    n_roped = sum(size[n] for n in ROPED)
    return dict(nsa_h=nsa_h, kv=kv, diff_h=diff_h, moba_h=moba_h, size=size, ref_off=ref_off,
                off=off, n_roped=n_roped, n_cols=o)


def _params(*sem):
    return pltpu.CompilerParams(dimension_semantics=sem, vmem_limit_bytes=VMEM_LIMIT)


def _dot(a, b):
    return jnp.dot(a, b, preferred_element_type=jnp.float32)


def _dot_nt(a, b):
    return lax.dot_general(a, b, (((1,), (1,)), ((), ())), preferred_element_type=jnp.float32)


def _iota(shape, dim):
    return lax.broadcasted_iota(jnp.int32, shape, dim)


def _lane_index(shape):
    return _iota(shape, 1).astype(jnp.float32)


def _first_lane(hit, lane):
    return jnp.min(jnp.where(hit, lane, float(LANES)), axis=-1, keepdims=True)


def _proj_kernel(x_ref, w_ref, ct_ref, sa_ref, sb_ref, o_ref, *, n_rope_blocks, tn):
    j = pl.program_id(1)
    acc = _dot(x_ref[...], w_ref[...])

    @pl.when(j < n_rope_blocks)
    def _():
        ct, sa, sb = ct_ref[...], sa_ref[...], sb_ref[...]
        for c in range(tn // HEAD_DIM):
            a = acc[:, c * HEAD_DIM:(c + 1) * HEAD_DIM]
            r = (a * ct + pltpu.roll(a, ROT_HALF, 1) * sa
                 + pltpu.roll(a, HEAD_DIM - ROT_HALF, 1) * sb)
            o_ref[:, c * HEAD_DIM:(c + 1) * HEAD_DIM] = r.astype(o_ref.dtype)

    @pl.when(j >= n_rope_blocks)
    def _():
        o_ref[...] = acc.astype(o_ref.dtype)


def _project(xb, w, ct, sa, sb, n_roped):
    m, k = xb.shape
    n = w.shape[1]
    tm = min(1024, m)
    tn = next(t for t in (512, 256, 128) if n % t == 0 and n_roped % t == 0)
    kern = functools.partial(_proj_kernel, n_rope_blocks=n_roped // tn, tn=tn)
    return pl.pallas_call(
        kern,
        out_shape=jax.ShapeDtypeStruct((m, n), MXU_DTYPE),
        grid=(m // tm, n // tn),
        in_specs=[pl.BlockSpec((tm, k), lambda i, j: (i, 0)),
                  pl.BlockSpec((k, tn), lambda i, j: (0, j)),
                  pl.BlockSpec((tm, HEAD_DIM), lambda i, j: (i, 0)),
                  pl.BlockSpec((tm, HEAD_DIM), lambda i, j: (i, 0)),
                  pl.BlockSpec((tm, HEAD_DIM), lambda i, j: (i, 0))],
        out_specs=pl.BlockSpec((tm, tn), lambda i, j: (i, j)),
        compiler_params=_params("parallel", "arbitrary"),
    )(xb, w, ct, sa, sb)


def _gate_kernel(x_ref, w_ref, o_ref):
    o_ref[...] = _dot(x_ref[...], w_ref[...])


def _gate_logits(xb, wg):
    m, k = xb.shape
    n = wg.shape[1]
    tm = min(1024, m)
    return pl.pallas_call(
        _gate_kernel,
        out_shape=jax.ShapeDtypeStruct((m, n), jnp.float32),
        grid=(m // tm,),
        in_specs=[pl.BlockSpec((tm, k), lambda i: (i, 0)),
                  pl.BlockSpec((k, n), lambda i: (0, 0))],
        out_specs=pl.BlockSpec((tm, n), lambda i: (i, 0)),
        compiler_params=_params("parallel"),
    )(xb, wg)


EXP2_SCALE = ATTN_SCALE * math.log2(math.e)


def _mask_bias(mask):
    return jnp.where(mask, 0.0, -jnp.inf)


def _lanes(x, n):
    return x if n == LANES else jnp.concatenate([x] * (n // LANES), axis=1)


def _flash_step(s, v, m_ref, acc_ref):
    dv = v.shape[1]
    m_old = m_ref[...]
    m_new = jnp.maximum(m_old, jnp.max(s, axis=-1, keepdims=True))
    p = jnp.exp2((s - _lanes(m_new, s.shape[1])) * EXP2_SCALE)
    alpha = jnp.exp2((m_old - m_new) * EXP2_SCALE)
    acc_ref[:, :dv] = _lanes(alpha, dv) * acc_ref[:, :dv] + _dot(p.astype(v.dtype), v)
    acc_ref[:, dv:] = alpha * acc_ref[:, dv:] + jnp.sum(p, axis=-1, keepdims=True)
    m_ref[...] = m_new


def _flash_result(acc_ref, rows):
    dv = acc_ref.shape[1] - LANES
    denom = jnp.maximum(acc_ref[rows, dv:], 1e-30)
    return acc_ref[rows, :dv] / _lanes(denom, dv)


def _init_state(m_ref, acc_ref):
    m_ref[...] = jnp.full(m_ref.shape, NEG_INF, jnp.float32)
    acc_ref[...] = jnp.zeros(acc_ref.shape, jnp.float32)


def _key_rows(ki, tk):
    return pl.ds(pl.multiple_of(ki * tk, tk), tk)


def _for_tiles(lo, hi, body):
    def wrapped(ki, carry):
        body(ki)
        return carry
    lax.fori_loop(lo, hi, wrapped, 0)


def _diff_kernel(lam_ref, g_ref, q_ref, k_ref, v_ref, o_ref, m_ref, acc_ref, *, tq, lambda_init):
    qi = pl.program_id(2)
    tk = tq
    _init_state(m_ref, acc_ref)

    def tile(ki, masked):
        rows = _key_rows(ki, tk)
        maps = []
        for mp in range(2):
            q = q_ref[:, mp * HEAD_DIM:(mp + 1) * HEAD_DIM]
            maps.append(_dot_nt(q, k_ref[rows, mp * HEAD_DIM:(mp + 1) * HEAD_DIM]))
        if masked:
            bias = _mask_bias(_iota((tq, tk), 1) <= _iota((tq, tk), 0))
            maps = [s + bias for s in maps]
        _flash_step(jnp.concatenate(maps, axis=0), v_ref[rows, :], m_ref, acc_ref)

    _for_tiles(0, qi, lambda ki: tile(ki, False))
    tile(qi, True)

    lv = lam_ref[...]
    lam = (jnp.exp(jnp.sum(lv[0:1] * lv[1:2], axis=-1, keepdims=True))
           - jnp.exp(jnp.sum(lv[2:3] * lv[3:4], axis=-1, keepdims=True)) + lambda_init)
    o = _flash_result(acc_ref, pl.ds(0, tq)) - lam * _flash_result(acc_ref, pl.ds(tq, tq))
    o = o * lax.rsqrt(jnp.mean(jnp.square(o), axis=-1, keepdims=True) + LN_EPS) * g_ref[...]
    o_ref[...] = (o * (1.0 - lambda_init)).astype(o_ref.dtype)


def _diff_attention(hp, lam_vecs, subln_g, lay, b, s, lambda_init):
    hd = lay['diff_h']
    tq = min(512, s)
    nq = s // tq
    qb, kb, vb = (lay['off'][n] // DIFF_VDIM for n in ('dq', 'dk', 'dv'))
    assert all(lay['off'][n] % DIFF_VDIM == 0 for n in ('dq', 'dk', 'dv'))
    kern = functools.partial(_diff_kernel, tq=tq, lambda_init=lambda_init)
    return pl.pallas_call(
        kern, out_shape=jax.ShapeDtypeStruct((b * s, hd * DIFF_VDIM), MXU_DTYPE),
        grid=(b, hd, nq),
        in_specs=[pl.BlockSpec((4, HEAD_DIM), lambda bi, h, qi: (0, 0)),
                  pl.BlockSpec((1, DIFF_VDIM), lambda bi, h, qi: (0, 0)),
                  pl.BlockSpec((tq, DIFF_VDIM), lambda bi, h, qi: (bi * nq + qi, qb + h)),
                  pl.BlockSpec((s, DIFF_VDIM), lambda bi, h, qi: (bi, kb + h)),
                  pl.BlockSpec((s, DIFF_VDIM), lambda bi, h, qi: (bi, vb + h))],
        out_specs=pl.BlockSpec((tq, DIFF_VDIM), lambda bi, h, qi: (bi * nq + qi, h)),
        scratch_shapes=[pltpu.VMEM((2 * tq, LANES), jnp.float32),
                        pltpu.VMEM((2 * tq, DIFF_VDIM + LANES), jnp.float32)],
        compiler_params=_params("parallel", "parallel", "arbitrary"),
    )(lam_vecs.astype(jnp.float32), subln_g.reshape(1, DIFF_VDIM).astype(jnp.float32), hp, hp, hp)


def _kmean_kernel(k_ref, o_ref, *, n_blk):
    k = k_ref[...].astype(jnp.float32).reshape(n_blk, MOBA_BLOCK, HEAD_DIM)
    o_ref[0, 0] = jnp.mean(k, axis=1)


def _moba_kmean(hp, lay, b, s):
    h = lay['moba_h']
    n_blk = s // MOBA_BLOCK
    kb = lay['off']['mk'] // HEAD_DIM
    return pl.pallas_call(
        functools.partial(_kmean_kernel, n_blk=n_blk),
        out_shape=jax.ShapeDtypeStruct((b, h, n_blk, HEAD_DIM), jnp.float32),
        grid=(b, h),
        in_specs=[pl.BlockSpec((s, HEAD_DIM), lambda bi, hi: (bi, kb + hi))],
        out_specs=pl.BlockSpec((1, 1, n_blk, HEAD_DIM), lambda bi, hi: (bi, hi, 0, 0)),
        compiler_params=_params("parallel", "parallel"),
    )(hp)


def _block_expansion(s, block):
    return (np.arange(s)[None, :] // block == np.arange(LANES)[:, None]).astype(np.float32)


def _lane_column(x, n):
    return jnp.sum(jnp.where(_iota(x.shape, 1) == n, x, 0.0), axis=-1, keepdims=True)


def _moba_kernel(q_ref, k_ref, v_ref, km_ref, o_ref, bias_ref, m_ref, acc_ref, *, tq, tk):
    qi = pl.program_id(2)
    q = q_ref[...]
    _init_state(m_ref, acc_ref)
    score = _dot_nt(q, km_ref[0, 0].astype(q.dtype))
    blk = _lane_index((tq, LANES))
    own = ((qi * tq + _iota((tq, LANES), 0)) // MOBA_BLOCK).astype(jnp.float32)
    work = jnp.where(blk < own, score, -jnp.inf)
    sel = jnp.where(blk == own, 1.0, 0.0)
    for r in range(MOBA_TOPK):
        mx = jnp.max(work, axis=-1, keepdims=True)
        first = _first_lane(work == mx, blk)
        pick = blk == first
        sel = jnp.where(pick & (own > r), 1.0, sel)
        work = jnp.where(pick, -jnp.inf, work)
    bias_ref[...] = _mask_bias(sel > 0.5)

    def tile(ki, causal):
        rows = _key_rows(ki, tk)
        cols = []
        for c in range(tk // MOBA_BLOCK):
            col = _lane_column(bias_ref[...], ki * (tk // MOBA_BLOCK) + c)
            cols.append(jnp.broadcast_to(col, (tq, MOBA_BLOCK)))
        bias = cols[0] if len(cols) == 1 else jnp.concatenate(cols, axis=1)
        if causal:
            t = qi * tq + _iota((tq, tk), 0)
            u = ki * tk + _iota((tq, tk), 1)
            bias = jnp.where(u <= t, bias, -jnp.inf)
        _flash_step(_dot_nt(q, k_ref[rows, :]) + bias, v_ref[rows, :], m_ref, acc_ref)

    past = qi * (tq // tk)
    _for_tiles(0, past, lambda ki: tile(ki, False))
    for d in range(tq // tk):
        tile(past + d, True)
    o_ref[...] = _flash_result(acc_ref, pl.ds(0, tq)).astype(o_ref.dtype)


def _moba_attention(hp, lay, b, s):
    h = lay['moba_h']
    n_blk = s // MOBA_BLOCK
    assert s % MOBA_BLOCK == 0 and MOBA_TOPK <= n_blk <= LANES
    km = _moba_kmean(hp, lay, b, s)
    km = jnp.pad(km, ((0, 0), (0, 0), (0, LANES - n_blk), (0, 0)))
    tq = min(1024, s)
    tk = min(512, s)
    nq = s // tq
    assert tq % tk == 0 and tk % MOBA_BLOCK == 0
    qb, kb, vb = (lay['off'][n] // HEAD_DIM for n in ('mq', 'mk', 'mv'))
    return pl.pallas_call(
        functools.partial(_moba_kernel, tq=tq, tk=tk),
        out_shape=jax.ShapeDtypeStruct((b * s, h * HEAD_DIM), MXU_DTYPE),
        grid=(b, h, nq),
        in_specs=[pl.BlockSpec((tq, HEAD_DIM), lambda bi, hi, qi: (bi * nq + qi, qb + hi)),
                  pl.BlockSpec((s, HEAD_DIM), lambda bi, hi, qi: (bi, kb + hi)),
                  pl.BlockSpec((s, HEAD_DIM), lambda bi, hi, qi: (bi, vb + hi)),
                  pl.BlockSpec((1, 1, LANES, HEAD_DIM), lambda bi, hi, qi: (bi, hi, 0, 0))],
        out_specs=pl.BlockSpec((tq, HEAD_DIM), lambda bi, hi, qi: (bi * nq + qi, hi)),
        scratch_shapes=[pltpu.VMEM((tq, LANES), jnp.float32), pltpu.VMEM((tq, LANES), jnp.float32),
                        pltpu.VMEM((tq, HEAD_DIM + LANES), jnp.float32)],
        compiler_params=_params("parallel", "parallel", "arbitrary"),
    )(hp, hp, hp, km)


def _gelu_tanh(x):
    return 0.5 * x * (1.0 + jnp.tanh(math.sqrt(2.0 / math.pi) * (x + 0.044715 * (x * x * x))))


def _compress_kernel(seg_ref, plo_ref, phi_ref, w1lo_ref, w1hi_ref, w2_ref, o_ref, *, n_seg):
    seg = seg_ref[0, 0, 0].astype(jnp.float32)
    lo = _dot((seg + plo_ref[0]).astype(MXU_DTYPE), w1lo_ref[0])
    hi = _dot((seg + phi_ref[0]).astype(MXU_DTYPE), w1hi_ref[0])
    pre = lo + pltpu.roll(hi, n_seg - 1, 0)
    o_ref[0, 0, 0] = _dot(_gelu_tanh(pre).astype(MXU_DTYPE), w2_ref[0]).astype(o_ref.dtype)


def _nsa_compress(hp, cmp_pos, cmp_w1, cmp_w2, lay, b, s):
    kv = lay['kv']
    n_seg = s // CMP_STRIDE
    half = CMP_STRIDE * HEAD_DIM

    def segments(name):
        o = lay['off'][name]
        t = hp[:, o:o + kv * HEAD_DIM].reshape(b, n_seg, CMP_STRIDE, kv, HEAD_DIM)
        return t.transpose(0, 3, 1, 2, 4).reshape(b, kv, n_seg, half)

    seg = jnp.stack([segments('nkc'), segments('nvc')], axis=1)
    pos = cmp_pos.astype(jnp.float32).reshape(2, CMP_LEN * HEAD_DIM)
    plo = pos[:, :half].reshape(2, 1, half)
    phi = pos[:, half:].reshape(2, 1, half)
    w1 = cmp_w1.astype(MXU_DTYPE)
    w1lo, w1hi = w1[:, :half], w1[:, half:]
    w2 = cmp_w2.astype(MXU_DTYPE)
    return pl.pallas_call(
        functools.partial(_compress_kernel, n_seg=n_seg),
        out_shape=jax.ShapeDtypeStruct((b, 2, kv, n_seg, HEAD_DIM), MXU_DTYPE),
        grid=(b, 2, kv),
        in_specs=[pl.BlockSpec((1, 1, 1, n_seg, half), lambda bi, c, k: (bi, c, k, 0, 0)),
                  pl.BlockSpec((1, 1, half), lambda bi, c, k: (c, 0, 0)),
                  pl.BlockSpec((1, 1, half), lambda bi, c, k: (c, 0, 0)),
                  pl.BlockSpec((1, half, HEAD_DIM), lambda bi, c, k: (c, 0, 0)),
                  pl.BlockSpec((1, half, HEAD_DIM), lambda bi, c, k: (c, 0, 0)),
                  pl.BlockSpec((1, HEAD_DIM, HEAD_DIM), lambda bi, c, k: (c, 0, 0))],
        out_specs=pl.BlockSpec((1, 1, 1, n_seg, HEAD_DIM), lambda bi, c, k: (bi, c, k, 0, 0)),
        compiler_params=_params("parallel", "parallel", "parallel"),
    )(seg, plo, phi, w1lo, w1hi, w2)


def _nsa_cmp_kernel(q_ref, kc_ref, vc_ref, o_ref, sel_ref, *, tq, n_sel, n_cmp):
    qi = pl.program_id(2)
    width = CMP_PER_SEL * LANES
    t = qi * tq + _iota((tq, width), 0)
    pos = _iota((tq, width), 1)
    m_idx = pos % LANES
    n_idx = CMP_PER_SEL * m_idx + pos // LANES
    valid = (m_idx < n_sel) & (n_idx < n_cmp) & (n_idx * CMP_STRIDE + CMP_LEN - 1 <= t)
    kc = kc_ref[0, 0, 0]
    vc = vc_ref[0, 0, 0]
    p_grp = jnp.zeros((tq, width), jnp.float32)
    for g in range(NSA_GROUP):
        q = q_ref[:, g * HEAD_DIM:(g + 1) * HEAD_DIM]
        s = jnp.where(valid, _dot_nt(q, kc) * ATTN_SCALE, NEG_INF)
        p = jnp.where(valid, jnp.exp(s - jnp.max(s, axis=-1, keepdims=True)), 0.0)
        p = p / jnp.maximum(jnp.sum(p, axis=-1, keepdims=True), 1e-30)
        o_ref[:, g * HEAD_DIM:(g + 1) * HEAD_DIM] = _dot(p.astype(vc.dtype), vc)
        p_grp = p_grp + p

    slabs = [p_grp[:, r * LANES:(r + 1) * LANES] for r in range(CMP_PER_SEL)]
    blk = _lane_index((tq, LANES))
    prev = jnp.where(blk == 0, 0.0, pltpu.roll(slabs[CMP_PER_SEL - 1], 1, 1))
    imp = prev
    for r in range(CMP_PER_SEL):
        imp = imp + slabs[r]
    cur = ((qi * tq + _iota((tq, LANES), 0)) // SEL_LEN).astype(jnp.float32)
    forced = (blk == 0) | (blk == cur) | (blk == cur - 1)
    work = jnp.where(forced | (blk > cur), -jnp.inf, imp)
    sel = jnp.where(forced, 1.0, 0.0)
    for _ in range(SEL_TOPN - SEL_FORCED):
        mx = jnp.max(work, axis=-1, keepdims=True)
        first = _first_lane(work == mx, blk)
        pick = blk == first
        sel = jnp.where(pick, 1.0, sel)
        work = jnp.where(pick, -jnp.inf, work)
    sel_ref[0, 0] = jnp.where(blk <= cur, sel, 0.0).astype(sel_ref.dtype)


def _nsa_cmp_select(hp, kcv, lay, b, s):
    kv = lay['kv']
    n_seg = s // CMP_STRIDE
    n_sel = s // SEL_LEN
    n_cmp = (s - CMP_LEN) // CMP_STRIDE + 1
    assert SEL_TOPN <= n_sel <= LANES
    width = CMP_PER_SEL * LANES
    kcp = kcv.reshape(b, 2, kv, n_sel, CMP_PER_SEL, HEAD_DIM).transpose(0, 1, 2, 4, 3, 5)
    kcp = jnp.pad(kcp, ((0, 0),) * 4 + ((0, LANES - n_sel), (0, 0))).reshape(b, 2, kv, width, HEAD_DIM)
    tq = min(512, s)
    nq = s // tq
    gw = NSA_GROUP * HEAD_DIM
    return pl.pallas_call(
        functools.partial(_nsa_cmp_kernel, tq=tq, n_sel=n_sel, n_cmp=n_cmp),
        out_shape=(jax.ShapeDtypeStruct((b * s, lay['nsa_h'] * HEAD_DIM), jnp.float32),
                   jax.ShapeDtypeStruct((b, kv, s, LANES), MXU_DTYPE)),
        grid=(b, kv, nq),
        in_specs=[pl.BlockSpec((tq, gw), lambda bi, k, qi: (bi * nq + qi, k)),
                  pl.BlockSpec((1, 1, 1, width, HEAD_DIM), lambda bi, k, qi: (bi, 0, k, 0, 0)),
                  pl.BlockSpec((1, 1, 1, width, HEAD_DIM), lambda bi, k, qi: (bi, 1, k, 0, 0))],
        out_specs=(pl.BlockSpec((tq, gw), lambda bi, k, qi: (bi * nq + qi, k)),
                   pl.BlockSpec((1, 1, tq, LANES), lambda bi, k, qi: (bi, k, qi, 0))),
        compiler_params=_params("parallel", "parallel", "parallel"),
    )(hp, kcp, kcp)


def _nsa_main_kernel(q_ref, ks_ref, vs_ref, kw_ref, vw_ref, sel_ref, e_ref, oc_ref, gl_ref, o_ref,
                     ms_ref, as_ref, mw_ref, aw_ref, *, tq, tk):
    qi = pl.program_id(2)
    last = (qi * tq) // tk
    _init_state(ms_ref, as_ref)
    _init_state(mw_ref, aw_ref)

    q_all = jnp.concatenate([q_ref[:, g * HEAD_DIM:(g + 1) * HEAD_DIM] for g in range(NSA_GROUP)], axis=0)

    def scores(k, bias):
        keys = k.shape[0]
        s = _dot_nt(q_all, k).reshape(NSA_GROUP, tq, keys) + bias[None]
        return s.reshape(NSA_GROUP * tq, keys)

    def selected(ki, causal):
        rows = _key_rows(ki, tk)
        ok = _dot(sel_ref[0, 0], e_ref[ki]) > 0.5
        if causal:
            ok = ok & (ki * tk + _iota((tq, tk), 1) <= qi * tq + _iota((tq, tk), 0))
        _flash_step(scores(ks_ref[rows, :], _mask_bias(ok)), vs_ref[rows, :], ms_ref, as_ref)

    _for_tiles(0, last, lambda ki: selected(ki, False))
    selected(last, True)

    span = WINDOW + tq
    start = pl.multiple_of(jnp.maximum(qi * tq - WINDOW, 0), tq)
    t = qi * tq + _iota((tq, span), 0)
    u = start + _iota((tq, span), 1)
    b_win = _mask_bias((u <= t) & (u > t - WINDOW))
    slab = pl.ds(start, span)
    _flash_step(scores(kw_ref[slab, :], b_win), vw_ref[slab, :], mw_ref, aw_ref)

    gate = 1.0 / (1.0 + jnp.exp(-gl_ref[...]))
    for g in range(NSA_GROUP):
        o_cmp = oc_ref[:, g * HEAD_DIM:(g + 1) * HEAD_DIM]
        o_slc = _flash_result(as_ref, pl.ds(g * tq, tq))
        o_win = _flash_result(aw_ref, pl.ds(g * tq, tq))
        out = (gate[:, 3 * g:3 * g + 1] * o_cmp + gate[:, 3 * g + 1:3 * g + 2] * o_slc
               + gate[:, 3 * g + 2:3 * g + 3] * o_win)
        o_ref[:, g * HEAD_DIM:(g + 1) * HEAD_DIM] = out.astype(o_ref.dtype)


def _nsa_main(hp, sel, o_cmp, gate_logits, lay, b, s):
    kv = lay['kv']
    tq = min(256, s)
    tk = min(512, s)
    nq, nk = s // tq, s // tk
    assert tk % tq == 0 and WINDOW % tq == 0 and s >= WINDOW + tq
    gw = NSA_GROUP * HEAD_DIM
    ksb, vsb, kwb, vwb = (lay['off'][n] // HEAD_DIM for n in ('nks', 'nvs', 'nkw', 'nvw'))
    expand = _block_expansion(s, SEL_LEN).reshape(LANES, nk, tk).transpose(1, 0, 2)
    expand = jnp.asarray(expand, MXU_DTYPE)
    row = lambda bi, k, qi: (bi * nq + qi, k)
    return pl.pallas_call(
        functools.partial(_nsa_main_kernel, tq=tq, tk=tk),
        out_shape=jax.ShapeDtypeStruct((b * s, lay['nsa_h'] * HEAD_DIM), MXU_DTYPE),
        grid=(b, kv, nq),
        in_specs=[pl.BlockSpec((tq, gw), row),
                  pl.BlockSpec((s, HEAD_DIM), lambda bi, k, qi: (bi, ksb + k)),
                  pl.BlockSpec((s, HEAD_DIM), lambda bi, k, qi: (bi, vsb + k)),
                  pl.BlockSpec((s, HEAD_DIM), lambda bi, k, qi: (bi, kwb + k)),
                  pl.BlockSpec((s, HEAD_DIM), lambda bi, k, qi: (bi, vwb + k)),
                  pl.BlockSpec((1, 1, tq, LANES), lambda bi, k, qi: (bi, k, qi, 0)),
                  pl.BlockSpec((nk, LANES, tk), lambda bi, k, qi: (0, 0, 0)),
                  pl.BlockSpec((tq, gw), row),
                  pl.BlockSpec((tq, LANES), row)],
        out_specs=pl.BlockSpec((tq, gw), row),
        scratch_shapes=[pltpu.VMEM((NSA_GROUP * tq, LANES), jnp.float32),
                        pltpu.VMEM((NSA_GROUP * tq, HEAD_DIM + LANES), jnp.float32)] * 2,
        compiler_params=_params("parallel", "parallel", "arbitrary"),
    )(hp, hp, hp, hp, hp, sel, expand, o_cmp, gate_logits)


def _layer_norm(y, g, b):
    mu = jnp.mean(y, axis=-1, keepdims=True)
    var = jnp.mean(jnp.square(y - mu), axis=-1, keepdims=True)
    return (y - mu) * lax.rsqrt(var + LN_EPS) * g + b


def _residual_proj_kernel(mix_ref, w_ref, x_ref, o_ref, *, alpha):
    o_ref[...] = alpha * x_ref[...] + _dot(mix_ref[...], w_ref[0])


def _ln_router_kernel(y_ref, g_ref, b_ref, wr_ref, br_ref, x1_ref, idx_ref, gate_ref, *, n_experts):
    x1 = _layer_norm(y_ref[...], g_ref[...], b_ref[...])
    x1_ref[...] = x1
    logits = _dot(x1.astype(MXU_DTYPE), wr_ref[...]) + br_ref[...]
    lane = _lane_index(logits.shape)
    work = jnp.where(lane < n_experts, logits, -jnp.inf)
    idx_out = jnp.zeros(logits.shape, jnp.float32)
    val_out = jnp.zeros(logits.shape, jnp.float32)
    top = None
    for r in range(TOP_K):
        mx = jnp.max(work, axis=-1, keepdims=True)
        first = _first_lane(work == mx, lane)
        top = mx if top is None else top
        idx_out = jnp.where(lane == r, first, idx_out)
        val_out = jnp.where(lane == r, jnp.exp(mx - top), val_out)
        work = jnp.where(lane == first, -jnp.inf, work)
    idx_ref[...] = idx_out.astype(jnp.int32)
    gate_ref[...] = val_out / jnp.sum(val_out, axis=-1, keepdims=True)


def _out_proj_ln_router(mix, w_out_stack, layer, x, ln_g, ln_b, w_router, b_router, alpha):
    t, kdim = mix.shape
    d = w_out_stack.shape[2]
    n_experts = w_router.shape[1]
    tm = min(1024, t)
    tn = min(512, d)
    y = pl.pallas_call(
        functools.partial(_residual_proj_kernel, alpha=alpha),
        out_shape=jax.ShapeDtypeStruct((t, d), jnp.float32),
        grid=(t // tm, d // tn),
        in_specs=[pl.BlockSpec((tm, kdim), lambda i, j: (i, 0)),
                  pl.BlockSpec((1, kdim, tn), lambda i, j: (layer, 0, j)),
                  pl.BlockSpec((tm, tn), lambda i, j: (i, j))],
        out_specs=pl.BlockSpec((tm, tn), lambda i, j: (i, j)),
        compiler_params=_params("parallel", "arbitrary"),
    )(mix, w_out_stack, x)

    tr = min(256, t)
    wr = jnp.pad(w_router, ((0, 0), (0, LANES - n_experts))).astype(MXU_DTYPE)
    br = jnp.pad(b_router.astype(jnp.float32), (0, LANES - n_experts)).reshape(1, LANES)
    row = lambda i: (i, 0)
    const = lambda i: (0, 0)
    return pl.pallas_call(
        functools.partial(_ln_router_kernel, n_experts=n_experts),
        out_shape=(jax.ShapeDtypeStruct((t, d), jnp.float32),
                   jax.ShapeDtypeStruct((t, LANES), jnp.int32),
                   jax.ShapeDtypeStruct((t, LANES), jnp.float32)),
        grid=(t // tr,),
        in_specs=[pl.BlockSpec((tr, d), row),
                  pl.BlockSpec((1, d), const), pl.BlockSpec((1, d), const),
                  pl.BlockSpec((d, LANES), const), pl.BlockSpec((1, LANES), const)],
        out_specs=(pl.BlockSpec((tr, d), row), pl.BlockSpec((tr, LANES), row),
                   pl.BlockSpec((tr, LANES), row)),
        compiler_params=_params("parallel"),
    )(y, ln_g.reshape(1, d), ln_b.reshape(1, d), wr, br)


MOE_TILE = 256
COMBINE_TILE = 128
GATHER_UNROLL = 8


GLU_GROUP = 2 * LANES


def _regroup_kernel(w_ref, p_ref, o_ref):
    w = w_ref[0, 0].astype(MXU_DTYPE)
    for c in range(w.shape[1] // GLU_GROUP):
        cols = slice(c * GLU_GROUP, (c + 1) * GLU_GROUP)
        o_ref[0, :, cols] = _dot(w[:, cols], p_ref[...]).astype(o_ref.dtype)


def _regroup_gate_up(w_gate_up, layer):
    _, e, d, f2 = w_gate_up.shape
    j = np.arange(GLU_GROUP)
    src = np.where(j < LANES, 2 * j, 2 * (j - LANES) + 1)
    perm = np.zeros((GLU_GROUP, GLU_GROUP), np.float32)
    perm[src, j] = 1.0
    tk = min(1024, d)
    return pl.pallas_call(
        _regroup_kernel,
        out_shape=jax.ShapeDtypeStruct((e, d, f2), MXU_DTYPE),
        grid=(e, d // tk),
        in_specs=[pl.BlockSpec((1, 1, tk, f2), lambda ei, ki: (layer, ei, ki, 0)),
                  pl.BlockSpec((GLU_GROUP, GLU_GROUP), lambda ei, ki: (0, 0))],
        out_specs=pl.BlockSpec((1, tk, f2), lambda ei, ki: (ei, ki, 0)),
        compiler_params=_params("parallel", "parallel"),
    )(w_gate_up, jnp.asarray(perm, MXU_DTYPE))


def _regroup_bias(b_gate_up):
    e, f2 = b_gate_up.shape
    b = b_gate_up.astype(jnp.float32).reshape(e, f2 // GLU_GROUP, LANES, 2)
    return b.transpose(0, 1, 3, 2).reshape(e, 1, f2)


def _moe_kernel(rt_ref, te_ref, nu_ref, x_hbm, wgu_ref, bgu_ref, wd_ref, bd_ref, o_ref, xbuf, sem):
    i = pl.program_id(0)
    n_used = nu_ref[0]

    def row_copy(tile, slot, r):
        tok = rt_ref[tile * MOE_TILE + r]
        return pltpu.make_async_copy(x_hbm.at[pl.ds(tok, 1)], xbuf.at[slot, pl.ds(r, 1)], sem.at[slot])

    @pl.when(i == 0)
    def _():
        def body(r, carry):
            row_copy(0, 0, r).start()
            return carry
        lax.fori_loop(0, MOE_TILE, body, 0, unroll=GATHER_UNROLL)

    def expert_tile(prefetch):
        slot = i % 2
        pltpu.make_async_copy(x_hbm.at[pl.ds(0, MOE_TILE)], xbuf.at[slot], sem.at[slot]).wait()
        if prefetch:
            for r in range(MOE_TILE):
                row_copy(i + 1, 1 - slot, r).start()
        x = xbuf[slot].astype(MXU_DTYPE)
        hid = _dot(x, wgu_ref[0]) + bgu_ref[0]
        acts = []
        for c in range(hid.shape[1] // GLU_GROUP):
            h_glu = jnp.minimum(hid[:, c * GLU_GROUP:c * GLU_GROUP + LANES], SWIGLU_LIMIT)
            h_lin = jnp.clip(hid[:, c * GLU_GROUP + LANES:(c + 1) * GLU_GROUP], -SWIGLU_LIMIT, SWIGLU_LIMIT)
            acts.append(h_glu * (1.0 / (1.0 + jnp.exp(-SWIGLU_ALPHA * h_glu))) * (h_lin + 1.0))
        act = jnp.concatenate(acts, axis=1)
        o_ref[...] = _dot(act.astype(MXU_DTYPE), wd_ref[0, 0]) + bd_ref[0]

    @pl.when(i + 1 < n_used)
    def _():
        expert_tile(True)

    @pl.when(i + 1 == n_used)
    def _():
        expert_tile(False)

    @pl.when(i >= n_used)
    def _():
        o_ref[...] = jnp.zeros(o_ref.shape, o_ref.dtype)


def _moe_experts(x1, row_token, tile_expert, n_used, wgu, bgu, wd_stack, layer, bd):
    t, d = x1.shape
    f2 = wgu.shape[2]
    n_tiles = tile_expert.shape[0]
    wmap = lambda i, rt, te, nu: (te[i], 0, 0)
    grid_spec = pltpu.PrefetchScalarGridSpec(
        num_scalar_prefetch=3,
        grid=(n_tiles,),
        in_specs=[pl.BlockSpec(memory_space=pl.ANY),
                  pl.BlockSpec((1, d, f2), wmap), pl.BlockSpec((1, 1, f2), wmap),
                  pl.BlockSpec((1, 1, f2 // 2, d), lambda i, rt, te, nu: (layer, te[i], 0, 0)),
                  pl.BlockSpec((1, 1, d), wmap)],
        out_specs=pl.BlockSpec((MOE_TILE, d), lambda i, rt, te, nu: (i, 0)),
        scratch_shapes=[pltpu.VMEM((2, MOE_TILE, d), jnp.float32), pltpu.SemaphoreType.DMA((2,))])
    return pl.pallas_call(
        _moe_kernel,
        out_shape=jax.ShapeDtypeStruct((n_tiles * MOE_TILE, d), jnp.float32),
        grid_spec=grid_spec, compiler_params=_params("arbitrary"),
    )(row_token, tile_expert, n_used, x1, wgu, bgu, wd_stack, bd)


def _combine_kernel(pos_ref, y_hbm, x_ref, gate_ref, g_ref, b_ref, o_ref, ob_ref, ybuf, sem, *, alpha):
    i = pl.program_id(0)
    n_rows = COMBINE_TILE * TOP_K

    def row_copy(tile, slot, r):
        src = pos_ref[tile * n_rows + r]
        return pltpu.make_async_copy(y_hbm.at[pl.ds(src, 1)], ybuf.at[slot, pl.ds(r, 1)], sem.at[slot])

    @pl.when(i == 0)
    def _():
        def body(r, carry):
            row_copy(0, 0, r).start()
            return carry
        lax.fori_loop(0, n_rows, body, 0, unroll=GATHER_UNROLL)

    def token_tile(prefetch):
        slot = i % 2
        pltpu.make_async_copy(y_hbm.at[pl.ds(0, n_rows)], ybuf.at[slot], sem.at[slot]).wait()
        if prefetch:
            for r in range(n_rows):
                row_copy(i + 1, 1 - slot, r).start()
        gate = gate_ref[...]
        ffn = jnp.zeros(x_ref.shape, jnp.float32)
        for k in range(TOP_K):
            ffn = ffn + gate[:, k:k + 1] * ybuf[slot, pl.ds(k * COMBINE_TILE, COMBINE_TILE)]
        x2 = _layer_norm(alpha * x_ref[...] + ffn, g_ref[...], b_ref[...])
        o_ref[...] = x2
        ob_ref[...] = x2.astype(ob_ref.dtype)

    last = pl.num_programs(0) - 1

    @pl.when(i < last)
    def _():
        token_tile(True)

    @pl.when(i == last)
    def _():
        token_tile(False)


def _combine_ln(y, pos, x1, gate, ln_g, ln_b, alpha):
    t, d = x1.shape
    row = lambda i, p: (i, 0)
    const = lambda i, p: (0, 0)
    grid_spec = pltpu.PrefetchScalarGridSpec(
        num_scalar_prefetch=1,
        grid=(t // COMBINE_TILE,),
        in_specs=[pl.BlockSpec(memory_space=pl.ANY),
                  pl.BlockSpec((COMBINE_TILE, d), row), pl.BlockSpec((COMBINE_TILE, LANES), row),
                  pl.BlockSpec((1, d), const), pl.BlockSpec((1, d), const)],
        out_specs=(pl.BlockSpec((COMBINE_TILE, d), row), pl.BlockSpec((COMBINE_TILE, d), row)),
        scratch_shapes=[pltpu.VMEM((2, COMBINE_TILE * TOP_K, d), jnp.float32),
                        pltpu.SemaphoreType.DMA((2,))])
    return pl.pallas_call(
        functools.partial(_combine_kernel, alpha=alpha),
        out_shape=(jax.ShapeDtypeStruct((t, d), jnp.float32), jax.ShapeDtypeStruct((t, d), MXU_DTYPE)),
        grid_spec=grid_spec, compiler_params=_params("arbitrary"),
    )(pos, y, x1, gate, ln_g.reshape(1, d), ln_b.reshape(1, d))


def _route(top_idx, n_experts):
    t = top_idx.shape[0]
    n_assign = t * TOP_K
    n_tiles = n_assign // MOE_TILE + n_experts
    expert = top_idx.reshape(n_assign)
    onehot = (expert[:, None] == jnp.arange(n_experts, dtype=jnp.int32)[None, :]).astype(jnp.int32)
    running = jnp.cumsum(onehot, axis=0)
    counts = running[-1]
    rank = jnp.take_along_axis(running, expert[:, None], axis=1)[:, 0] - 1
    padded = (counts + MOE_TILE - 1) // MOE_TILE * MOE_TILE
    pad_end = jnp.cumsum(padded)
    dest = (pad_end - padded)[expert] + rank
    slot = dest.reshape(t // COMBINE_TILE, COMBINE_TILE, TOP_K).transpose(0, 2, 1).reshape(n_assign)
    token = jnp.arange(n_assign, dtype=jnp.int32) // TOP_K
    row_token = jnp.zeros((n_tiles * MOE_TILE,), jnp.int32).at[dest].set(token)
    tile_start = jnp.arange(n_tiles, dtype=jnp.int32) * MOE_TILE
    tile_expert = jnp.minimum(
        jnp.sum((pad_end[None, :] <= tile_start[:, None]).astype(jnp.int32), axis=1), n_experts - 1)
    n_used = (pad_end[-1:] // MOE_TILE).astype(jnp.int32)
    return row_token, tile_expert, n_used, slot.astype(jnp.int32)


def _rope_tables(positions):
    inv_freq = 1.0 / (ROPE_THETA ** (jnp.arange(0, ROT_DIM, 2, dtype=jnp.float32) / ROT_DIM))
    ang = positions.astype(jnp.float32).reshape(-1)[:, None] * inv_freq
    cos, sin = jnp.cos(ang), jnp.sin(ang)
    rest = HEAD_DIM - ROT_DIM
    n = ang.shape[0]
    ct = jnp.concatenate([cos, cos, jnp.ones((n, rest), jnp.float32)], axis=1)
    sa = jnp.concatenate([jnp.zeros_like(sin), sin, jnp.zeros((n, rest), jnp.float32)], axis=1)
    sb = jnp.concatenate([-sin, jnp.zeros_like(sin), jnp.zeros((n, rest), jnp.float32)], axis=1)
    return ct, sa, sb


def _gate_weight(w_in_l, lay):
    d = w_in_l.shape[0]
    kv = lay['kv']
    o = lay['ref_off']['ngate']
    wg = w_in_l[:, o:o + lay['size']['ngate']].reshape(d, kv, NSA_GROUP * 3)
    wg = jnp.pad(wg, ((0, 0), (0, 0), (0, LANES - NSA_GROUP * 3)))
    return wg.reshape(d, kv * LANES).astype(MXU_DTYPE)


def kernel(x, positions, w_in, nsa_cmp_pos, nsa_cmp_w1, nsa_cmp_w2, diff_lambda, diff_subln_g, w_out,
           ln1_g, ln1_b, w_router, b_router, w_gate_up, b_gate_up, w_down, b_down, ln2_g, ln2_b):
    b, s, d = x.shape
    depth = w_in.shape[0]
    n_experts = w_router.shape[2]
    lay = _layout(d)
    alpha = (2 * depth) ** 0.25
    ct, sa, sb = _rope_tables(positions)
    xf = x.reshape(b * s, d).astype(jnp.float32)
    xb = xf.astype(MXU_DTYPE)
    w_out_b = w_out.astype(MXU_DTYPE)
    w_down_b = w_down.astype(MXU_DTYPE)
    for layer in range(depth):
        w_l = w_in[layer]
        w_perm = jnp.concatenate(
            [w_l[:, lay['ref_off'][n]:lay['ref_off'][n] + lay['size'][n]] for n in ROPED + PLAIN],
            axis=1).astype(MXU_DTYPE)
        hp = _project(xb, w_perm, ct, sa, sb, lay['n_roped'])
        gate_logits = _gate_logits(xb, _gate_weight(w_l, lay))

        kcv = _nsa_compress(hp, nsa_cmp_pos[layer], nsa_cmp_w1[layer], nsa_cmp_w2[layer], lay, b, s)
        o_cmp, sel = _nsa_cmp_select(hp, kcv, lay, b, s)
        y_nsa = _nsa_main(hp, sel, o_cmp, gate_logits, lay, b, s)
        lambda_init = 0.8 - 0.6 * math.exp(-0.3 * layer)
        y_diff = _diff_attention(hp, diff_lambda[layer], diff_subln_g[layer], lay, b, s, lambda_init)
        y_moba = _moba_attention(hp, lay, b, s)
        mix = jnp.concatenate([y_nsa, y_diff, y_moba], axis=1)

        x1, top_idx, top_gate = _out_proj_ln_router(
            mix, w_out_b, layer, xf, ln1_g[layer], ln1_b[layer],
            w_router[layer], b_router[layer], alpha)

        row_token, tile_expert, n_used, slot = _route(top_idx[:, :TOP_K], n_experts)
        y = _moe_experts(
            x1, row_token, tile_expert, n_used,
            _regroup_gate_up(w_gate_up, layer), _regroup_bias(b_gate_up[layer]),
            w_down_b, layer, b_down[layer].astype(jnp.float32).reshape(n_experts, 1, d))
        xf, xb = _combine_ln(y, slot, x1, top_gate, ln2_g[layer], ln2_b[layer], alpha)
    return xf.reshape(b, s, d).astype(x.dtype)
```
